```python
import jax, jax.numpy as jnp
from jax import lax
import numpy as np

D_MODEL = 2048
BATCH = 4
SEQ = 2048
DEPTH = 2

N_BRANCH = 4
BRANCH_WIDTH = D_MODEL // 4
LRU_WIDTH = BRANCH_WIDTH
LRU_BLOCKS = 8
LRU_BLOCK_DIM = LRU_WIDTH // LRU_BLOCKS
LRU_CONV = 4
LRU_C = 8.0
POOL_WIDTH = BRANCH_WIDTH
POOL_WINDOWS = (2, 4, 8, 16)
POOL_GROUPS = 4
POOL_GROUP_DIM = POOL_WIDTH // POOL_GROUPS
SCONV_WIDTH = BRANCH_WIDTH
SCONV_K = 3
ATTN_HEADS = 8
HEAD_DIM = BRANCH_WIDTH // ATTN_HEADS
ATTN_WIDTH = ATTN_HEADS * HEAD_DIM
Q_BLOCK = 128
D_FF = 4 * D_MODEL
EPS = 1e-6
IN_SIZES = (LRU_WIDTH, POOL_WIDTH, 3 * SCONV_WIDTH, 3 * ATTN_WIDTH, ATTN_HEADS, N_BRANCH * D_MODEL)
N_IN = sum(IN_SIZES)

kernel_name = "hybrid_gated_parallel_mixers"


def _split_points():
    return [int(v) for v in np.cumsum(IN_SIZES)[:-1]]


def rms_norm(x, g):
    xf = x.astype(jnp.float32)
    y = xf * lax.rsqrt(jnp.mean(xf * xf, axis=-1, keepdims=True) + EPS)
    return (y * g.astype(jnp.float32)).astype(x.dtype)


def causal_depthwise_conv(x, w):
    k_width = w.shape[0]
    s = x.shape[1]
    xp = jnp.pad(x, ((0, 0), (k_width - 1, 0), (0, 0)))
    return sum(w[k] * xp[:, k:k + s] for k in range(k_width))


def rglru_branch(xa, conv_w, conv_b, wr, br, wi, bi, lam):
    b, s, _ = xa.shape
    u = causal_depthwise_conv(xa, conv_w) + conv_b
    ub = u.reshape(b, s, LRU_BLOCKS, LRU_BLOCK_DIM)
    r = jax.nn.sigmoid(jnp.einsum('bshi,hij->bshj', ub, wr).reshape(b, s, LRU_WIDTH) + br)
    gi = jax.nn.sigmoid(jnp.einsum('bshi,hij->bshj', ub, wi).reshape(b, s, LRU_WIDTH) + bi)
    log_a = (-LRU_C * r.astype(jnp.float32)) * jax.nn.softplus(-lam.astype(jnp.float32))
    a = jnp.exp(log_a)
    inp = jnp.sqrt(-jnp.expm1(2.0 * log_a)) * (gi * u).astype(jnp.float32)

    def combine(c1, c2):
        a1, b1 = c1
        a2, b2 = c2
        return a1 * a2, a2 * b1 + b2

    _, h = lax.associative_scan(combine, (a, inp), axis=1)
    return h.astype(xa.dtype)


def pool_branch(xp, w_grp, scale):
    b, s, _ = xp.shape
    xf = xp.astype(jnp.float32)
    csum = jnp.pad(jnp.cumsum(xf, axis=1), ((0, 0), (1, 0), (0, 0)))
    t = jnp.arange(s)
    outs = []
    for gidx, win in enumerate(POOL_WINDOWS):
        sl = slice(gidx * POOL_GROUP_DIM, (gidx + 1) * POOL_GROUP_DIM)
        start = jnp.maximum(t + 1 - win, 0)
        win_sum = csum[:, 1:, sl] - csum[:, start, sl]
        count = jnp.minimum(t + 1, win).astype(jnp.float32)[None, :, None]
        outs.append(win_sum / count - xf[:, :, sl])
    pooled = jnp.stack(outs, axis=2)
    mixed = jnp.einsum('bsgi,gij->bsgj', pooled, w_grp.astype(jnp.float32)).reshape(b, s, POOL_WIDTH)
    return (mixed * scale.astype(jnp.float32)).astype(xp.dtype)


def shortconv_branch(xsc, w):
    gate_b, gate_c, xc = jnp.split(xsc, 3, axis=-1)
    return gate_b * causal_depthwise_conv(gate_c * xc, w)


def forgetting_attention(qkv, f_logit, f_bias, q_g, k_g):
    b, s, _ = qkv.shape
    q, k, v = jnp.split(qkv, 3, axis=-1)
    q = rms_norm(q.reshape(b, s, ATTN_HEADS, HEAD_DIM), q_g)
    k = rms_norm(k.reshape(b, s, ATTN_HEADS, HEAD_DIM), k_g)
    v = v.reshape(b, s, ATTN_HEADS, HEAD_DIM)
    log_f = jax.nn.log_sigmoid((f_logit + f_bias).astype(jnp.float32))
    cum = jnp.cumsum(log_f, axis=1).transpose(0, 2, 1)
    n_blk = s // Q_BLOCK
    qb = q.reshape(b, n_blk, Q_BLOCK, ATTN_HEADS, HEAD_DIM).transpose(1, 0, 2, 3, 4)
    cqb = cum.reshape(b, ATTN_HEADS, n_blk, Q_BLOCK).transpose(2, 0, 1, 3)
    key_pos = jnp.arange(s)
    scale = HEAD_DIM ** -0.5

    def one_block(args):
        qi, cqi, blk = args
        logits = jnp.einsum('bqhd,bkhd->bhqk', qi, k).astype(jnp.float32) * scale
        logits = logits + (cqi[..., :, None] - cum[..., None, :])
        q_pos = blk * Q_BLOCK + jnp.arange(Q_BLOCK)
        logits = jnp.where(key_pos[None, :] <= q_pos[:, None], logits, -jnp.inf)
        p = jax.nn.softmax(logits, axis=-1)
        return jnp.einsum('bhqk,bkhd->bqhd', p.astype(v.dtype), v)

    out = lax.map(one_block, (qb, cqb, jnp.arange(n_blk)))
    return out.transpose(1, 0, 2, 3, 4).reshape(b, s, ATTN_WIDTH)


def setup_inputs(seed: int = 0) -> dict:
    key = jax.random.key(seed)
    ks = jax.random.split(key, 24)
    L, D, W = DEPTH, D_MODEL, BRANCH_WIDTH
    nrm = lambda k, shape, fan: jax.random.normal(k, shape, jnp.float32) * (fan ** -0.5)
    u = jax.random.uniform(ks[9], (L, LRU_WIDTH), jnp.float32, 0.9, 0.999)
    a0 = u ** (1.0 / LRU_C)
    return {
        "x": jax.random.normal(ks[0], (BATCH, SEQ, D), jnp.float32),
        "norm_mix_g": 1.0 + 0.1 * jax.random.normal(ks[1], (L, D), jnp.float32),
        "w_in": nrm(ks[2], (L, D, N_IN), D),
        "lru_conv_w": nrm(ks[3], (L, LRU_CONV, LRU_WIDTH), LRU_CONV),
        "lru_conv_b": 0.02 * jax.random.normal(ks[4], (L, LRU_WIDTH), jnp.float32),
        "lru_wr": nrm(ks[5], (L, LRU_BLOCKS, LRU_BLOCK_DIM, LRU_BLOCK_DIM), LRU_BLOCK_DIM),
        "lru_br": 0.02 * jax.random.normal(ks[6], (L, LRU_WIDTH), jnp.float32),
        "lru_wi": nrm(ks[7], (L, LRU_BLOCKS, LRU_BLOCK_DIM, LRU_BLOCK_DIM), LRU_BLOCK_DIM),
        "lru_bi": 0.02 * jax.random.normal(ks[8], (L, LRU_WIDTH), jnp.float32),
        "lru_lambda": jnp.log(a0) - jnp.log1p(-a0),
        "pool_w": nrm(ks[10], (L, POOL_GROUPS, POOL_GROUP_DIM, POOL_GROUP_DIM), POOL_GROUP_DIM),
        "pool_scale": 1.0 + 0.1 * jax.random.normal(ks[11], (L, POOL_WIDTH), jnp.float32),
        "sconv_w": nrm(ks[12], (L, SCONV_K, SCONV_WIDTH), SCONV_K),
        "q_norm_g": 1.0 + 0.1 * jax.random.normal(ks[13], (L, HEAD_DIM), jnp.float32),
        "k_norm_g": 1.0 + 0.1 * jax.random.normal(ks[14], (L, HEAD_DIM), jnp.float32),
        "forget_b": jax.random.uniform(ks[15], (L, ATTN_HEADS), jnp.float32, 1.0, 5.0),
        "w_branch": nrm(ks[16], (L, N_BRANCH, W, D), W),
        "w_out": nrm(ks[17], (L, D, D), D),
        "norm_mlp_g": 1.0 + 0.1 * jax.random.normal(ks[18], (L, D), jnp.float32),
        "w_mlp_up": nrm(ks[19], (L, D, D_FF), D),
        "w_mlp_down": nrm(ks[20], (L, D_FF, D), D_FF),
    }


def reference(x, norm_mix_g, w_in, lru_conv_w, lru_conv_b, lru_wr, lru_br, lru_wi, lru_bi,
              lru_lambda, pool_w, pool_scale, sconv_w, q_norm_g, k_norm_g, forget_b,
              w_branch, w_out, norm_mlp_g, w_mlp_up, w_mlp_down):
    b, s, d = x.shape
    split_pts = _split_points()
    for l in range(DEPTH):
        xn = rms_norm(x, norm_mix_g[l])
        proj = xn @ w_in[l]
        xa, xpool, xsc, qkv, f_logit, gate_logits = jnp.split(proj, split_pts, axis=-1)
        y_a = rglru_branch(xa, lru_conv_w[l], lru_conv_b[l], lru_wr[l], lru_br[l],
                           lru_wi[l], lru_bi[l], lru_lambda[l])
        y_b = pool_branch(xpool, pool_w[l], pool_scale[l])
        y_c = shortconv_branch(xsc, sconv_w[l])
        y_d = forgetting_attention(qkv, f_logit, forget_b[l], q_norm_g[l], k_norm_g[l])
        ys = jnp.stack([y_a, y_b, y_c, y_d], axis=2)
        branches = jnp.einsum('bskw,kwd->bskd', ys, w_branch[l])
        gates = jax.nn.sigmoid(gate_logits.reshape(b, s, N_BRANCH, d))
        merged = jnp.sum(gates * branches, axis=2)
        x = x + merged @ w_out[l]
        h = rms_norm(x, norm_mlp_g[l]) @ w_mlp_up[l]
        x = x + jnp.square(jax.nn.relu(h)) @ w_mlp_down[l]
    return x
```

```python
import functools

import jax
import jax.numpy as jnp
from jax import lax
from jax.experimental import pallas as pl
from jax.experimental.pallas import tpu as pltpu

F32 = jnp.float32
BF16 = jnp.bfloat16

D_MODEL = 2048
WIDTH = 512
N_BRANCH = 4
LRU_BLOCK_DIM = 64
LRU_CONV = 4
LRU_C = 8.0
POOL_GROUPS = 4
SCONV_K = 3
HEADS = 8
HEAD_DIM = 64
D_FF = 4 * D_MODEL
EPS = 1e-6

LANES = 128
SUBLANES = 8
VMEM_LIMIT = 56 * 1024 * 1024

N_MIX = 2 * WIDTH + 3 * WIDTH + 3 * WIDTH + HEADS
N_MIX_PAD = -(-N_MIX // LANES) * LANES
COL_LRU = 0
COL_POOL = WIDTH
COL_SCONV = 2 * WIDTH
COL_QKV = 5 * WIDTH
COL_FORGET = 8 * WIDTH


def _params(semantics):
    return pltpu.CompilerParams(dimension_semantics=semantics, vmem_limit_bytes=VMEM_LIMIT)


def _dot(a, b):
    return jnp.dot(a, b, preferred_element_type=F32)


def _shift_rows(x, k, fill):
    n, c = x.shape
    if k >= n:
        return jnp.full((n, c), fill, x.dtype)
    if k % SUBLANES == 0:
        return jnp.concatenate([jnp.full((k, c), fill, x.dtype), x[:n - k]], axis=0)
    r = pltpu.roll(x, k, 0)
    row = lax.broadcasted_iota(jnp.int32, (SUBLANES, c), 0)
    top = jnp.where(row < k, jnp.asarray(fill, x.dtype), r[:SUBLANES])
    return jnp.concatenate([top, r[SUBLANES:]], axis=0)


def _cumsum_rows(x):
    k = 1
    while k < x.shape[0]:
        x = x + _shift_rows(x, k, 0.0)
        k *= 2
    return x


def _linear_scan_rows(a, b):
    k = 1
    n = a.shape[0]
    while k < n:
        b = a * _shift_rows(b, k, 0.0) + b
        if 2 * k < n:
            a = a * _shift_rows(a, k, 1.0)
        k *= 2
    return b


def _softplus(x):
    return jnp.maximum(x, 0.0) + jnp.log1p(jnp.exp(-jnp.abs(x)))


def _inproj_kernel(x_ref, g_ref, w_ref, proj_ref, xn_ref):
    @pl.when(pl.program_id(1) == 0)
    def _():
        x = x_ref[...]
        ms = jnp.mean(x * x, axis=-1, keepdims=True)
        xn_ref[...] = ((x * lax.rsqrt(ms + EPS)) * g_ref[...]).astype(xn_ref.dtype)

    proj_ref[...] = _dot(xn_ref[...], w_ref[...])


def _inproj(x, g, w, *, tm, tn):
    t, d = x.shape
    n = w.shape[1]
    return pl.pallas_call(
        _inproj_kernel,
        grid=(t // tm, n // tn),
        in_specs=[
            pl.BlockSpec((tm, d), lambda i, j: (i, 0)),
            pl.BlockSpec((1, d), lambda i, j: (0, 0)),
            pl.BlockSpec((d, tn), lambda i, j: (0, j)),
        ],
        out_specs=[
            pl.BlockSpec((tm, tn), lambda i, j: (i, j)),
            pl.BlockSpec((tm, d), lambda i, j: (i, 0)),
        ],
        out_shape=[
            jax.ShapeDtypeStruct((t, n), F32),
            jax.ShapeDtypeStruct((t, d), BF16),
        ],
        compiler_params=_params(("parallel", "arbitrary")),
        name="inproj",
    )(x, g, w)


def _lru_kernel(x_ref, cw_ref, cb_ref, wr_ref, br_ref, wi_ref, bi_ref, lam_ref, y_ref):
    x = x_ref[...]
    cw = cw_ref[...]
    u = cw[LRU_CONV - 1:LRU_CONV] * x + cb_ref[...]
    for k in range(1, LRU_CONV):
        u = u + cw[LRU_CONV - 1 - k:LRU_CONV - k] * _shift_rows(x, k, 0.0)
    ub = u.astype(BF16)
    r = jax.nn.sigmoid(_dot(ub, wr_ref[0]) + br_ref[...])
    gi = jax.nn.sigmoid(_dot(ub, wi_ref[0]) + bi_ref[...])
    log_a = (-LRU_C * r) * _softplus(-lam_ref[...])
    a = jnp.exp(log_a)
    inp = jnp.sqrt(-jnp.tanh(log_a) * (a * a + 1.0)) * (gi * u)
    y_ref[...] = _linear_scan_rows(a, inp).astype(y_ref.dtype)


def _lru(proj, cw, cb, wr_bd, br, wi_bd, bi, lam, *, batch, seq):
    nc = WIDTH // LANES
    col0 = COL_LRU // LANES
    vec = pl.BlockSpec((1, LANES), lambda b, c: (0, c))
    mat = pl.BlockSpec((1, LANES, LANES), lambda b, c: (c, 0, 0))
    return pl.pallas_call(
        _lru_kernel,
        grid=(batch, nc),
        in_specs=[
            pl.BlockSpec((seq, LANES), lambda b, c: (b, col0 + c)),
            pl.BlockSpec((LRU_CONV, LANES), lambda b, c: (0, c)),
            vec, mat, vec, mat, vec, vec,
        ],
        out_specs=pl.BlockSpec((seq, LANES), lambda b, c: (b, c)),
        out_shape=jax.ShapeDtypeStruct((batch * seq, WIDTH), BF16),
        compiler_params=_params(("parallel", "parallel")),
        name="rglru",
    )(proj, cw, cb, wr_bd, br, wi_bd, bi, lam)


def _pool_kernel(x_ref, w_ref, sc_ref, y_ref):
    g = pl.program_id(1)
    x = x_ref[...]
    w2 = x + _shift_rows(x, 1, 0.0)
    w4 = w2 + _shift_rows(w2, 2, 0.0)
    w8 = w4 + _shift_rows(w4, 4, 0.0)
    w16 = w8 + _shift_rows(w8, 8, 0.0)
    wsum = jnp.where(g == 0, w2, jnp.where(g == 1, w4, jnp.where(g == 2, w8, w16)))
    win = jnp.left_shift(2, g)
    t = lax.broadcasted_iota(jnp.int32, x.shape, 0)
    count = jnp.minimum(t + 1, win).astype(F32)
    pooled = wsum / count - x
    y_ref[...] = (_dot(pooled.astype(BF16), w_ref[0]) * sc_ref[...]).astype(y_ref.dtype)


def _pool(proj, w, scale, *, batch, seq):
    col0 = COL_POOL // LANES
    return pl.pallas_call(
        _pool_kernel,
        grid=(batch, POOL_GROUPS),
        in_specs=[
            pl.BlockSpec((seq, LANES), lambda b, g: (b, col0 + g)),
            pl.BlockSpec((1, LANES, LANES), lambda b, g: (g, 0, 0)),
            pl.BlockSpec((1, LANES), lambda b, g: (0, g)),
        ],
        out_specs=pl.BlockSpec((seq, LANES), lambda b, g: (b, g)),
        out_shape=jax.ShapeDtypeStruct((batch * seq, WIDTH), BF16),
        compiler_params=_params(("parallel", "parallel")),
        name="pool",
    )(proj, w, scale)


def _sconv_kernel(gb_ref, gc_ref, xc_ref, w_ref, y_ref):
    w = w_ref[...]
    z = gc_ref[...] * xc_ref[...]
    acc = w[SCONV_K - 1:SCONV_K] * z
    for k in range(1, SCONV_K):
        acc = acc + w[SCONV_K - 1 - k:SCONV_K - k] * _shift_rows(z, k, 0.0)
    y_ref[...] = (gb_ref[...] * acc).astype(y_ref.dtype)


def _sconv(proj, w, *, batch, seq):
    nc = WIDTH // LANES
    col0 = COL_SCONV // LANES

    def part(p):
        return pl.BlockSpec((seq, LANES), lambda b, c: (b, col0 + p * nc + c))

    return pl.pallas_call(
        _sconv_kernel,
        grid=(batch, nc),
        in_specs=[part(0), part(1), part(2),
                  pl.BlockSpec((SCONV_K, LANES), lambda b, c: (0, c))],
        out_specs=pl.BlockSpec((seq, LANES), lambda b, c: (b, c)),
        out_shape=jax.ShapeDtypeStruct((batch * seq, WIDTH), BF16),
        compiler_params=_params(("parallel", "parallel")),
        name="sconv",
    )(proj, proj, proj, w)


def _forget_kernel(f_ref, fb_ref, ck_ref, cq_ref):
    z = f_ref[...] + fb_ref[...]
    log_f = jnp.minimum(z, 0.0) - jnp.log1p(jnp.exp(-jnp.abs(z)))
    c = _cumsum_rows(log_f)
    ct = c.T
    for h in range(HEADS):
        ck_ref[0, h] = ct[h:h + 1, :]
        cq_ref[0, h] = jnp.broadcast_to(c[:, h:h + 1], c.shape)


def _forget_cumsum(proj, fb, *, batch, seq):
    col0 = COL_FORGET // LANES
    return pl.pallas_call(
        _forget_kernel,
        grid=(batch,),
        in_specs=[
            pl.BlockSpec((seq, LANES), lambda b: (b, col0)),
            pl.BlockSpec((1, LANES), lambda b: (0, 0)),
        ],
        out_specs=[
            pl.BlockSpec((1, HEADS, 1, seq), lambda b: (b, 0, 0, 0)),
            pl.BlockSpec((1, HEADS, seq, LANES), lambda b: (b, 0, 0, 0)),
        ],
        out_shape=[
            jax.ShapeDtypeStruct((batch, HEADS, 1, seq), F32),
            jax.ShapeDtypeStruct((batch, HEADS, seq, LANES), F32),
        ],
        compiler_params=_params(("parallel",)),
        name="forget_cumsum",
    )(proj, fb)


ATTN_TQ = 256


def _attn_kernel(q_ref, k_ref, v_ref, cka_ref, ckb_ref, cqa_ref, cqb_ref, qg_ref, kg_ref, hm_ref,
                 o_ref, qa_scr, qb_scr, kn_scr, vb_scr):
    seq = q_ref.shape[0]
    hm = hm_ref[...]

    def head_norm(x, g):
        sq = x * x
        hi = sq.astype(BF16)
        lo = (sq - hi.astype(F32)).astype(BF16)
        ms = (_dot(hi, hm) + _dot(lo, hm)) * (1.0 / HEAD_DIM)
        return (x * lax.rsqrt(ms + EPS)) * g

    qn = head_norm(q_ref[...], qg_ref[...]) * (HEAD_DIM ** -0.5)
    first = lax.broadcasted_iota(jnp.int32, (1, LANES), 1) < HEAD_DIM
    qa_scr[...] = jnp.where(first, qn, 0.0).astype(BF16)
    qb_scr[...] = jnp.where(first, 0.0, qn).astype(BF16)
    kn_scr[...] = head_norm(k_ref[...], kg_ref[...]).astype(BF16)
    vb_scr[...] = v_ref[...].astype(BF16)

    tq = ATTN_TQ
    for i in range(seq // tq):
        q0, kend = i * tq, (i + 1) * tq
        row = lax.broadcasted_iota(jnp.int32, (tq, kend), 0) + q0
        col = lax.broadcasted_iota(jnp.int32, (tq, kend), 1)
        causal = col <= row
        outs = []
        for q_scr, ck_ref, cq_ref in ((qa_scr, cka_ref, cqa_ref), (qb_scr, ckb_ref, cqb_ref)):
            s = lax.dot_general(q_scr[q0:kend, :], kn_scr[:kend, :],
                                (((1,), (1,)), ((), ())), preferred_element_type=F32)
            s = s + (cq_ref[0, 0, q0:kend, 0:1] - ck_ref[0, 0, :, :kend])
            s = jnp.where(causal, s, -jnp.inf)
            m = jnp.max(s, axis=1, keepdims=True)
            p = jnp.exp(s - m)
            l = jnp.sum(p, axis=1, keepdims=True)
            outs.append(_dot(p.astype(BF16), vb_scr[:kend, :]) / l)
        o_ref[q0:kend, :] = jnp.where(first, outs[0], outs[1]).astype(o_ref.dtype)


def _attention(proj, ck, cq, qg, kg, hm, *, batch, seq):
    npair = HEADS // 2
    col0 = COL_QKV // LANES

    def part(p):
        return pl.BlockSpec((seq, LANES), lambda b, h: (b, col0 + p * npair + h))

    def ck_spec(o):
        return pl.BlockSpec((1, 1, 1, seq), lambda b, h: (b, 2 * h + o, 0, 0))

    def cq_spec(o):
        return pl.BlockSpec((1, 1, seq, LANES), lambda b, h: (b, 2 * h + o, 0, 0))

    vec = pl.BlockSpec((1, LANES), lambda b, h: (0, 0))
    return pl.pallas_call(
        _attn_kernel,
        grid=(batch, npair),
        in_specs=[part(0), part(1), part(2), ck_spec(0), ck_spec(1), cq_spec(0), cq_spec(1),
                  vec, vec, pl.BlockSpec((LANES, LANES), lambda b, h: (0, 0))],
        out_specs=pl.BlockSpec((seq, LANES), lambda b, h: (b, h)),
        out_shape=jax.ShapeDtypeStruct((batch * seq, WIDTH), BF16),
        scratch_shapes=[pltpu.VMEM((seq, LANES), BF16)] * 4,
        compiler_params=_params(("parallel", "parallel")),
        name="fox_attention",
    )(proj, proj, proj, ck, ck, cq, cq, qg, kg, hm)


def _merge_kernel(xn_ref, ya_ref, yb_ref, yc_ref, yd_ref, wg0_ref, wg1_ref, wg2_ref, wg3_ref, wb_ref, o_ref):
    xn = xn_ref[...]
    acc = None
    for k, (y_ref, wg_ref) in enumerate(((ya_ref, wg0_ref), (yb_ref, wg1_ref),
                                         (yc_ref, wg2_ref), (yd_ref, wg3_ref))):
        gate = jax.nn.sigmoid(_dot(xn, wg_ref[...]))
        term = gate * _dot(y_ref[...], wb_ref[k])
        acc = term if acc is None else acc + term
    o_ref[...] = acc.astype(o_ref.dtype)


def _merge(xn, ys, wg, wb, *, tm, tn):
    t, d = xn.shape
    nblk = d // tn

    def wg_spec(k):
        return pl.BlockSpec((d, tn), lambda i, j: (0, k * nblk + j))

    y_spec = pl.BlockSpec((tm, WIDTH), lambda i, j: (i, 0))
    return pl.pallas_call(
        _merge_kernel,
        grid=(t // tm, nblk),
        in_specs=[pl.BlockSpec((tm, d), lambda i, j: (i, 0)), y_spec, y_spec, y_spec, y_spec,
                  wg_spec(0), wg_spec(1), wg_spec(2), wg_spec(3),
                  pl.BlockSpec((N_BRANCH, WIDTH, tn), lambda i, j: (0, 0, j))],
        out_specs=pl.BlockSpec((tm, tn), lambda i, j: (i, j)),
        out_shape=jax.ShapeDtypeStruct((t, d), BF16),
        compiler_params=_params(("parallel", "arbitrary")),
        name="merge",
    )(xn, *ys, wg, wg, wg, wg, wb)


def _outproj_kernel(m_ref, x_ref, w_ref, g_ref, x1_ref, xn_ref):
    x1 = x_ref[...] + _dot(m_ref[...], w_ref[...])
    x1_ref[...] = x1
    ms = jnp.mean(x1 * x1, axis=-1, keepdims=True)
    xn_ref[...] = ((x1 * lax.rsqrt(ms + EPS)) * g_ref[...]).astype(xn_ref.dtype)


def _outproj(merged, x, w, g, *, tm):
    t, d = x.shape
    row = pl.BlockSpec((tm, d), lambda i: (i, 0))
    return pl.pallas_call(
        _outproj_kernel,
        grid=(t // tm,),
        in_specs=[row, row, pl.BlockSpec((d, d), lambda i: (0, 0)), pl.BlockSpec((1, d), lambda i: (0, 0))],
        out_specs=[row, row],
        out_shape=[jax.ShapeDtypeStruct((t, d), F32), jax.ShapeDtypeStruct((t, d), BF16)],
        compiler_params=_params(("parallel",)),
        name="outproj",
    )(merged, x, w, g)


def _mlp_kernel(xn_ref, x_ref, wu_ref, wd_ref, o_ref):
    h = jnp.maximum(_dot(xn_ref[...], wu_ref[...]), 0.0)
    upd = _dot((h * h).astype(BF16), wd_ref[...])

    @pl.when(pl.program_id(1) == 0)
    def _():
        o_ref[...] = x_ref[...] + upd

    @pl.when(pl.program_id(1) != 0)
    def _():
        o_ref[...] += upd


def _mlp(xn, x, wu, wd, *, tm, tf):
    t, d = x.shape
    f = wu.shape[1]
    row = pl.BlockSpec((tm, d), lambda i, j: (i, 0))
    return pl.pallas_call(
        _mlp_kernel,
        grid=(t // tm, f // tf),
        in_specs=[row, row,
                  pl.BlockSpec((d, tf), lambda i, j: (0, j)),
                  pl.BlockSpec((tf, d), lambda i, j: (j, 0))],
        out_specs=row,
        out_shape=jax.ShapeDtypeStruct((t, d), F32),
        compiler_params=_params(("parallel", "arbitrary")),
        name="mlp",
    )(xn, x, wu, wd)


def _block_diag_pairs(w):
    n2, bd, _ = w.shape
    per = LANES // bd
    w4 = w.reshape(n2 // per, per, bd, bd)
    eye = jnp.eye(per, dtype=w.dtype)
    return jnp.einsum('jaik,ab->jaibk', w4, eye).reshape(n2 // per, LANES, LANES)


def _row(v):
    return v.reshape(1, -1).astype(F32)


def kernel(x, norm_mix_g, w_in, lru_conv_w, lru_conv_b, lru_wr, lru_br, lru_wi, lru_bi, lru_lambda, pool_w, pool_scale, sconv_w, q_norm_g, k_norm_g, forget_b, w_branch, w_out, norm_mlp_g, w_mlp_up, w_mlp_down):
    batch, seq, d = x.shape
    assert d == D_MODEL and seq % ATTN_TQ == 0
    depth = w_in.shape[0]
    t = batch * seq
    xf = x.reshape(t, d)

    lane = jnp.arange(LANES)
    same_head = (lane[:, None] // HEAD_DIM == lane[None, :] // HEAD_DIM).astype(BF16)

    for l in range(depth):
        w_mix = jnp.pad(w_in[l][:, :N_MIX], ((0, 0), (0, N_MIX_PAD - N_MIX))).astype(BF16)
        w_gate = w_in[l][:, N_MIX:].astype(BF16)
        proj, xn = _inproj(xf, _row(norm_mix_g[l]), w_mix, tm=512, tn=N_MIX_PAD // 3)

        y_a = _lru(proj, lru_conv_w[l], _row(lru_conv_b[l]), _block_diag_pairs(lru_wr[l]).astype(BF16),
                   _row(lru_br[l]), _block_diag_pairs(lru_wi[l]).astype(BF16), _row(lru_bi[l]),
                   _row(lru_lambda[l]), batch=batch, seq=seq)
        y_b = _pool(proj, pool_w[l].astype(BF16), _row(pool_scale[l]), batch=batch, seq=seq)
        y_c = _sconv(proj, sconv_w[l], batch=batch, seq=seq)
        fb = jnp.pad(_row(forget_b[l]), ((0, 0), (0, LANES - HEADS)))
        ck, cq = _forget_cumsum(proj, fb, batch=batch, seq=seq)
        qg = jnp.tile(_row(q_norm_g[l]), (1, LANES // HEAD_DIM))
        kg = jnp.tile(_row(k_norm_g[l]), (1, LANES // HEAD_DIM))
        y_d = _attention(proj, ck, cq, qg, kg, same_head, batch=batch, seq=seq)

        merged = _merge(xn, (y_a, y_b, y_c, y_d), w_gate, w_branch[l].astype(BF16), tm=1024, tn=512)
        x1, xn2 = _outproj(merged, xf, w_out[l].astype(BF16), _row(norm_mlp_g[l]), tm=512)
        xf = _mlp(xn2, x1, w_mlp_up[l].astype(BF16), w_mlp_down[l].astype(BF16), tm=512, tf=1024)

    return xf.reshape(batch, seq, d)
```

```python
import functools

import jax
import jax.numpy as jnp
from jax import lax
from jax.experimental import pallas as pl
from jax.experimental.pallas import tpu as pltpu

F32 = jnp.float32
BF16 = jnp.bfloat16

D_MODEL = 2048
WIDTH = 512
N_BRANCH = 4
LRU_BLOCK_DIM = 64
LRU_CONV = 4
LRU_C = 8.0
POOL_GROUPS = 4
SCONV_K = 3
HEADS = 8
HEAD_DIM = 64
D_FF = 4 * D_MODEL
EPS = 1e-6

LANES = 128
SUBLANES = 8
VMEM_LIMIT = 56 * 1024 * 1024

N_MIX = 2 * WIDTH + 3 * WIDTH + 3 * WIDTH + HEADS
N_MIX_PAD = -(-N_MIX // LANES) * LANES
COL_LRU = 0
COL_POOL = WIDTH
COL_SCONV = 2 * WIDTH
COL_QKV = 5 * WIDTH
COL_FORGET = 8 * WIDTH


def _params(semantics):
    return pltpu.CompilerParams(dimension_semantics=semantics, vmem_limit_bytes=VMEM_LIMIT)


def _dot(a, b):
    return jnp.dot(a, b, preferred_element_type=F32)


def _shift_rows(x, k, fill):
    n, c = x.shape
    if k >= n:
        return jnp.full((n, c), fill, x.dtype)
    if k % SUBLANES == 0:
        return jnp.concatenate([jnp.full((k, c), fill, x.dtype), x[:n - k]], axis=0)
    r = pltpu.roll(x, k, 0)
    row = lax.broadcasted_iota(jnp.int32, (SUBLANES, c), 0)
    top = jnp.where(row < k, jnp.asarray(fill, x.dtype), r[:SUBLANES])
    return jnp.concatenate([top, r[SUBLANES:]], axis=0)


def _cumsum_rows(x):
    k = 1
    while k < x.shape[0]:
        x = x + _shift_rows(x, k, 0.0)
        k *= 2
    return x


def _linear_scan_rows(a, b):
    k = 1
    n = a.shape[0]
    while k < n:
        b = a * _shift_rows(b, k, 0.0) + b
        if 2 * k < n:
            a = a * _shift_rows(a, k, 1.0)
        k *= 2
    return b


def _softplus(x):
    return jnp.maximum(x, 0.0) + jnp.log1p(jnp.exp(-jnp.abs(x)))


def _inproj_kernel(x_ref, g_ref, w_ref, proj_ref, xn_ref):
    @pl.when(pl.program_id(1) == 0)
    def _():
        x = x_ref[...]
        ms = jnp.mean(x * x, axis=-1, keepdims=True)
        xn_ref[...] = ((x * lax.rsqrt(ms + EPS)) * g_ref[...]).astype(xn_ref.dtype)

    proj_ref[...] = _dot(xn_ref[...], w_ref[0].astype(BF16))


def _inproj(x, g, w_in, layer, *, tm, tn):
    t, d = x.shape
    n = N_MIX_PAD
    return pl.pallas_call(
        _inproj_kernel,
        grid=(t // tm, n // tn),
        in_specs=[
            pl.BlockSpec((tm, d), lambda i, j: (i, 0), pipeline_mode=pl.Buffered(1)),
            pl.BlockSpec((1, d), lambda i, j: (0, 0)),
            pl.BlockSpec((1, d, tn), lambda i, j: (layer, 0, j)),
        ],
        out_specs=[
            pl.BlockSpec((tm, tn), lambda i, j: (i, j)),
            pl.BlockSpec((tm, d), lambda i, j: (i, 0)),
        ],
        out_shape=[
            jax.ShapeDtypeStruct((t, n), F32),
            jax.ShapeDtypeStruct((t, d), BF16),
        ],
        compiler_params=_params(("parallel", "arbitrary")),
        name="inproj",
    )(x, g, w_in)


def _lru_kernel(x_ref, cw_ref, cb_ref, wr_ref, br_ref, wi_ref, bi_ref, lam_ref, y_ref):
    x = x_ref[...]
    cw = cw_ref[...]
    u = cw[LRU_CONV - 1:LRU_CONV] * x + cb_ref[...]
    for k in range(1, LRU_CONV):
        u = u + cw[LRU_CONV - 1 - k:LRU_CONV - k] * _shift_rows(x, k, 0.0)
    ub = u.astype(BF16)
    r = jax.nn.sigmoid(_dot(ub, wr_ref[0]) + br_ref[...])
    gi = jax.nn.sigmoid(_dot(ub, wi_ref[0]) + bi_ref[...])
    log_a = (-LRU_C * r) * _softplus(-lam_ref[...])
    a = jnp.exp(log_a)
    inp = jnp.sqrt(-jnp.tanh(log_a) * (a * a + 1.0)) * (gi * u)
    y_ref[...] = _linear_scan_rows(a, inp).astype(y_ref.dtype)


def _lru(proj, cw, cb, wr_bd, br, wi_bd, bi, lam, *, batch, seq):
    nc = WIDTH // LANES
    col0 = COL_LRU // LANES
    vec = pl.BlockSpec((1, LANES), lambda b, c: (0, c))
    mat = pl.BlockSpec((1, LANES, LANES), lambda b, c: (c, 0, 0))
    return pl.pallas_call(
        _lru_kernel,
        grid=(batch, nc),
        in_specs=[
            pl.BlockSpec((seq, LANES), lambda b, c: (b, col0 + c)),
            pl.BlockSpec((LRU_CONV, LANES), lambda b, c: (0, c)),
            vec, mat, vec, mat, vec, vec,
        ],
        out_specs=pl.BlockSpec((seq, LANES), lambda b, c: (b, c)),
        out_shape=jax.ShapeDtypeStruct((batch * seq, WIDTH), BF16),
        compiler_params=_params(("parallel", "parallel")),
        name="rglru",
    )(proj, cw, cb, wr_bd, br, wi_bd, bi, lam)


def _pool_kernel(x_ref, w_ref, sc_ref, y_ref):
    g = pl.program_id(1)
    x = x_ref[...]
    w2 = x + _shift_rows(x, 1, 0.0)
    w4 = w2 + _shift_rows(w2, 2, 0.0)
    w8 = w4 + _shift_rows(w4, 4, 0.0)
    w16 = w8 + _shift_rows(w8, 8, 0.0)
    wsum = jnp.where(g == 0, w2, jnp.where(g == 1, w4, jnp.where(g == 2, w8, w16)))
    win = jnp.left_shift(2, g)
    t = lax.broadcasted_iota(jnp.int32, x.shape, 0)
    count = jnp.minimum(t + 1, win).astype(F32)
    pooled = wsum / count - x
    y_ref[...] = (_dot(pooled.astype(BF16), w_ref[0]) * sc_ref[...]).astype(y_ref.dtype)


def _pool(proj, w, scale, *, batch, seq):
    col0 = COL_POOL // LANES
    return pl.pallas_call(
        _pool_kernel,
        grid=(batch, POOL_GROUPS),
        in_specs=[
            pl.BlockSpec((seq, LANES), lambda b, g: (b, col0 + g)),
            pl.BlockSpec((1, LANES, LANES), lambda b, g: (g, 0, 0)),
            pl.BlockSpec((1, LANES), lambda b, g: (0, g)),
        ],
        out_specs=pl.BlockSpec((seq, LANES), lambda b, g: (b, g)),
        out_shape=jax.ShapeDtypeStruct((batch * seq, WIDTH), BF16),
        compiler_params=_params(("parallel", "parallel")),
        name="pool",
    )(proj, w, scale)


def _sconv_kernel(gb_ref, gc_ref, xc_ref, w_ref, y_ref):
    w = w_ref[...]
    z = gc_ref[...] * xc_ref[...]
    acc = w[SCONV_K - 1:SCONV_K] * z
    for k in range(1, SCONV_K):
        acc = acc + w[SCONV_K - 1 - k:SCONV_K - k] * _shift_rows(z, k, 0.0)
    y_ref[...] = (gb_ref[...] * acc).astype(y_ref.dtype)


def _sconv(proj, w, *, batch, seq):
    nc = WIDTH // LANES
    col0 = COL_SCONV // LANES

    def part(p):
        return pl.BlockSpec((seq, LANES), lambda b, c: (b, col0 + p * nc + c))

    return pl.pallas_call(
        _sconv_kernel,
        grid=(batch, nc),
        in_specs=[part(0), part(1), part(2),
                  pl.BlockSpec((SCONV_K, LANES), lambda b, c: (0, c))],
        out_specs=pl.BlockSpec((seq, LANES), lambda b, c: (b, c)),
        out_shape=jax.ShapeDtypeStruct((batch * seq, WIDTH), BF16),
        compiler_params=_params(("parallel", "parallel")),
        name="sconv",
    )(proj, proj, proj, w)


def _forget_kernel(f_ref, fb_ref, ck_ref, cq_ref):
    z = f_ref[...] + fb_ref[...]
    log_f = jnp.minimum(z, 0.0) - jnp.log1p(jnp.exp(-jnp.abs(z)))
    c = _cumsum_rows(log_f)
    ct = c.T
    for h in range(HEADS):
        ck_ref[0, h] = ct[h:h + 1, :]
        cq_ref[0, h] = jnp.broadcast_to(c[:, h:h + 1], c.shape)


def _forget_cumsum(proj, fb, *, batch, seq):
    col0 = COL_FORGET // LANES
    return pl.pallas_call(
        _forget_kernel,
        grid=(batch,),
        in_specs=[
            pl.BlockSpec((seq, LANES), lambda b: (b, col0)),
            pl.BlockSpec((1, LANES), lambda b: (0, 0)),
        ],
        out_specs=[
            pl.BlockSpec((1, HEADS, 1, seq), lambda b: (b, 0, 0, 0)),
            pl.BlockSpec((1, HEADS, seq, LANES), lambda b: (b, 0, 0, 0)),
        ],
        out_shape=[
            jax.ShapeDtypeStruct((batch, HEADS, 1, seq), F32),
            jax.ShapeDtypeStruct((batch, HEADS, seq, LANES), F32),
        ],
        compiler_params=_params(("parallel",)),
        name="forget_cumsum",
    )(proj, fb)


ATTN_TQ = 256


def _attn_kernel(q_ref, k_ref, v_ref, cka_ref, ckb_ref, cqa_ref, cqb_ref, qg_ref, kg_ref, hm_ref,
                 o_ref, qa_scr, qb_scr, kn_scr, vb_scr):
    seq = q_ref.shape[0]
    hm = hm_ref[...]

    def head_norm(x, g):
        sq = x * x
        hi = sq.astype(BF16)
        lo = (sq - hi.astype(F32)).astype(BF16)
        ms = (_dot(hi, hm) + _dot(lo, hm)) * (1.0 / HEAD_DIM)
        return (x * lax.rsqrt(ms + EPS)) * g

    qn = head_norm(q_ref[...], qg_ref[...]) * (HEAD_DIM ** -0.5)
    first = lax.broadcasted_iota(jnp.int32, (1, LANES), 1) < HEAD_DIM
    qa_scr[...] = jnp.where(first, qn, 0.0).astype(BF16)
    qb_scr[...] = jnp.where(first, 0.0, qn).astype(BF16)
    kn_scr[...] = head_norm(k_ref[...], kg_ref[...]).astype(BF16)
    vb_scr[...] = v_ref[...].astype(BF16)

    tq = ATTN_TQ
    for i in range(seq // tq):
        q0, kend = i * tq, (i + 1) * tq
        row = lax.broadcasted_iota(jnp.int32, (tq, kend), 0) + q0
        col = lax.broadcasted_iota(jnp.int32, (tq, kend), 1)
        causal = col <= row
        outs = []
        for q_scr, ck_ref, cq_ref in ((qa_scr, cka_ref, cqa_ref), (qb_scr, ckb_ref, cqb_ref)):
            s = lax.dot_general(q_scr[q0:kend, :], kn_scr[:kend, :],
                                (((1,), (1,)), ((), ())), preferred_element_type=F32)
            s = s + (cq_ref[0, 0, q0:kend, 0:1] - ck_ref[0, 0, :, :kend])
            s = jnp.where(causal, s, -jnp.inf)
            m = jnp.max(s, axis=1, keepdims=True)
            p = jnp.exp(s - m)
            l = jnp.sum(p, axis=1, keepdims=True)
            outs.append(_dot(p.astype(BF16), vb_scr[:kend, :]) / l)
        o_ref[q0:kend, :] = jnp.where(first, outs[0], outs[1]).astype(o_ref.dtype)


def _attention(proj, ck, cq, qg, kg, hm, *, batch, seq):
    npair = HEADS // 2
    col0 = COL_QKV // LANES

    def part(p):
        return pl.BlockSpec((seq, LANES), lambda b, h: (b, col0 + p * npair + h))

    def ck_spec(o):
        return pl.BlockSpec((1, 1, 1, seq), lambda b, h: (b, 2 * h + o, 0, 0))

    def cq_spec(o):
        return pl.BlockSpec((1, 1, seq, LANES), lambda b, h: (b, 2 * h + o, 0, 0))

    vec = pl.BlockSpec((1, LANES), lambda b, h: (0, 0))
    return pl.pallas_call(
        _attn_kernel,
        grid=(batch, npair),
        in_specs=[part(0), part(1), part(2), ck_spec(0), ck_spec(1), cq_spec(0), cq_spec(1),
                  vec, vec, pl.BlockSpec((LANES, LANES), lambda b, h: (0, 0))],
        out_specs=pl.BlockSpec((seq, LANES), lambda b, h: (b, h)),
        out_shape=jax.ShapeDtypeStruct((batch * seq, WIDTH), BF16),
        scratch_shapes=[pltpu.VMEM((seq, LANES), BF16)] * 4,
        compiler_params=_params(("parallel", "parallel")),
        name="fox_attention",
    )(proj, proj, proj, ck, ck, cq, cq, qg, kg, hm)


def _merge_kernel(xn_ref, ya_ref, yb_ref, yc_ref, yd_ref, wg0_ref, wg1_ref, wg2_ref, wg3_ref, wb_ref, o_ref):
    xn = xn_ref[...]
    acc = None
    for k, (y_ref, wg_ref) in enumerate(((ya_ref, wg0_ref), (yb_ref, wg1_ref),
                                         (yc_ref, wg2_ref), (yd_ref, wg3_ref))):
        gate = jax.nn.sigmoid(_dot(xn, wg_ref[0]))
        term = gate * _dot(y_ref[...], wb_ref[0, k].astype(BF16))
        acc = term if acc is None else acc + term
    o_ref[...] = acc.astype(o_ref.dtype)


def _merge(xn, ys, wg, wb, layer, *, tm, tn):
    t, d = xn.shape
    nblk = d // tn

    def wg_spec(k):
        return pl.BlockSpec((1, d, tn), lambda i, j: (layer, 0, k * nblk + j))

    y_spec = pl.BlockSpec((tm, WIDTH), lambda i, j: (i, 0))
    return pl.pallas_call(
        _merge_kernel,
        grid=(t // tm, nblk),
        in_specs=[pl.BlockSpec((tm, d), lambda i, j: (i, 0)), y_spec, y_spec, y_spec, y_spec,
                  wg_spec(0), wg_spec(1), wg_spec(2), wg_spec(3),
                  pl.BlockSpec((1, N_BRANCH, WIDTH, tn), lambda i, j: (layer, 0, 0, j))],
        out_specs=pl.BlockSpec((tm, tn), lambda i, j: (i, j)),
        out_shape=jax.ShapeDtypeStruct((t, d), BF16),
        compiler_params=_params(("parallel", "arbitrary")),
        name="merge",
    )(xn, *ys, wg, wg, wg, wg, wb)


def _outproj_kernel(m_ref, x_ref, w_ref, g_ref, x1_ref, xn_ref, wb_scr):
    @pl.when(pl.program_id(0) == 0)
    def _():
        wb_scr[...] = w_ref[0].astype(BF16)

    x1 = x_ref[...] + _dot(m_ref[...], wb_scr[...])
    x1_ref[...] = x1
    ms = jnp.mean(x1 * x1, axis=-1, keepdims=True)
    xn_ref[...] = ((x1 * lax.rsqrt(ms + EPS)) * g_ref[...]).astype(xn_ref.dtype)


def _outproj(merged, x, w_out, g, layer, *, tm):
    t, d = x.shape
    row = pl.BlockSpec((tm, d), lambda i: (i, 0))
    return pl.pallas_call(
        _outproj_kernel,
        grid=(t // tm,),
        in_specs=[row, row,
                  pl.BlockSpec((1, d, d), lambda i: (layer, 0, 0), pipeline_mode=pl.Buffered(1)),
                  pl.BlockSpec((1, d), lambda i: (0, 0))],
        out_specs=[row, row],
        out_shape=[jax.ShapeDtypeStruct((t, d), F32), jax.ShapeDtypeStruct((t, d), BF16)],
        scratch_shapes=[pltpu.VMEM((d, d), BF16)],
        compiler_params=_params(("arbitrary",)),
        name="outproj",
    )(merged, x, w_out, g)


def _mlp_kernel(xn_ref, x_ref, wu_ref, wd_ref, o_ref):
    @pl.when(pl.program_id(1) == 0)
    def _():
        o_ref[...] = x_ref[...]

    h = jnp.maximum(_dot(xn_ref[...], wu_ref[0].astype(BF16)), 0.0)
    o_ref[...] += _dot((h * h).astype(BF16), wd_ref[0].astype(BF16))


def _mlp(xn, x, wu, wd, layer, *, tm, tf):
    t, d = x.shape
    f = wu.shape[2]
    row = pl.BlockSpec((tm, d), lambda i, j: (i, 0))
    return pl.pallas_call(
        _mlp_kernel,
        grid=(t // tm, f // tf),
        in_specs=[row, pl.BlockSpec((tm, d), lambda i, j: (i, 0), pipeline_mode=pl.Buffered(1)),
                  pl.BlockSpec((1, d, tf), lambda i, j: (layer, 0, j)),
                  pl.BlockSpec((1, tf, d), lambda i, j: (layer, j, 0))],
        out_specs=row,
        out_shape=jax.ShapeDtypeStruct((t, d), F32),
        compiler_params=_params(("parallel", "arbitrary")),
        name="mlp",
    )(xn, x, wu, wd)


def _block_diag_pairs(w):
    n2, bd, _ = w.shape
    per = LANES // bd
    w4 = w.reshape(n2 // per, per, bd, bd)
    eye = jnp.eye(per, dtype=w.dtype)
    return jnp.einsum('jaik,ab->jaibk', w4, eye).reshape(n2 // per, LANES, LANES)


def _row(v):
    return v.reshape(1, -1).astype(F32)


def kernel(x, norm_mix_g, w_in, lru_conv_w, lru_conv_b, lru_wr, lru_br, lru_wi, lru_bi, lru_lambda, pool_w, pool_scale, sconv_w, q_norm_g, k_norm_g, forget_b, w_branch, w_out, norm_mlp_g, w_mlp_up, w_mlp_down):
    batch, seq, d = x.shape
    assert d == D_MODEL and seq % ATTN_TQ == 0
    depth = w_in.shape[0]
    t = batch * seq
    xf = x.reshape(t, d)

    lane = jnp.arange(LANES)
    same_head = (lane[:, None] // HEAD_DIM == lane[None, :] // HEAD_DIM).astype(BF16)

    w_gate = w_in[:, :, N_MIX:].astype(BF16)

    for l in range(depth):
        proj, xn = _inproj(xf, _row(norm_mix_g[l]), w_in, l, tm=1024, tn=N_MIX_PAD // 3)

        y_a = _lru(proj, lru_conv_w[l], _row(lru_conv_b[l]), _block_diag_pairs(lru_wr[l]).astype(BF16),
                   _row(lru_br[l]), _block_diag_pairs(lru_wi[l]).astype(BF16), _row(lru_bi[l]),
                   _row(lru_lambda[l]), batch=batch, seq=seq)
        y_b = _pool(proj, pool_w[l].astype(BF16), _row(pool_scale[l]), batch=batch, seq=seq)
        y_c = _sconv(proj, sconv_w[l], batch=batch, seq=seq)
        fb = jnp.pad(_row(forget_b[l]), ((0, 0), (0, LANES - HEADS)))
        ck, cq = _forget_cumsum(proj, fb, batch=batch, seq=seq)
        qg = jnp.tile(_row(q_norm_g[l]), (1, LANES // HEAD_DIM))
        kg = jnp.tile(_row(k_norm_g[l]), (1, LANES // HEAD_DIM))
        y_d = _attention(proj, ck, cq, qg, kg, same_head, batch=batch, seq=seq)

        merged = _merge(xn, (y_a, y_b, y_c, y_d), w_gate, w_branch, l, tm=1024, tn=512)
        x1, xn2 = _outproj(merged, xf, w_out, _row(norm_mlp_g[l]), l, tm=512)
        xf = _mlp(xn2, x1, w_mlp_up, w_mlp_down, l, tm=1024, tf=512)

    return xf.reshape(batch, seq, d)
```

```python
import functools

import jax
import jax.numpy as jnp
from jax import lax
from jax.experimental import pallas as pl
from jax.experimental.pallas import tpu as pltpu

F32 = jnp.float32
BF16 = jnp.bfloat16

D_MODEL = 2048
WIDTH = 512
N_BRANCH = 4
LRU_BLOCK_DIM = 64
LRU_CONV = 4
LRU_C = 8.0
POOL_GROUPS = 4
SCONV_K = 3
HEADS = 8
HEAD_DIM = 64
D_FF = 4 * D_MODEL
EPS = 1e-6

LANES = 128
SUBLANES = 8
VMEM_LIMIT = 56 * 1024 * 1024

N_MIX = 2 * WIDTH + 3 * WIDTH + 3 * WIDTH + HEADS
N_MIX_PAD = -(-N_MIX // LANES) * LANES
COL_LRU = 0
COL_POOL = WIDTH
COL_SCONV = 2 * WIDTH
COL_QKV = 5 * WIDTH
COL_FORGET = 8 * WIDTH


def _params(semantics):
    return pltpu.CompilerParams(dimension_semantics=semantics, vmem_limit_bytes=VMEM_LIMIT)


def _dot(a, b):
    return jnp.dot(a, b, preferred_element_type=F32)


def _dot_nt(a, b):
    return lax.dot_general(a, b, (((1,), (1,)), ((), ())), preferred_element_type=F32)


def _shift_rows(x, k, fill):
    n, c = x.shape
    if k >= n:
        return jnp.full((n, c), fill, x.dtype)
    if k % SUBLANES == 0:
        return jnp.concatenate([jnp.full((k, c), fill, x.dtype), x[:n - k]], axis=0)
    r = pltpu.roll(x, k, 0)
    row = lax.broadcasted_iota(jnp.int32, (SUBLANES, c), 0)
    top = jnp.where(row < k, jnp.asarray(fill, x.dtype), r[:SUBLANES])
    return jnp.concatenate([top, r[SUBLANES:]], axis=0)


def _cumsum_rows(x):
    k = 1
    while k < x.shape[0]:
        x = x + _shift_rows(x, k, 0.0)
        k *= 2
    return x


def _linear_scan_rows(a, b):
    k = 1
    n = a.shape[0]
    while k < n:
        b = a * _shift_rows(b, k, 0.0) + b
        if 2 * k < n:
            a = a * _shift_rows(a, k, 1.0)
        k *= 2
    return b


def _softplus(x):
    return jnp.maximum(x, 0.0) + jnp.log1p(jnp.exp(-jnp.abs(x)))


def _inproj_kernel(x_ref, g_ref, w_ref, proj_ref, xn_ref):
    @pl.when(pl.program_id(1) == 0)
    def _():
        x = x_ref[...]
        ms = jnp.mean(x * x, axis=-1, keepdims=True)
        xn_ref[...] = ((x * lax.rsqrt(ms + EPS)) * g_ref[...]).astype(xn_ref.dtype)

    proj_ref[...] = _dot_nt(xn_ref[...], w_ref[0].astype(BF16))


def _inproj(x, g, w_in_t, layer, *, tm, tn):
    t, d = x.shape
    n = N_MIX_PAD
    return pl.pallas_call(
        _inproj_kernel,
        grid=(t // tm, n // tn),
        in_specs=[
            pl.BlockSpec((tm, d), lambda i, j: (i, 0), pipeline_mode=pl.Buffered(1)),
            pl.BlockSpec((1, d), lambda i, j: (0, 0)),
            pl.BlockSpec((1, tn, d), lambda i, j: (layer, j, 0)),
        ],
        out_specs=[
            pl.BlockSpec((tm, tn), lambda i, j: (i, j)),
            pl.BlockSpec((tm, d), lambda i, j: (i, 0)),
        ],
        out_shape=[
            jax.ShapeDtypeStruct((t, n), F32),
            jax.ShapeDtypeStruct((t, d), BF16),
        ],
        compiler_params=_params(("parallel", "arbitrary")),
        name="inproj",
    )(x, g, w_in_t)


def _lru_kernel(x_ref, cw_ref, cb_ref, wr_ref, br_ref, wi_ref, bi_ref, lam_ref, y_ref):
    x = x_ref[...]
    cw = cw_ref[...]
    u = cw[LRU_CONV - 1:LRU_CONV] * x + cb_ref[...]
    for k in range(1, LRU_CONV):
        u = u + cw[LRU_CONV - 1 - k:LRU_CONV - k] * _shift_rows(x, k, 0.0)
    ub = u.astype(BF16)
    r = jax.nn.sigmoid(_dot(ub, wr_ref[0]) + br_ref[...])
    gi = jax.nn.sigmoid(_dot(ub, wi_ref[0]) + bi_ref[...])
    log_a = (-LRU_C * r) * _softplus(-lam_ref[...])
    a = jnp.exp(log_a)
    inp = jnp.sqrt(-jnp.tanh(log_a) * (a * a + 1.0)) * (gi * u)
    y_ref[...] = _linear_scan_rows(a, inp).astype(y_ref.dtype)


def _lru(proj, cw, cb, wr_bd, br, wi_bd, bi, lam, *, batch, seq):
    nc = WIDTH // LANES
    col0 = COL_LRU // LANES
    vec = pl.BlockSpec((1, LANES), lambda b, c: (0, c))
    mat = pl.BlockSpec((1, LANES, LANES), lambda b, c: (c, 0, 0))
    return pl.pallas_call(
        _lru_kernel,
        grid=(batch, nc),
        in_specs=[
            pl.BlockSpec((seq, LANES), lambda b, c: (b, col0 + c)),
            pl.BlockSpec((LRU_CONV, LANES), lambda b, c: (0, c)),
            vec, mat, vec, mat, vec, vec,
        ],
        out_specs=pl.BlockSpec((seq, LANES), lambda b, c: (b, c)),
        out_shape=jax.ShapeDtypeStruct((batch * seq, WIDTH), BF16),
        compiler_params=_params(("parallel", "parallel")),
        name="rglru",
    )(proj, cw, cb, wr_bd, br, wi_bd, bi, lam)


def _pool_kernel(x_ref, w_ref, sc_ref, y_ref):
    g = pl.program_id(1)
    x = x_ref[...]
    w2 = x + _shift_rows(x, 1, 0.0)
    w4 = w2 + _shift_rows(w2, 2, 0.0)
    w8 = w4 + _shift_rows(w4, 4, 0.0)
    w16 = w8 + _shift_rows(w8, 8, 0.0)
    wsum = jnp.where(g == 0, w2, jnp.where(g == 1, w4, jnp.where(g == 2, w8, w16)))
    win = jnp.left_shift(2, g)
    t = lax.broadcasted_iota(jnp.int32, x.shape, 0)
    count = jnp.minimum(t + 1, win).astype(F32)
    pooled = wsum / count - x
    y_ref[...] = (_dot(pooled.astype(BF16), w_ref[0]) * sc_ref[...]).astype(y_ref.dtype)


def _pool(proj, w, scale, *, batch, seq):
    col0 = COL_POOL // LANES
    return pl.pallas_call(
        _pool_kernel,
        grid=(batch, POOL_GROUPS),
        in_specs=[
            pl.BlockSpec((seq, LANES), lambda b, g: (b, col0 + g)),
            pl.BlockSpec((1, LANES, LANES), lambda b, g: (g, 0, 0)),
            pl.BlockSpec((1, LANES), lambda b, g: (0, g)),
        ],
        out_specs=pl.BlockSpec((seq, LANES), lambda b, g: (b, g)),
        out_shape=jax.ShapeDtypeStruct((batch * seq, WIDTH), BF16),
        compiler_params=_params(("parallel", "parallel")),
        name="pool",
    )(proj, w, scale)


def _sconv_kernel(gb_ref, gc_ref, xc_ref, w_ref, y_ref):
    w = w_ref[...]
    z = gc_ref[...] * xc_ref[...]
    acc = w[SCONV_K - 1:SCONV_K] * z
    for k in range(1, SCONV_K):
        acc = acc + w[SCONV_K - 1 - k:SCONV_K - k] * _shift_rows(z, k, 0.0)
    y_ref[...] = (gb_ref[...] * acc).astype(y_ref.dtype)


def _sconv(proj, w, *, batch, seq):
    nc = WIDTH // LANES
    col0 = COL_SCONV // LANES

    def part(p):
        return pl.BlockSpec((seq, LANES), lambda b, c: (b, col0 + p * nc + c))

    return pl.pallas_call(
        _sconv_kernel,
        grid=(batch, nc),
        in_specs=[part(0), part(1), part(2),
                  pl.BlockSpec((SCONV_K, LANES), lambda b, c: (0, c))],
        out_specs=pl.BlockSpec((seq, LANES), lambda b, c: (b, c)),
        out_shape=jax.ShapeDtypeStruct((batch * seq, WIDTH), BF16),
        compiler_params=_params(("parallel", "parallel")),
        name="sconv",
    )(proj, proj, proj, w)


def _forget_kernel(f_ref, fb_ref, ck_ref, cq_ref):
    z = f_ref[...] + fb_ref[...]
    log_f = jnp.minimum(z, 0.0) - jnp.log1p(jnp.exp(-jnp.abs(z)))
    c = _cumsum_rows(log_f)
    ct = c.T
    for h in range(HEADS):
        ck_ref[0, h] = ct[h:h + 1, :]
        cq_ref[0, h] = jnp.broadcast_to(c[:, h:h + 1], c.shape)


def _forget_cumsum(proj, fb, *, batch, seq):
    col0 = COL_FORGET // LANES
    return pl.pallas_call(
        _forget_kernel,
        grid=(batch,),
        in_specs=[
            pl.BlockSpec((seq, LANES), lambda b: (b, col0)),
            pl.BlockSpec((1, LANES), lambda b: (0, 0)),
        ],
        out_specs=[
            pl.BlockSpec((1, HEADS, 1, seq), lambda b: (b, 0, 0, 0)),
            pl.BlockSpec((1, HEADS, seq, LANES), lambda b: (b, 0, 0, 0)),
        ],
        out_shape=[
            jax.ShapeDtypeStruct((batch, HEADS, 1, seq), F32),
            jax.ShapeDtypeStruct((batch, HEADS, seq, LANES), F32),
        ],
        compiler_params=_params(("parallel",)),
        name="forget_cumsum",
    )(proj, fb)


ATTN_TQ = 256


def _attn_kernel(q_ref, k_ref, v_ref, cka_ref, ckb_ref, cqa_ref, cqb_ref, qg_ref, kg_ref, hm_ref,
                 o_ref, qa_scr, qb_scr, kn_scr, vb_scr):
    seq = q_ref.shape[0]
    hm = hm_ref[...]

    def head_norm(x, g):
        sq = x * x
        hi = sq.astype(BF16)
        lo = (sq - hi.astype(F32)).astype(BF16)
        ms = (_dot(hi, hm) + _dot(lo, hm)) * (1.0 / HEAD_DIM)
        return (x * lax.rsqrt(ms + EPS)) * g

    qn = head_norm(q_ref[...], qg_ref[...]) * (HEAD_DIM ** -0.5)
    first = lax.broadcasted_iota(jnp.int32, (1, LANES), 1) < HEAD_DIM
    qa_scr[...] = jnp.where(first, qn, 0.0).astype(BF16)
    qb_scr[...] = jnp.where(first, 0.0, qn).astype(BF16)
    kn_scr[...] = head_norm(k_ref[...], kg_ref[...]).astype(BF16)
    vb_scr[...] = v_ref[...].astype(BF16)

    tq = ATTN_TQ
    for i in range(seq // tq):
        q0, kend = i * tq, (i + 1) * tq
        row = lax.broadcasted_iota(jnp.int32, (tq, kend), 0) + q0
        col = lax.broadcasted_iota(jnp.int32, (tq, kend), 1)
        causal = col <= row
        outs = []
        for q_scr, ck_ref, cq_ref in ((qa_scr, cka_ref, cqa_ref), (qb_scr, ckb_ref, cqb_ref)):
            s = lax.dot_general(q_scr[q0:kend, :], kn_scr[:kend, :],
                                (((1,), (1,)), ((), ())), preferred_element_type=F32)
            s = s + (cq_ref[0, 0, q0:kend, 0:1] - ck_ref[0, 0, :, :kend])
            s = jnp.where(causal, s, -jnp.inf)
            m = jnp.max(s, axis=1, keepdims=True)
            p = jnp.exp(s - m)
            l = jnp.sum(p, axis=1, keepdims=True)
            outs.append(_dot(p.astype(BF16), vb_scr[:kend, :]) / l)
        o_ref[q0:kend, :] = jnp.where(first, outs[0], outs[1]).astype(o_ref.dtype)


def _attention(proj, ck, cq, qg, kg, hm, *, batch, seq):
    npair = HEADS // 2
    col0 = COL_QKV // LANES

    def part(p):
        return pl.BlockSpec((seq, LANES), lambda b, h: (b, col0 + p * npair + h))

    def ck_spec(o):
        return pl.BlockSpec((1, 1, 1, seq), lambda b, h: (b, 2 * h + o, 0, 0))

    def cq_spec(o):
        return pl.BlockSpec((1, 1, seq, LANES), lambda b, h: (b, 2 * h + o, 0, 0))

    vec = pl.BlockSpec((1, LANES), lambda b, h: (0, 0))
    return pl.pallas_call(
        _attn_kernel,
        grid=(batch, npair),
        in_specs=[part(0), part(1), part(2), ck_spec(0), ck_spec(1), cq_spec(0), cq_spec(1),
                  vec, vec, pl.BlockSpec((LANES, LANES), lambda b, h: (0, 0))],
        out_specs=pl.BlockSpec((seq, LANES), lambda b, h: (b, h)),
        out_shape=jax.ShapeDtypeStruct((batch * seq, WIDTH), BF16),
        scratch_shapes=[pltpu.VMEM((seq, LANES), BF16)] * 4,
        compiler_params=_params(("parallel", "parallel")),
        name="fox_attention",
    )(proj, proj, proj, ck, ck, cq, cq, qg, kg, hm)


def _merge_kernel(xn_ref, ya_ref, yb_ref, yc_ref, yd_ref, wg0_ref, wg1_ref, wg2_ref, wg3_ref, wb_ref, o_ref):
    xn = xn_ref[...]
    acc = None
    for k, (y_ref, wg_ref) in enumerate(((ya_ref, wg0_ref), (yb_ref, wg1_ref),
                                         (yc_ref, wg2_ref), (yd_ref, wg3_ref))):
        gate = jax.nn.sigmoid(_dot_nt(xn, wg_ref[...].astype(BF16)))
        term = gate * _dot(y_ref[...], wb_ref[0, k].astype(BF16))
        acc = term if acc is None else acc + term
    o_ref[...] = acc.astype(o_ref.dtype)


def _merge(xn, ys, w_in_t, wb, layer, *, tm, tn):
    t, d = xn.shape
    nblk = d // tn

    def wg_spec(k):
        return pl.BlockSpec((pl.Squeezed(), pl.Element(tn), pl.Element(d)),
                            lambda i, j: (layer, pl.multiple_of(N_MIX + k * d + j * tn, SUBLANES), 0))

    y_spec = pl.BlockSpec((tm, WIDTH), lambda i, j: (i, 0))
    return pl.pallas_call(
        _merge_kernel,
        grid=(t // tm, nblk),
        in_specs=[pl.BlockSpec((tm, d), lambda i, j: (i, 0)), y_spec, y_spec, y_spec, y_spec,
                  wg_spec(0), wg_spec(1), wg_spec(2), wg_spec(3),
                  pl.BlockSpec((1, N_BRANCH, WIDTH, tn), lambda i, j: (layer, 0, 0, j))],
        out_specs=pl.BlockSpec((tm, tn), lambda i, j: (i, j)),
        out_shape=jax.ShapeDtypeStruct((t, d), BF16),
        compiler_params=_params(("parallel", "arbitrary")),
        name="merge",
    )(xn, *ys, w_in_t, w_in_t, w_in_t, w_in_t, wb)


def _outproj_kernel(m_ref, x_ref, w_ref, g_ref, x1_ref, xn_ref, wb_scr):
    @pl.when(pl.program_id(0) == 0)
    def _():
        wb_scr[...] = w_ref[0].astype(BF16)

    x1 = x_ref[...] + _dot(m_ref[...], wb_scr[...])
    x1_ref[...] = x1
    ms = jnp.mean(x1 * x1, axis=-1, keepdims=True)
    xn_ref[...] = ((x1 * lax.rsqrt(ms + EPS)) * g_ref[...]).astype(xn_ref.dtype)


def _outproj(merged, x, w_out, g, layer, *, tm):
    t, d = x.shape
    row = pl.BlockSpec((tm, d), lambda i: (i, 0))
    return pl.pallas_call(
        _outproj_kernel,
        grid=(t // tm,),
        in_specs=[row, row,
                  pl.BlockSpec((1, d, d), lambda i: (layer, 0, 0), pipeline_mode=pl.Buffered(1)),
                  pl.BlockSpec((1, d), lambda i: (0, 0))],
        out_specs=[row, row],
        out_shape=[jax.ShapeDtypeStruct((t, d), F32), jax.ShapeDtypeStruct((t, d), BF16)],
        scratch_shapes=[pltpu.VMEM((d, d), BF16)],
        compiler_params=_params(("arbitrary",)),
        name="outproj",
    )(merged, x, w_out, g)


def _mlp_kernel(xn_ref, x_ref, wu_ref, wd_ref, o_ref):
    @pl.when(pl.program_id(1) == 0)
    def _():
        o_ref[...] = x_ref[...]

    h = jnp.maximum(_dot(xn_ref[...], wu_ref[0].astype(BF16)), 0.0)
    o_ref[...] += _dot((h * h).astype(BF16), wd_ref[0].astype(BF16))


def _mlp(xn, x, wu, wd, layer, *, tm, tf):
    t, d = x.shape
    f = wu.shape[2]
    row = pl.BlockSpec((tm, d), lambda i, j: (i, 0))
    return pl.pallas_call(
        _mlp_kernel,
        grid=(t // tm, f // tf),
        in_specs=[row, pl.BlockSpec((tm, d), lambda i, j: (i, 0), pipeline_mode=pl.Buffered(1)),
                  pl.BlockSpec((1, d, tf), lambda i, j: (layer, 0, j)),
                  pl.BlockSpec((1, tf, d), lambda i, j: (layer, j, 0))],
        out_specs=row,
        out_shape=jax.ShapeDtypeStruct((t, d), F32),
        compiler_params=_params(("parallel", "arbitrary")),
        name="mlp",
    )(xn, x, wu, wd)


def _block_diag_pairs(w):
    n2, bd, _ = w.shape
    per = LANES // bd
    w4 = w.reshape(n2 // per, per, bd, bd)
    eye = jnp.eye(per, dtype=w.dtype)
    return jnp.einsum('jaik,ab->jaibk', w4, eye).reshape(n2 // per, LANES, LANES)


def _row(v):
    return v.reshape(1, -1).astype(F32)


def kernel(x, norm_mix_g, w_in, lru_conv_w, lru_conv_b, lru_wr, lru_br, lru_wi, lru_bi, lru_lambda, pool_w, pool_scale, sconv_w, q_norm_g, k_norm_g, forget_b, w_branch, w_out, norm_mlp_g, w_mlp_up, w_mlp_down):
    batch, seq, d = x.shape
    assert d == D_MODEL and seq % ATTN_TQ == 0
    depth = w_in.shape[0]
    t = batch * seq
    xf = x.reshape(t, d)

    lane = jnp.arange(LANES)
    same_head = (lane[:, None] // HEAD_DIM == lane[None, :] // HEAD_DIM).astype(BF16)

    w_in_t = jnp.swapaxes(w_in, 1, 2)

    for l in range(depth):
        proj, xn = _inproj(xf, _row(norm_mix_g[l]), w_in_t, l, tm=1024, tn=N_MIX_PAD // 3)

        y_a = _lru(proj, lru_conv_w[l], _row(lru_conv_b[l]), _block_diag_pairs(lru_wr[l]).astype(BF16),
                   _row(lru_br[l]), _block_diag_pairs(lru_wi[l]).astype(BF16), _row(lru_bi[l]),
                   _row(lru_lambda[l]), batch=batch, seq=seq)
        y_b = _pool(proj, pool_w[l].astype(BF16), _row(pool_scale[l]), batch=batch, seq=seq)
        y_c = _sconv(proj, sconv_w[l], batch=batch, seq=seq)
        fb = jnp.pad(_row(forget_b[l]), ((0, 0), (0, LANES - HEADS)))
        ck, cq = _forget_cumsum(proj, fb, batch=batch, seq=seq)
        qg = jnp.tile(_row(q_norm_g[l]), (1, LANES // HEAD_DIM))
        kg = jnp.tile(_row(k_norm_g[l]), (1, LANES // HEAD_DIM))
        y_d = _attention(proj, ck, cq, qg, kg, same_head, batch=batch, seq=seq)

        merged = _merge(xn, (y_a, y_b, y_c, y_d), w_in_t, w_branch, l, tm=1024, tn=256)
        x1, xn2 = _outproj(merged, xf, w_out, _row(norm_mlp_g[l]), l, tm=512)
        xf = _mlp(xn2, x1, w_mlp_up, w_mlp_down, l, tm=1024, tf=512)

    return xf.reshape(batch, seq, d)
```

```python
import functools

import jax
import jax.numpy as jnp
from jax import lax
from jax.experimental import pallas as pl
from jax.experimental.pallas import tpu as pltpu

F32 = jnp.float32
BF16 = jnp.bfloat16

D_MODEL = 2048
WIDTH = 512
N_BRANCH = 4
LRU_BLOCK_DIM = 64
LRU_CONV = 4
LRU_C = 8.0
POOL_GROUPS = 4
SCONV_K = 3
HEADS = 8
HEAD_DIM = 64
D_FF = 4 * D_MODEL
EPS = 1e-6

LANES = 128
SUBLANES = 8
VMEM_LIMIT = 56 * 1024 * 1024

N_MIX = 2 * WIDTH + 3 * WIDTH + 3 * WIDTH + HEADS
N_MIX_PAD = -(-N_MIX // LANES) * LANES
COL_LRU = 0
COL_POOL = WIDTH
COL_SCONV = 2 * WIDTH
COL_QKV = 5 * WIDTH
COL_FORGET = 8 * WIDTH


def _params(semantics):
    return pltpu.CompilerParams(dimension_semantics=semantics, vmem_limit_bytes=VMEM_LIMIT)


def _dot(a, b):
    return jnp.dot(a, b, preferred_element_type=F32)


def _dot_nt(a, b):
    return lax.dot_general(a, b, (((1,), (1,)), ((), ())), preferred_element_type=F32)


def _shift_rows(x, k, fill):
    n, c = x.shape
    if k >= n:
        return jnp.full((n, c), fill, x.dtype)
    if k % SUBLANES == 0:
        return jnp.concatenate([jnp.full((k, c), fill, x.dtype), x[:n - k]], axis=0)
    r = pltpu.roll(x, k, 0)
    row = lax.broadcasted_iota(jnp.int32, (SUBLANES, c), 0)
    top = jnp.where(row < k, jnp.asarray(fill, x.dtype), r[:SUBLANES])
    return jnp.concatenate([top, r[SUBLANES:]], axis=0)


def _cumsum_rows(x):
    k = 1
    while k < x.shape[0]:
        x = x + _shift_rows(x, k, 0.0)
        k *= 2
    return x


def _linear_scan_rows(a, b):
    k = 1
    n = a.shape[0]
    while k < n:
        b = a * _shift_rows(b, k, 0.0) + b
        if 2 * k < n:
            a = a * _shift_rows(a, k, 1.0)
        k *= 2
    return b


def _softplus(x):
    return jnp.maximum(x, 0.0) + jnp.log1p(jnp.exp(-jnp.abs(x)))


def _inproj_kernel(x_ref, g_ref, w_ref, proj_ref, xn_ref):
    @pl.when(pl.program_id(1) == 0)
    def _():
        x = x_ref[...]
        ms = jnp.mean(x * x, axis=-1, keepdims=True)
        xn_ref[...] = ((x * lax.rsqrt(ms + EPS)) * g_ref[...]).astype(xn_ref.dtype)

    proj_ref[...] = _dot_nt(xn_ref[...], w_ref[0].astype(BF16))


def _inproj(x, g, w_in_t, layer, *, tm, tn):
    t, d = x.shape
    n = N_MIX_PAD
    return pl.pallas_call(
        _inproj_kernel,
        grid=(t // tm, n // tn),
        in_specs=[
            pl.BlockSpec((tm, d), lambda i, j: (i, 0), pipeline_mode=pl.Buffered(1)),
            pl.BlockSpec((1, d), lambda i, j: (0, 0)),
            pl.BlockSpec((1, tn, d), lambda i, j: (layer, j, 0)),
        ],
        out_specs=[
            pl.BlockSpec((tm, tn), lambda i, j: (i, j)),
            pl.BlockSpec((tm, d), lambda i, j: (i, 0)),
        ],
        out_shape=[
            jax.ShapeDtypeStruct((t, n), F32),
            jax.ShapeDtypeStruct((t, d), BF16),
        ],
        compiler_params=_params(("parallel", "arbitrary")),
        name="inproj",
    )(x, g, w_in_t)


def _lru_kernel(x_ref, cw_ref, cb_ref, wr_ref, br_ref, wi_ref, bi_ref, lam_ref, y_ref):
    x = x_ref[...]
    cw = cw_ref[...]
    u = cw[LRU_CONV - 1:LRU_CONV] * x + cb_ref[...]
    for k in range(1, LRU_CONV):
        u = u + cw[LRU_CONV - 1 - k:LRU_CONV - k] * _shift_rows(x, k, 0.0)
    ub = u.astype(BF16)
    r = jax.nn.sigmoid(_dot(ub, wr_ref[0]) + br_ref[...])
    gi = jax.nn.sigmoid(_dot(ub, wi_ref[0]) + bi_ref[...])
    log_a = (-LRU_C * r) * _softplus(-lam_ref[...])
    a = jnp.exp(log_a)
    inp = jnp.sqrt(-jnp.tanh(log_a) * (a * a + 1.0)) * (gi * u)
    y_ref[...] = _linear_scan_rows(a, inp).astype(y_ref.dtype)


def _lru(proj, cw, cb, wr_bd, br, wi_bd, bi, lam, *, batch, seq):
    nc = WIDTH // LANES
    col0 = COL_LRU // LANES
    vec = pl.BlockSpec((1, LANES), lambda b, c: (0, c))
    mat = pl.BlockSpec((1, LANES, LANES), lambda b, c: (c, 0, 0))
    return pl.pallas_call(
        _lru_kernel,
        grid=(batch, nc),
        in_specs=[
            pl.BlockSpec((seq, LANES), lambda b, c: (b, col0 + c)),
            pl.BlockSpec((LRU_CONV, LANES), lambda b, c: (0, c)),
            vec, mat, vec, mat, vec, vec,
        ],
        out_specs=pl.BlockSpec((seq, LANES), lambda b, c: (b, c)),
        out_shape=jax.ShapeDtypeStruct((batch * seq, WIDTH), BF16),
        compiler_params=_params(("parallel", "parallel")),
        name="rglru",
    )(proj, cw, cb, wr_bd, br, wi_bd, bi, lam)


def _pool_kernel(x_ref, w_ref, sc_ref, y_ref):
    g = pl.program_id(1)
    x = x_ref[...]
    w2 = x + _shift_rows(x, 1, 0.0)
    w4 = w2 + _shift_rows(w2, 2, 0.0)
    w8 = w4 + _shift_rows(w4, 4, 0.0)
    w16 = w8 + _shift_rows(w8, 8, 0.0)
    wsum = jnp.where(g == 0, w2, jnp.where(g == 1, w4, jnp.where(g == 2, w8, w16)))
    win = jnp.left_shift(2, g)
    t = lax.broadcasted_iota(jnp.int32, x.shape, 0)
    count = jnp.minimum(t + 1, win).astype(F32)
    pooled = wsum / count - x
    y_ref[...] = (_dot(pooled.astype(BF16), w_ref[0]) * sc_ref[...]).astype(y_ref.dtype)


def _pool(proj, w, scale, *, batch, seq):
    col0 = COL_POOL // LANES
    return pl.pallas_call(
        _pool_kernel,
        grid=(batch, POOL_GROUPS),
        in_specs=[
            pl.BlockSpec((seq, LANES), lambda b, g: (b, col0 + g)),
            pl.BlockSpec((1, LANES, LANES), lambda b, g: (g, 0, 0)),
            pl.BlockSpec((1, LANES), lambda b, g: (0, g)),
        ],
        out_specs=pl.BlockSpec((seq, LANES), lambda b, g: (b, g)),
        out_shape=jax.ShapeDtypeStruct((batch * seq, WIDTH), BF16),
        compiler_params=_params(("parallel", "parallel")),
        name="pool",
    )(proj, w, scale)


def _sconv_kernel(gb_ref, gc_ref, xc_ref, w_ref, y_ref):
    w = w_ref[...]
    z = gc_ref[...] * xc_ref[...]
    acc = w[SCONV_K - 1:SCONV_K] * z
    for k in range(1, SCONV_K):
        acc = acc + w[SCONV_K - 1 - k:SCONV_K - k] * _shift_rows(z, k, 0.0)
    y_ref[...] = (gb_ref[...] * acc).astype(y_ref.dtype)


def _sconv(proj, w, *, batch, seq):
    nc = WIDTH // LANES
    col0 = COL_SCONV // LANES

    def part(p):
        return pl.BlockSpec((seq, LANES), lambda b, c: (b, col0 + p * nc + c))

    return pl.pallas_call(
        _sconv_kernel,
        grid=(batch, nc),
        in_specs=[part(0), part(1), part(2),
                  pl.BlockSpec((SCONV_K, LANES), lambda b, c: (0, c))],
        out_specs=pl.BlockSpec((seq, LANES), lambda b, c: (b, c)),
        out_shape=jax.ShapeDtypeStruct((batch * seq, WIDTH), BF16),
        compiler_params=_params(("parallel", "parallel")),
        name="sconv",
    )(proj, proj, proj, w)


def _forget_kernel(f_ref, fb_ref, ck_ref, cq_ref):
    z = f_ref[...] + fb_ref[...]
    log_f = jnp.minimum(z, 0.0) - jnp.log1p(jnp.exp(-jnp.abs(z)))
    c = _cumsum_rows(log_f)
    ct = c.T
    for h in range(HEADS):
        ck_ref[0, h] = ct[h:h + 1, :]
        cq_ref[0, h] = jnp.broadcast_to(c[:, h:h + 1], c.shape)


def _forget_cumsum(proj, fb, *, batch, seq):
    col0 = COL_FORGET // LANES
    return pl.pallas_call(
        _forget_kernel,
        grid=(batch,),
        in_specs=[
            pl.BlockSpec((seq, LANES), lambda b: (b, col0)),
            pl.BlockSpec((1, LANES), lambda b: (0, 0)),
        ],
        out_specs=[
            pl.BlockSpec((1, HEADS, 1, seq), lambda b: (b, 0, 0, 0)),
            pl.BlockSpec((1, HEADS, seq, LANES), lambda b: (b, 0, 0, 0)),
        ],
        out_shape=[
            jax.ShapeDtypeStruct((batch, HEADS, 1, seq), F32),
            jax.ShapeDtypeStruct((batch, HEADS, seq, LANES), F32),
        ],
        compiler_params=_params(("parallel",)),
        name="forget_cumsum",
    )(proj, fb)


ATTN_TQ = 256


def _attn_kernel(q_ref, k_ref, v_ref, cka_ref, ckb_ref, cqa_ref, cqb_ref, qg_ref, kg_ref, hm_ref,
                 wu_ref, wd_ref, o_ref, wub_ref, wdb_ref, qa_scr, qb_scr, kn_scr, vb_scr):
    wub_ref[...] = wu_ref[0].astype(BF16)
    wdb_ref[...] = wd_ref[0].astype(BF16)

    seq = q_ref.shape[0]
    hm = hm_ref[...]

    def head_norm(x, g):
        sq = x * x
        hi = sq.astype(BF16)
        lo = (sq - hi.astype(F32)).astype(BF16)
        ms = (_dot(hi, hm) + _dot(lo, hm)) * (1.0 / HEAD_DIM)
        return (x * lax.rsqrt(ms + EPS)) * g

    qn = head_norm(q_ref[...], qg_ref[...]) * (HEAD_DIM ** -0.5)
    first = lax.broadcasted_iota(jnp.int32, (1, LANES), 1) < HEAD_DIM
    qa_scr[...] = jnp.where(first, qn, 0.0).astype(BF16)
    qb_scr[...] = jnp.where(first, 0.0, qn).astype(BF16)
    kn_scr[...] = head_norm(k_ref[...], kg_ref[...]).astype(BF16)
    vb_scr[...] = v_ref[...].astype(BF16)

    tq = ATTN_TQ
    for i in range(seq // tq):
        q0, kend = i * tq, (i + 1) * tq
        row = lax.broadcasted_iota(jnp.int32, (tq, kend), 0) + q0
        col = lax.broadcasted_iota(jnp.int32, (tq, kend), 1)
        causal = col <= row
        outs = []
        for q_scr, ck_ref, cq_ref in ((qa_scr, cka_ref, cqa_ref), (qb_scr, ckb_ref, cqb_ref)):
            s = lax.dot_general(q_scr[q0:kend, :], kn_scr[:kend, :],
                                (((1,), (1,)), ((), ())), preferred_element_type=F32)
            s = s + (cq_ref[0, 0, q0:kend, 0:1] - ck_ref[0, 0, :, :kend])
            s = jnp.where(causal, s, -jnp.inf)
            m = jnp.max(s, axis=1, keepdims=True)
            p = jnp.exp(s - m)
            l = jnp.sum(p, axis=1, keepdims=True)
            outs.append(_dot(p.astype(BF16), vb_scr[:kend, :]) / l)
        o_ref[q0:kend, :] = jnp.where(first, outs[0], outs[1]).astype(o_ref.dtype)


def _attention(proj, ck, cq, qg, kg, hm, wu, wd, layer, *, batch, seq):
    npair = HEADS // 2
    col0 = COL_QKV // LANES
    steps = batch * npair
    _, d, f = wu.shape
    ru, rd = d // steps, f // steps

    def part(p):
        return pl.BlockSpec((seq, LANES), lambda b, h: (b, col0 + p * npair + h))

    def ck_spec(o):
        return pl.BlockSpec((1, 1, 1, seq), lambda b, h: (b, 2 * h + o, 0, 0))

    def cq_spec(o):
        return pl.BlockSpec((1, 1, seq, LANES), lambda b, h: (b, 2 * h + o, 0, 0))

    vec = pl.BlockSpec((1, LANES), lambda b, h: (0, 0))
    return pl.pallas_call(
        _attn_kernel,
        grid=(batch, npair),
        in_specs=[part(0), part(1), part(2), ck_spec(0), ck_spec(1), cq_spec(0), cq_spec(1),
                  vec, vec, pl.BlockSpec((LANES, LANES), lambda b, h: (0, 0)),
                  pl.BlockSpec((1, ru, f), lambda b, h: (layer, b * npair + h, 0)),
                  pl.BlockSpec((1, rd, d), lambda b, h: (layer, b * npair + h, 0))],
        out_specs=[pl.BlockSpec((seq, LANES), lambda b, h: (b, h)),
                   pl.BlockSpec((ru, f), lambda b, h: (b * npair + h, 0)),
                   pl.BlockSpec((rd, d), lambda b, h: (b * npair + h, 0))],
        out_shape=[jax.ShapeDtypeStruct((batch * seq, WIDTH), BF16),
                   jax.ShapeDtypeStruct((d, f), BF16),
                   jax.ShapeDtypeStruct((f, d), BF16)],
        scratch_shapes=[pltpu.VMEM((seq, LANES), BF16)] * 4,
        compiler_params=_params(("parallel", "parallel")),
        name="fox_attention",
    )(proj, proj, proj, ck, ck, cq, cq, qg, kg, hm, wu, wd)


def _merge_kernel(xn_ref, ya_ref, yb_ref, yc_ref, yd_ref, wg0_ref, wg1_ref, wg2_ref, wg3_ref, wb_ref, o_ref):
    xn = xn_ref[...]
    acc = None
    for k, (y_ref, wg_ref) in enumerate(((ya_ref, wg0_ref), (yb_ref, wg1_ref),
                                         (yc_ref, wg2_ref), (yd_ref, wg3_ref))):
        gate = jax.nn.sigmoid(_dot_nt(xn, wg_ref[...].astype(BF16)))
        term = gate * _dot(y_ref[...], wb_ref[0, k].astype(BF16))
        acc = term if acc is None else acc + term
    o_ref[...] = acc.astype(o_ref.dtype)


def _merge(xn, ys, w_in_t, wb, layer, *, tm, tn):
    t, d = xn.shape
    nblk = d // tn

    def wg_spec(k):
        return pl.BlockSpec((pl.Squeezed(), pl.Element(tn), pl.Element(d)),
                            lambda i, j: (layer, pl.multiple_of(N_MIX + k * d + j * tn, SUBLANES), 0))

    y_spec = pl.BlockSpec((tm, WIDTH), lambda i, j: (i, 0))
    return pl.pallas_call(
        _merge_kernel,
        grid=(t // tm, nblk),
        in_specs=[pl.BlockSpec((tm, d), lambda i, j: (i, 0)), y_spec, y_spec, y_spec, y_spec,
                  wg_spec(0), wg_spec(1), wg_spec(2), wg_spec(3),
                  pl.BlockSpec((1, N_BRANCH, WIDTH, tn), lambda i, j: (layer, 0, 0, j))],
        out_specs=pl.BlockSpec((tm, tn), lambda i, j: (i, j)),
        out_shape=jax.ShapeDtypeStruct((t, d), BF16),
        compiler_params=_params(("parallel", "arbitrary")),
        name="merge",
    )(xn, *ys, w_in_t, w_in_t, w_in_t, w_in_t, wb)


def _outproj_kernel(m_ref, x_ref, w_ref, g_ref, x1_ref, xn_ref, wb_scr):
    @pl.when(pl.program_id(0) == 0)
    def _():
        wb_scr[...] = w_ref[0].astype(BF16)

    x1 = x_ref[...] + _dot(m_ref[...], wb_scr[...])
    x1_ref[...] = x1
    ms = jnp.mean(x1 * x1, axis=-1, keepdims=True)
    xn_ref[...] = ((x1 * lax.rsqrt(ms + EPS)) * g_ref[...]).astype(xn_ref.dtype)


def _outproj(merged, x, w_out, g, layer, *, tm):
    t, d = x.shape
    row = pl.BlockSpec((tm, d), lambda i: (i, 0))
    return pl.pallas_call(
        _outproj_kernel,
        grid=(t // tm,),
        in_specs=[row, row,
                  pl.BlockSpec((1, d, d), lambda i: (layer, 0, 0), pipeline_mode=pl.Buffered(1)),
                  pl.BlockSpec((1, d), lambda i: (0, 0))],
        out_specs=[row, row],
        out_shape=[jax.ShapeDtypeStruct((t, d), F32), jax.ShapeDtypeStruct((t, d), BF16)],
        scratch_shapes=[pltpu.VMEM((d, d), BF16)],
        compiler_params=_params(("arbitrary",)),
        name="outproj",
    )(merged, x, w_out, g)


def _mlp_kernel(xn_ref, x_ref, wu_ref, wd_ref, o_ref):
    @pl.when(pl.program_id(1) == 0)
    def _():
        o_ref[...] = x_ref[...]

    h = jnp.maximum(_dot(xn_ref[...], wu_ref[...]), 0.0)
    o_ref[...] += _dot((h * h).astype(BF16), wd_ref[...])


def _mlp(xn, x, wu, wd, *, tm, tf):
    t, d = x.shape
    f = wu.shape[1]
    row = pl.BlockSpec((tm, d), lambda i, j: (i, 0))
    return pl.pallas_call(
        _mlp_kernel,
        grid=(t // tm, f // tf),
        in_specs=[row, row,
                  pl.BlockSpec((d, tf), lambda i, j: (0, j)),
                  pl.BlockSpec((tf, d), lambda i, j: (j, 0))],
        out_specs=row,
        out_shape=jax.ShapeDtypeStruct((t, d), F32),
        compiler_params=_params(("parallel", "arbitrary")),
        name="mlp",
    )(xn, x, wu, wd)


def _block_diag_pairs(w):
    n2, bd, _ = w.shape
    per = LANES // bd
    w4 = w.reshape(n2 // per, per, bd, bd)
    eye = jnp.eye(per, dtype=w.dtype)
    return jnp.einsum('jaik,ab->jaibk', w4, eye).reshape(n2 // per, LANES, LANES)


def _row(v):
    return v.reshape(1, -1).astype(F32)


def kernel(x, norm_mix_g, w_in, lru_conv_w, lru_conv_b, lru_wr, lru_br, lru_wi, lru_bi, lru_lambda, pool_w, pool_scale, sconv_w, q_norm_g, k_norm_g, forget_b, w_branch, w_out, norm_mlp_g, w_mlp_up, w_mlp_down):
    batch, seq, d = x.shape
    assert d == D_MODEL and seq % ATTN_TQ == 0
    depth = w_in.shape[0]
    t = batch * seq
    xf = x.reshape(t, d)

    lane = jnp.arange(LANES)
    same_head = (lane[:, None] // HEAD_DIM == lane[None, :] // HEAD_DIM).astype(BF16)

    w_in_t = jnp.swapaxes(w_in, 1, 2)

    for l in range(depth):
        proj, xn = _inproj(xf, _row(norm_mix_g[l]), w_in_t, l, tm=1024, tn=N_MIX_PAD // 3)

        y_a = _lru(proj, lru_conv_w[l], _row(lru_conv_b[l]), _block_diag_pairs(lru_wr[l]).astype(BF16),
                   _row(lru_br[l]), _block_diag_pairs(lru_wi[l]).astype(BF16), _row(lru_bi[l]),
                   _row(lru_lambda[l]), batch=batch, seq=seq)
        y_b = _pool(proj, pool_w[l].astype(BF16), _row(pool_scale[l]), batch=batch, seq=seq)
        y_c = _sconv(proj, sconv_w[l], batch=batch, seq=seq)
        fb = jnp.pad(_row(forget_b[l]), ((0, 0), (0, LANES - HEADS)))
        ck, cq = _forget_cumsum(proj, fb, batch=batch, seq=seq)
        qg = jnp.tile(_row(q_norm_g[l]), (1, LANES // HEAD_DIM))
        kg = jnp.tile(_row(k_norm_g[l]), (1, LANES // HEAD_DIM))
        y_d, wu_bf, wd_bf = _attention(proj, ck, cq, qg, kg, same_head, w_mlp_up, w_mlp_down, l,
                                       batch=batch, seq=seq)

        merged = _merge(xn, (y_a, y_b, y_c, y_d), w_in_t, w_branch, l, tm=1024, tn=256)
        x1, xn2 = _outproj(merged, xf, w_out, _row(norm_mlp_g[l]), l, tm=512)
        xf = _mlp(xn2, x1, wu_bf, wd_bf, tm=512, tf=2048)

    return xf.reshape(batch, seq, d)
```

```python
import functools

import jax
import jax.numpy as jnp
from jax import lax
from jax.experimental import pallas as pl
from jax.experimental.pallas import tpu as pltpu

F32 = jnp.float32
BF16 = jnp.bfloat16

D_MODEL = 2048
WIDTH = 512
N_BRANCH = 4
LRU_BLOCK_DIM = 64
LRU_CONV = 4
LRU_C = 8.0
POOL_GROUPS = 4
SCONV_K = 3
HEADS = 8
HEAD_DIM = 64
D_FF = 4 * D_MODEL
EPS = 1e-6

LANES = 128
SUBLANES = 8
VMEM_LIMIT = 56 * 1024 * 1024

N_MIX = 2 * WIDTH + 3 * WIDTH + 3 * WIDTH + HEADS
N_MIX_PAD = -(-N_MIX // LANES) * LANES
COL_LRU = 0
COL_POOL = WIDTH
COL_SCONV = 2 * WIDTH
COL_QKV = 5 * WIDTH
COL_FORGET = 8 * WIDTH


def _params(semantics):
    return pltpu.CompilerParams(dimension_semantics=semantics, vmem_limit_bytes=VMEM_LIMIT)


def _dot(a, b):
    return jnp.dot(a, b, preferred_element_type=F32)


def _dot_nt(a, b):
    return lax.dot_general(a, b, (((1,), (1,)), ((), ())), preferred_element_type=F32)


def _shift_rows(x, k, fill):
    n, c = x.shape
    if k >= n:
        return jnp.full((n, c), fill, x.dtype)
    if k % SUBLANES == 0:
        return jnp.concatenate([jnp.full((k, c), fill, x.dtype), x[:n - k]], axis=0)
    r = pltpu.roll(x, k, 0)
    row = lax.broadcasted_iota(jnp.int32, (SUBLANES, c), 0)
    top = jnp.where(row < k, jnp.asarray(fill, x.dtype), r[:SUBLANES])
    return jnp.concatenate([top, r[SUBLANES:]], axis=0)


def _cumsum_rows(x):
    k = 1
    while k < x.shape[0]:
        x = x + _shift_rows(x, k, 0.0)
        k *= 2
    return x


def _linear_scan_rows(a, b):
    k = 1
    n = a.shape[0]
    while k < n:
        b = a * _shift_rows(b, k, 0.0) + b
        if 2 * k < n:
            a = a * _shift_rows(a, k, 1.0)
        k *= 2
    return b


def _softplus(x):
    return jnp.maximum(x, 0.0) + jnp.log1p(jnp.exp(-jnp.abs(x)))


def _wcast_kernel(w_ref, o_ref):
    o_ref[...] = w_ref[...].astype(o_ref.dtype)


def _cast_mix_weights(w_in_t, *, rows):
    depth, _, d = w_in_t.shape
    spec = pl.BlockSpec((1, rows, d), lambda l, j: (l, j, 0))
    return pl.pallas_call(
        _wcast_kernel,
        grid=(depth, N_MIX_PAD // rows),
        in_specs=[spec],
        out_specs=spec,
        out_shape=jax.ShapeDtypeStruct((depth, N_MIX_PAD, d), BF16),
        compiler_params=_params(("parallel", "parallel")),
        name="cast_mix_weights",
    )(w_in_t)


def _inproj_kernel(x_ref, g_ref, w_ref, proj_ref, xn_ref):
    x = x_ref[...]
    ms = jnp.mean(x * x, axis=-1, keepdims=True)
    xn = ((x * lax.rsqrt(ms + EPS)) * g_ref[...]).astype(xn_ref.dtype)
    xn_ref[...] = xn
    proj_ref[...] = _dot_nt(xn, w_ref[0])


def _inproj(x, g, w_mix, layer, *, tm):
    t, d = x.shape
    n = w_mix.shape[1]
    return pl.pallas_call(
        _inproj_kernel,
        grid=(t // tm,),
        in_specs=[
            pl.BlockSpec((tm, d), lambda i: (i, 0)),
            pl.BlockSpec((1, d), lambda i: (0, 0)),
            pl.BlockSpec((1, n, d), lambda i: (layer, 0, 0), pipeline_mode=pl.Buffered(1)),
        ],
        out_specs=[
            pl.BlockSpec((tm, n), lambda i: (i, 0)),
            pl.BlockSpec((tm, d), lambda i: (i, 0)),
        ],
        out_shape=[
            jax.ShapeDtypeStruct((t, n), F32),
            jax.ShapeDtypeStruct((t, d), BF16),
        ],
        compiler_params=_params(("parallel",)),
        name="inproj",
    )(x, g, w_mix)


def _lru_kernel(x_ref, cw_ref, cb_ref, wr_ref, br_ref, wi_ref, bi_ref, lam_ref, y_ref):
    x = x_ref[...]
    cw = cw_ref[...]
    u = cw[LRU_CONV - 1:LRU_CONV] * x + cb_ref[...]
    for k in range(1, LRU_CONV):
        u = u + cw[LRU_CONV - 1 - k:LRU_CONV - k] * _shift_rows(x, k, 0.0)
    ub = u.astype(BF16)
    r = jax.nn.sigmoid(_dot(ub, wr_ref[0]) + br_ref[...])
    gi = jax.nn.sigmoid(_dot(ub, wi_ref[0]) + bi_ref[...])
    log_a = (-LRU_C * r) * _softplus(-lam_ref[...])
    a = jnp.exp(log_a)
    inp = jnp.sqrt(-jnp.tanh(log_a) * (a * a + 1.0)) * (gi * u)
    y_ref[...] = _linear_scan_rows(a, inp).astype(y_ref.dtype)


def _lru(proj, cw, cb, wr_bd, br, wi_bd, bi, lam, *, batch, seq):
    nc = WIDTH // LANES
    col0 = COL_LRU // LANES
    vec = pl.BlockSpec((1, LANES), lambda b, c: (0, c))
    mat = pl.BlockSpec((1, LANES, LANES), lambda b, c: (c, 0, 0))
    return pl.pallas_call(
        _lru_kernel,
        grid=(batch, nc),
        in_specs=[
            pl.BlockSpec((seq, LANES), lambda b, c: (b, col0 + c)),
            pl.BlockSpec((LRU_CONV, LANES), lambda b, c: (0, c)),
            vec, mat, vec, mat, vec, vec,
        ],
        out_specs=pl.BlockSpec((seq, LANES), lambda b, c: (b, c)),
        out_shape=jax.ShapeDtypeStruct((batch * seq, WIDTH), BF16),
        compiler_params=_params(("parallel", "parallel")),
        name="rglru",
    )(proj, cw, cb, wr_bd, br, wi_bd, bi, lam)


def _pool_kernel(x_ref, w_ref, sc_ref, y_ref):
    g = pl.program_id(1)
    x = x_ref[...]
    w2 = x + _shift_rows(x, 1, 0.0)
    w4 = w2 + _shift_rows(w2, 2, 0.0)
    w8 = w4 + _shift_rows(w4, 4, 0.0)
    w16 = w8 + _shift_rows(w8, 8, 0.0)
    wsum = jnp.where(g == 0, w2, jnp.where(g == 1, w4, jnp.where(g == 2, w8, w16)))
    win = jnp.left_shift(2, g)
    t = lax.broadcasted_iota(jnp.int32, x.shape, 0)
    count = jnp.minimum(t + 1, win).astype(F32)
    pooled = wsum / count - x
    y_ref[...] = (_dot(pooled.astype(BF16), w_ref[0]) * sc_ref[...]).astype(y_ref.dtype)


def _pool(proj, w, scale, *, batch, seq):
    col0 = COL_POOL // LANES
    return pl.pallas_call(
        _pool_kernel,
        grid=(batch, POOL_GROUPS),
        in_specs=[
            pl.BlockSpec((seq, LANES), lambda b, g: (b, col0 + g)),
            pl.BlockSpec((1, LANES, LANES), lambda b, g: (g, 0, 0)),
            pl.BlockSpec((1, LANES), lambda b, g: (0, g)),
        ],
        out_specs=pl.BlockSpec((seq, LANES), lambda b, g: (b, g)),
        out_shape=jax.ShapeDtypeStruct((batch * seq, WIDTH), BF16),
        compiler_params=_params(("parallel", "parallel")),
        name="pool",
    )(proj, w, scale)


def _sconv_kernel(gb_ref, gc_ref, xc_ref, w_ref, y_ref):
    w = w_ref[...]
    z = gc_ref[...] * xc_ref[...]
    acc = w[SCONV_K - 1:SCONV_K] * z
    for k in range(1, SCONV_K):
        acc = acc + w[SCONV_K - 1 - k:SCONV_K - k] * _shift_rows(z, k, 0.0)
    y_ref[...] = (gb_ref[...] * acc).astype(y_ref.dtype)


def _sconv(proj, w, *, batch, seq):
    nc = WIDTH // LANES
    col0 = COL_SCONV // LANES

    def part(p):
        return pl.BlockSpec((seq, LANES), lambda b, c: (b, col0 + p * nc + c))

    return pl.pallas_call(
        _sconv_kernel,
        grid=(batch, nc),
        in_specs=[part(0), part(1), part(2),
                  pl.BlockSpec((SCONV_K, LANES), lambda b, c: (0, c))],
        out_specs=pl.BlockSpec((seq, LANES), lambda b, c: (b, c)),
        out_shape=jax.ShapeDtypeStruct((batch * seq, WIDTH), BF16),
        compiler_params=_params(("parallel", "parallel")),
        name="sconv",
    )(proj, proj, proj, w)


def _forget_kernel(f_ref, fb_ref, ck_ref, cq_ref):
    z = f_ref[...] + fb_ref[...]
    log_f = jnp.minimum(z, 0.0) - jnp.log1p(jnp.exp(-jnp.abs(z)))
    c = _cumsum_rows(log_f)
    ct = c.T
    for h in range(HEADS):
        ck_ref[0, h] = ct[h:h + 1, :]
        cq_ref[0, h] = jnp.broadcast_to(c[:, h:h + 1], c.shape)


def _forget_cumsum(proj, fb, *, batch, seq):
    col0 = COL_FORGET // LANES
    return pl.pallas_call(
        _forget_kernel,
        grid=(batch,),
        in_specs=[
            pl.BlockSpec((seq, LANES), lambda b: (b, col0)),
            pl.BlockSpec((1, LANES), lambda b: (0, 0)),
        ],
        out_specs=[
            pl.BlockSpec((1, HEADS, 1, seq), lambda b: (b, 0, 0, 0)),
            pl.BlockSpec((1, HEADS, seq, LANES), lambda b: (b, 0, 0, 0)),
        ],
        out_shape=[
            jax.ShapeDtypeStruct((batch, HEADS, 1, seq), F32),
            jax.ShapeDtypeStruct((batch, HEADS, seq, LANES), F32),
        ],
        compiler_params=_params(("parallel",)),
        name="forget_cumsum",
    )(proj, fb)


ATTN_TQ = 256


def _attn_kernel(q_ref, k_ref, v_ref, cka_ref, ckb_ref, cqa_ref, cqb_ref, qg_ref, kg_ref, hm_ref,
                 wu_ref, wd_ref, o_ref, wub_ref, wdb_ref, qa_scr, qb_scr, kn_scr, vb_scr):
    wub_ref[...] = wu_ref[0].astype(BF16)
    wdb_ref[...] = wd_ref[0].astype(BF16)

    seq = q_ref.shape[0]
    hm = hm_ref[...]

    def head_norm(x, g):
        sq = x * x
        hi = sq.astype(BF16)
        lo = (sq - hi.astype(F32)).astype(BF16)
        ms = (_dot(hi, hm) + _dot(lo, hm)) * (1.0 / HEAD_DIM)
        return (x * lax.rsqrt(ms + EPS)) * g

    qn = head_norm(q_ref[...], qg_ref[...]) * (HEAD_DIM ** -0.5)
    first = lax.broadcasted_iota(jnp.int32, (1, LANES), 1) < HEAD_DIM
    qa_scr[...] = jnp.where(first, qn, 0.0).astype(BF16)
    qb_scr[...] = jnp.where(first, 0.0, qn).astype(BF16)
    kn_scr[...] = head_norm(k_ref[...], kg_ref[...]).astype(BF16)
    vb_scr[...] = v_ref[...].astype(BF16)

    tq = ATTN_TQ
    for i in range(seq // tq):
        q0, kend = i * tq, (i + 1) * tq
        row = lax.broadcasted_iota(jnp.int32, (tq, kend), 0) + q0
        col = lax.broadcasted_iota(jnp.int32, (tq, kend), 1)
        causal = col <= row
        outs = []
        for q_scr, ck_ref, cq_ref in ((qa_scr, cka_ref, cqa_ref), (qb_scr, ckb_ref, cqb_ref)):
            s = lax.dot_general(q_scr[q0:kend, :], kn_scr[:kend, :],
                                (((1,), (1,)), ((), ())), preferred_element_type=F32)
            s = s + (cq_ref[0, 0, q0:kend, 0:1] - ck_ref[0, 0, :, :kend])
            s = jnp.where(causal, s, -jnp.inf)
            m = jnp.max(s, axis=1, keepdims=True)
            p = jnp.exp(s - m)
            l = jnp.sum(p, axis=1, keepdims=True)
            outs.append(_dot(p.astype(BF16), vb_scr[:kend, :]) / l)
        o_ref[q0:kend, :] = jnp.where(first, outs[0], outs[1]).astype(o_ref.dtype)


def _attention(proj, ck, cq, qg, kg, hm, wu, wd, layer, *, batch, seq):
    npair = HEADS // 2
    col0 = COL_QKV // LANES
    steps = batch * npair
    _, d, f = wu.shape
    ru, rd = d // steps, f // steps

    def part(p):
        return pl.BlockSpec((seq, LANES), lambda b, h: (b, col0 + p * npair + h))

    def ck_spec(o):
        return pl.BlockSpec((1, 1, 1, seq), lambda b, h: (b, 2 * h + o, 0, 0))

    def cq_spec(o):
        return pl.BlockSpec((1, 1, seq, LANES), lambda b, h: (b, 2 * h + o, 0, 0))

    vec = pl.BlockSpec((1, LANES), lambda b, h: (0, 0))
    return pl.pallas_call(
        _attn_kernel,
        grid=(batch, npair),
        in_specs=[part(0), part(1), part(2), ck_spec(0), ck_spec(1), cq_spec(0), cq_spec(1),
                  vec, vec, pl.BlockSpec((LANES, LANES), lambda b, h: (0, 0)),
                  pl.BlockSpec((1, ru, f), lambda b, h: (layer, b * npair + h, 0)),
                  pl.BlockSpec((1, rd, d), lambda b, h: (layer, b * npair + h, 0))],
        out_specs=[pl.BlockSpec((seq, LANES), lambda b, h: (b, h)),
                   pl.BlockSpec((ru, f), lambda b, h: (b * npair + h, 0)),
                   pl.BlockSpec((rd, d), lambda b, h: (b * npair + h, 0))],
        out_shape=[jax.ShapeDtypeStruct((batch * seq, WIDTH), BF16),
                   jax.ShapeDtypeStruct((d, f), BF16),
                   jax.ShapeDtypeStruct((f, d), BF16)],
        scratch_shapes=[pltpu.VMEM((seq, LANES), BF16)] * 4,
        compiler_params=_params(("parallel", "parallel")),
        name="fox_attention",
    )(proj, proj, proj, ck, ck, cq, cq, qg, kg, hm, wu, wd)


def _merge_kernel(xn_ref, ya_ref, yb_ref, yc_ref, yd_ref, wg0_ref, wg1_ref, wg2_ref, wg3_ref, wb_ref, o_ref):
    xn = xn_ref[...]
    acc = None
    for k, (y_ref, wg_ref) in enumerate(((ya_ref, wg0_ref), (yb_ref, wg1_ref),
                                         (yc_ref, wg2_ref), (yd_ref, wg3_ref))):
        gate = jax.nn.sigmoid(_dot_nt(xn, wg_ref[...].astype(BF16)))
        term = gate * _dot(y_ref[...], wb_ref[0, k].astype(BF16))
        acc = term if acc is None else acc + term
    o_ref[...] = acc.astype(o_ref.dtype)


def _merge(xn, ys, w_in_t, wb, layer, *, tm, tn):
    t, d = xn.shape
    nblk = d // tn

    def wg_spec(k):
        return pl.BlockSpec((pl.Squeezed(), pl.Element(tn), pl.Element(d)),
                            lambda i, j: (layer, pl.multiple_of(N_MIX + k * d + j * tn, SUBLANES), 0))

    y_spec = pl.BlockSpec((tm, WIDTH), lambda i, j: (i, 0))
    return pl.pallas_call(
        _merge_kernel,
        grid=(t // tm, nblk),
        in_specs=[pl.BlockSpec((tm, d), lambda i, j: (i, 0)), y_spec, y_spec, y_spec, y_spec,
                  wg_spec(0), wg_spec(1), wg_spec(2), wg_spec(3),
                  pl.BlockSpec((1, N_BRANCH, WIDTH, tn), lambda i, j: (layer, 0, 0, j))],
        out_specs=pl.BlockSpec((tm, tn), lambda i, j: (i, j)),
        out_shape=jax.ShapeDtypeStruct((t, d), BF16),
        compiler_params=_params(("parallel", "arbitrary")),
        name="merge",
    )(xn, *ys, w_in_t, w_in_t, w_in_t, w_in_t, wb)


def _outproj_kernel(m_ref, x_ref, w_ref, g_ref, x1_ref, xn_ref, wb_scr):
    @pl.when(pl.program_id(0) == 0)
    def _():
        wb_scr[...] = w_ref[0].astype(BF16)

    x1 = x_ref[...] + _dot(m_ref[...], wb_scr[...])
    x1_ref[...] = x1
    ms = jnp.mean(x1 * x1, axis=-1, keepdims=True)
    xn_ref[...] = ((x1 * lax.rsqrt(ms + EPS)) * g_ref[...]).astype(xn_ref.dtype)


def _outproj(merged, x, w_out, g, layer, *, tm):
    t, d = x.shape
    row = pl.BlockSpec((tm, d), lambda i: (i, 0))
    return pl.pallas_call(
        _outproj_kernel,
        grid=(t // tm,),
        in_specs=[row, row,
                  pl.BlockSpec((1, d, d), lambda i: (layer, 0, 0), pipeline_mode=pl.Buffered(1)),
                  pl.BlockSpec((1, d), lambda i: (0, 0))],
        out_specs=[row, row],
        out_shape=[jax.ShapeDtypeStruct((t, d), F32), jax.ShapeDtypeStruct((t, d), BF16)],
        scratch_shapes=[pltpu.VMEM((d, d), BF16)],
        compiler_params=_params(("arbitrary",)),
        name="outproj",
    )(merged, x, w_out, g)


def _mlp_kernel(xn_ref, x_ref, wu_ref, wd_ref, o_ref):
    @pl.when(pl.program_id(1) == 0)
    def _():
        o_ref[...] = x_ref[...]

    h = jnp.maximum(_dot(xn_ref[...], wu_ref[...]), 0.0)
    o_ref[...] += _dot((h * h).astype(BF16), wd_ref[...])


def _mlp(xn, x, wu, wd, *, tm, tf):
    t, d = x.shape
    f = wu.shape[1]
    row = pl.BlockSpec((tm, d), lambda i, j: (i, 0))
    return pl.pallas_call(
        _mlp_kernel,
        grid=(t // tm, f // tf),
        in_specs=[row, row,
                  pl.BlockSpec((d, tf), lambda i, j: (0, j)),
                  pl.BlockSpec((tf, d), lambda i, j: (j, 0))],
        out_specs=row,
        out_shape=jax.ShapeDtypeStruct((t, d), F32),
        compiler_params=_params(("parallel", "arbitrary")),
        name="mlp",
    )(xn, x, wu, wd)


def _block_diag_pairs(w):
    n2, bd, _ = w.shape
    per = LANES // bd
    w4 = w.reshape(n2 // per, per, bd, bd)
    eye = jnp.eye(per, dtype=w.dtype)
    return jnp.einsum('jaik,ab->jaibk', w4, eye).reshape(n2 // per, LANES, LANES)


def _row(v):
    return v.reshape(1, -1).astype(F32)


def kernel(x, norm_mix_g, w_in, lru_conv_w, lru_conv_b, lru_wr, lru_br, lru_wi, lru_bi, lru_lambda, pool_w, pool_scale, sconv_w, q_norm_g, k_norm_g, forget_b, w_branch, w_out, norm_mlp_g, w_mlp_up, w_mlp_down):
    batch, seq, d = x.shape
    assert d == D_MODEL and seq % ATTN_TQ == 0
    depth = w_in.shape[0]
    t = batch * seq
    xf = x.reshape(t, d)

    lane = jnp.arange(LANES)
    same_head = (lane[:, None] // HEAD_DIM == lane[None, :] // HEAD_DIM).astype(BF16)

    w_in_t = jnp.swapaxes(w_in, 1, 2)
    w_mix = _cast_mix_weights(w_in_t, rows=N_MIX_PAD // 8)

    for l in range(depth):
        proj, xn = _inproj(xf, _row(norm_mix_g[l]), w_mix, l, tm=512)

        y_a = _lru(proj, lru_conv_w[l], _row(lru_conv_b[l]), _block_diag_pairs(lru_wr[l]).astype(BF16),
                   _row(lru_br[l]), _block_diag_pairs(lru_wi[l]).astype(BF16), _row(lru_bi[l]),
                   _row(lru_lambda[l]), batch=batch, seq=seq)
        y_b = _pool(proj, pool_w[l].astype(BF16), _row(pool_scale[l]), batch=batch, seq=seq)
        y_c = _sconv(proj, sconv_w[l], batch=batch, seq=seq)
        fb = jnp.pad(_row(forget_b[l]), ((0, 0), (0, LANES - HEADS)))
        ck, cq = _forget_cumsum(proj, fb, batch=batch, seq=seq)
        qg = jnp.tile(_row(q_norm_g[l]), (1, LANES // HEAD_DIM))
        kg = jnp.tile(_row(k_norm_g[l]), (1, LANES // HEAD_DIM))
        y_d, wu_bf, wd_bf = _attention(proj, ck, cq, qg, kg, same_head, w_mlp_up, w_mlp_down, l,
                                       batch=batch, seq=seq)

        merged = _merge(xn, (y_a, y_b, y_c, y_d), w_in_t, w_branch, l, tm=1024, tn=256)
        x1, xn2 = _outproj(merged, xf, w_out, _row(norm_mlp_g[l]), l, tm=512)
        xf = _mlp(xn2, x1, wu_bf, wd_bf, tm=512, tf=2048)

    return xf.reshape(batch, seq, d)
```

```python
import functools

import jax
import jax.numpy as jnp
from jax import lax
from jax.experimental import pallas as pl
from jax.experimental.pallas import tpu as pltpu

F32 = jnp.float32
BF16 = jnp.bfloat16

D_MODEL = 2048
WIDTH = 512
N_BRANCH = 4
LRU_BLOCK_DIM = 64
LRU_CONV = 4
LRU_C = 8.0
POOL_GROUPS = 4
SCONV_K = 3
HEADS = 8
HEAD_DIM = 64
D_FF = 4 * D_MODEL
EPS = 1e-6

LANES = 128
SUBLANES = 8
VMEM_LIMIT = 56 * 1024 * 1024

N_MIX = 2 * WIDTH + 3 * WIDTH + 3 * WIDTH + HEADS
N_MIX_PAD = -(-N_MIX // LANES) * LANES
COL_LRU = 0
COL_POOL = WIDTH
COL_SCONV = 2 * WIDTH
COL_QKV = 5 * WIDTH
COL_FORGET = 8 * WIDTH


def _params(semantics):
    return pltpu.CompilerParams(dimension_semantics=semantics, vmem_limit_bytes=VMEM_LIMIT)


def _dot(a, b):
    return jnp.dot(a, b, preferred_element_type=F32)


def _dot_nt(a, b):
    return lax.dot_general(a, b, (((1,), (1,)), ((), ())), preferred_element_type=F32)


def _shift_rows(x, k, fill):
    n, c = x.shape
    if k >= n:
        return jnp.full((n, c), fill, x.dtype)
    if k % SUBLANES == 0:
        return jnp.concatenate([jnp.full((k, c), fill, x.dtype), x[:n - k]], axis=0)
    r = pltpu.roll(x, k, 0)
    row = lax.broadcasted_iota(jnp.int32, (SUBLANES, c), 0)
    top = jnp.where(row < k, jnp.asarray(fill, x.dtype), r[:SUBLANES])
    return jnp.concatenate([top, r[SUBLANES:]], axis=0)


def _cumsum_rows(x):
    k = 1
    while k < x.shape[0]:
        x = x + _shift_rows(x, k, 0.0)
        k *= 2
    return x


def _linear_scan_rows(a, b):
    k = 1
    n = a.shape[0]
    while k < n:
        b = a * _shift_rows(b, k, 0.0) + b
        if 2 * k < n:
            a = a * _shift_rows(a, k, 1.0)
        k *= 2
    return b


def _softplus(x):
    return jnp.maximum(x, 0.0) + jnp.log1p(jnp.exp(-jnp.abs(x)))


def _wcast_kernel(w_ref, o_ref):
    o_ref[...] = w_ref[...].astype(o_ref.dtype)


def _cast_mix_weights(w_in_t, *, rows):
    depth, _, d = w_in_t.shape
    spec = pl.BlockSpec((1, rows, d), lambda l, j: (l, j, 0))
    return pl.pallas_call(
        _wcast_kernel,
        grid=(depth, N_MIX_PAD // rows),
        in_specs=[spec],
        out_specs=spec,
        out_shape=jax.ShapeDtypeStruct((depth, N_MIX_PAD, d), BF16),
        compiler_params=_params(("parallel", "parallel")),
        name="cast_mix_weights",
    )(w_in_t)


def _inproj_kernel(x_ref, g_ref, w_ref, proj_ref, xn_ref):
    x = x_ref[...]
    ms = jnp.mean(x * x, axis=-1, keepdims=True)
    xn = ((x * lax.rsqrt(ms + EPS)) * g_ref[...]).astype(xn_ref.dtype)
    xn_ref[...] = xn
    proj_ref[...] = _dot_nt(xn, w_ref[0])


def _inproj(x, g, w_mix, layer, *, tm):
    t, d = x.shape
    n = w_mix.shape[1]
    return pl.pallas_call(
        _inproj_kernel,
        grid=(t // tm,),
        in_specs=[
            pl.BlockSpec((tm, d), lambda i: (i, 0)),
            pl.BlockSpec((1, d), lambda i: (0, 0)),
            pl.BlockSpec((1, n, d), lambda i: (layer, 0, 0), pipeline_mode=pl.Buffered(1)),
        ],
        out_specs=[
            pl.BlockSpec((tm, n), lambda i: (i, 0)),
            pl.BlockSpec((tm, d), lambda i: (i, 0)),
        ],
        out_shape=[
            jax.ShapeDtypeStruct((t, n), F32),
            jax.ShapeDtypeStruct((t, d), BF16),
        ],
        compiler_params=_params(("parallel",)),
        name="inproj",
    )(x, g, w_mix)


def _lru_kernel(x_ref, cw_ref, cb_ref, wr_ref, br_ref, wi_ref, bi_ref, lam_ref, y_ref):
    x = x_ref[...]
    cw = cw_ref[...]
    u = cw[LRU_CONV - 1:LRU_CONV] * x + cb_ref[...]
    for k in range(1, LRU_CONV):
        u = u + cw[LRU_CONV - 1 - k:LRU_CONV - k] * _shift_rows(x, k, 0.0)
    ub = u.astype(BF16)
    r = jax.nn.sigmoid(_dot(ub, wr_ref[0]) + br_ref[...])
    gi = jax.nn.sigmoid(_dot(ub, wi_ref[0]) + bi_ref[...])
    log_a = (-LRU_C * r) * _softplus(-lam_ref[...])
    a = jnp.exp(log_a)
    inp = jnp.sqrt(-jnp.tanh(log_a) * (a * a + 1.0)) * (gi * u)
    y_ref[...] = _linear_scan_rows(a, inp).astype(y_ref.dtype)


def _lru(proj, cw, cb, wr_bd, br, wi_bd, bi, lam, *, batch, seq):
    nc = WIDTH // LANES
    col0 = COL_LRU // LANES
    vec = pl.BlockSpec((1, LANES), lambda b, c: (0, c))
    mat = pl.BlockSpec((1, LANES, LANES), lambda b, c: (c, 0, 0))
    return pl.pallas_call(
        _lru_kernel,
        grid=(batch, nc),
        in_specs=[
            pl.BlockSpec((seq, LANES), lambda b, c: (b, col0 + c)),
            pl.BlockSpec((LRU_CONV, LANES), lambda b, c: (0, c)),
            vec, mat, vec, mat, vec, vec,
        ],
        out_specs=pl.BlockSpec((seq, LANES), lambda b, c: (b, c)),
        out_shape=jax.ShapeDtypeStruct((batch * seq, WIDTH), BF16),
        compiler_params=_params(("parallel", "parallel")),
        name="rglru",
    )(proj, cw, cb, wr_bd, br, wi_bd, bi, lam)


def _pool_kernel(x_ref, w_ref, sc_ref, y_ref):
    g = pl.program_id(1)
    x = x_ref[...]
    w2 = x + _shift_rows(x, 1, 0.0)
    w4 = w2 + _shift_rows(w2, 2, 0.0)
    w8 = w4 + _shift_rows(w4, 4, 0.0)
    w16 = w8 + _shift_rows(w8, 8, 0.0)
    wsum = jnp.where(g == 0, w2, jnp.where(g == 1, w4, jnp.where(g == 2, w8, w16)))
    win = jnp.left_shift(2, g)
    t = lax.broadcasted_iota(jnp.int32, x.shape, 0)
    count = jnp.minimum(t + 1, win).astype(F32)
    pooled = wsum / count - x
    y_ref[...] = (_dot(pooled.astype(BF16), w_ref[0]) * sc_ref[...]).astype(y_ref.dtype)


def _pool(proj, w, scale, *, batch, seq):
    col0 = COL_POOL // LANES
    return pl.pallas_call(
        _pool_kernel,
        grid=(batch, POOL_GROUPS),
        in_specs=[
            pl.BlockSpec((seq, LANES), lambda b, g: (b, col0 + g)),
            pl.BlockSpec((1, LANES, LANES), lambda b, g: (g, 0, 0)),
            pl.BlockSpec((1, LANES), lambda b, g: (0, g)),
        ],
        out_specs=pl.BlockSpec((seq, LANES), lambda b, g: (b, g)),
        out_shape=jax.ShapeDtypeStruct((batch * seq, WIDTH), BF16),
        compiler_params=_params(("parallel", "parallel")),
        name="pool",
    )(proj, w, scale)


def _sconv_kernel(gb_ref, gc_ref, xc_ref, w_ref, y_ref):
    w = w_ref[...]
    z = gc_ref[...] * xc_ref[...]
    acc = w[SCONV_K - 1:SCONV_K] * z
    for k in range(1, SCONV_K):
        acc = acc + w[SCONV_K - 1 - k:SCONV_K - k] * _shift_rows(z, k, 0.0)
    y_ref[...] = (gb_ref[...] * acc).astype(y_ref.dtype)


def _sconv(proj, w, *, batch, seq):
    nc = WIDTH // LANES
    col0 = COL_SCONV // LANES

    def part(p):
        return pl.BlockSpec((seq, LANES), lambda b, c: (b, col0 + p * nc + c))

    return pl.pallas_call(
        _sconv_kernel,
        grid=(batch, nc),
        in_specs=[part(0), part(1), part(2),
                  pl.BlockSpec((SCONV_K, LANES), lambda b, c: (0, c))],
        out_specs=pl.BlockSpec((seq, LANES), lambda b, c: (b, c)),
        out_shape=jax.ShapeDtypeStruct((batch * seq, WIDTH), BF16),
        compiler_params=_params(("parallel", "parallel")),
        name="sconv",
    )(proj, proj, proj, w)


def _forget_kernel(f_ref, fb_ref, c_ref):
    z = f_ref[...] + fb_ref[...]
    log_f = jnp.minimum(z, 0.0) - jnp.log1p(jnp.exp(-jnp.abs(z)))
    c = _cumsum_rows(log_f)
    for h in range(HEADS):
        c_ref[0, h] = jnp.broadcast_to(c[:, h:h + 1], c.shape)


def _forget_cumsum(proj, fb, *, batch, seq):
    col0 = COL_FORGET // LANES
    return pl.pallas_call(
        _forget_kernel,
        grid=(batch,),
        in_specs=[
            pl.BlockSpec((seq, LANES), lambda b: (b, col0)),
            pl.BlockSpec((1, LANES), lambda b: (0, 0)),
        ],
        out_specs=pl.BlockSpec((1, HEADS, seq, LANES), lambda b: (b, 0, 0, 0)),
        out_shape=jax.ShapeDtypeStruct((batch, HEADS, seq, LANES), F32),
        compiler_params=_params(("parallel",)),
        name="forget_cumsum",
    )(proj, fb)


ATTN_TQ = 256
ATTN_ROWS = 64
ATTN_BIAS_LANES = 12


def _attn_kernel(q_ref, k_ref, v_ref, ca_ref, cb_ref, qg_ref, kg_ref, hm_ref,
                 wu_ref, wd_ref, o_ref, wub_ref, wdb_ref, q_scr, k_scr, vt_scr, s_scr, p_scr):
    wub_ref[...] = wu_ref[0].astype(BF16)
    wdb_ref[...] = wd_ref[0].astype(BF16)

    seq = q_ref.shape[0]
    hm = hm_ref[...]

    def head_norm(x, g):
        sq = x * x
        hi = sq.astype(BF16)
        lo = (sq - hi.astype(F32)).astype(BF16)
        ms = (_dot(hi, hm) + _dot(lo, hm)) * (1.0 / HEAD_DIM)
        return (x * lax.rsqrt(ms + EPS)) * g

    tq = ATTN_TQ
    nblk = seq // tq
    qg, kg = qg_ref[...], kg_ref[...]
    lane = lax.broadcasted_iota(jnp.int32, (1, LANES), 1)
    first = lane < HEAD_DIM

    def lane_in(lo, hi):
        return (lane >= lo) & (lane < hi)

    head_a_lane = lane_in(0, 3) | lane_in(6, 9)
    piece0 = (lane == 0) | (lane == 3) | (lane == 6) | (lane == 9)
    piece1 = (lane == 1) | (lane == 4) | (lane == 7) | (lane == 10)

    for j in range(nblk):
        rows = slice(j * tq, (j + 1) * tq)
        c = jnp.where(head_a_lane, ca_ref[0, 0, rows, :], cb_ref[0, 0, rows, :])
        hi = c.astype(BF16).astype(F32)
        mid = (c - hi).astype(BF16).astype(F32)
        lo = ((c - hi) - mid).astype(BF16).astype(F32)
        pieces = jnp.where(piece0, hi, jnp.where(piece1, mid, lo))
        k_extra = jnp.where(lane_in(0, 6), -pieces, jnp.where(lane_in(6, ATTN_BIAS_LANES), 1.0, 0.0))
        qa_extra = jnp.where(lane_in(0, 3), 1.0, jnp.where(lane_in(6, 9), pieces, 0.0))
        qb_extra = jnp.where(lane_in(3, 6), 1.0, jnp.where(lane_in(9, ATTN_BIAS_LANES), pieces, 0.0))
        qn = head_norm(q_ref[rows, :], qg) * (HEAD_DIM ** -0.5)
        q_scr[j, :tq, :LANES] = jnp.where(first, qn, 0.0).astype(BF16)
        q_scr[j, :tq, LANES:] = qa_extra.astype(BF16)
        q_scr[j, tq:, :LANES] = jnp.where(first, 0.0, qn).astype(BF16)
        q_scr[j, tq:, LANES:] = qb_extra.astype(BF16)
        k_scr[rows, :LANES] = head_norm(k_ref[rows, :], kg).astype(BF16)
        k_scr[rows, LANES:] = k_extra.astype(BF16)
        vt_scr[:, rows] = v_ref[rows, :].T.astype(BF16)

    rb = ATTN_ROWS
    nq = 2 * tq
    key_pos = lax.broadcasted_iota(jnp.int32, (rb, nq), 0)
    lane_q = lax.broadcasted_iota(jnp.int32, (rb, nq), 1)
    query_pos = jnp.where(lane_q < tq, lane_q, lane_q - tq)

    def fold8(x, op):
        out = x[:SUBLANES]
        for g in range(1, rb // SUBLANES):
            out = op(out, x[g * SUBLANES:(g + 1) * SUBLANES])
        return out

    for i in range(nblk):
        q0, kend = i * tq, (i + 1) * tq
        s_blk = s_scr.at[i % 2]
        s_blk[:kend, :] = _dot_nt(k_scr[:kend, :], q_scr[i])

        def chunk(c, s_blk=s_blk, q0=q0):
            s = s_blk[c * rb:(c + 1) * rb, :]
            if c * rb < q0:
                return s
            return jnp.where(key_pos + (c * rb - q0) <= query_pos, s, -jnp.inf)

        m8 = fold8(chunk(0), jnp.maximum)
        for c in range(1, kend // rb):
            m8 = jnp.maximum(m8, fold8(chunk(c), jnp.maximum))
        m = jnp.max(m8, axis=0, keepdims=True)

        l8 = jnp.zeros((SUBLANES, nq), F32)
        for c in range(kend // rb):
            p = jnp.exp(chunk(c) - m)
            p_scr[c * rb:(c + 1) * rb, :] = p.astype(BF16)
            l8 = l8 + fold8(p, jnp.add)
        l = jnp.sum(l8, axis=0, keepdims=True)

        o_t = _dot(vt_scr[:, :kend], p_scr[:kend, :]) / l
        o = o_t.T
        o_ref[q0:kend, :] = jnp.where(first, o[:tq], o[tq:]).astype(o_ref.dtype)


def _attention(proj, c, qg, kg, hm, wu, wd, layer, *, batch, seq):
    npair = HEADS // 2
    col0 = COL_QKV // LANES
    steps = batch * npair
    _, d, f = wu.shape
    ru, rd = d // steps, f // steps

    def part(p):
        return pl.BlockSpec((seq, LANES), lambda b, h: (b, col0 + p * npair + h))

    def c_spec(o):
        return pl.BlockSpec((1, 1, seq, LANES), lambda b, h: (b, 2 * h + o, 0, 0))

    vec = pl.BlockSpec((1, LANES), lambda b, h: (0, 0))
    return pl.pallas_call(
        _attn_kernel,
        grid=(batch, npair),
        in_specs=[part(0), part(1), part(2), c_spec(0), c_spec(1),
                  vec, vec, pl.BlockSpec((LANES, LANES), lambda b, h: (0, 0)),
                  pl.BlockSpec((1, ru, f), lambda b, h: (layer, b * npair + h, 0)),
                  pl.BlockSpec((1, rd, d), lambda b, h: (layer, b * npair + h, 0))],
        out_specs=[pl.BlockSpec((seq, LANES), lambda b, h: (b, h)),
                   pl.BlockSpec((ru, f), lambda b, h: (b * npair + h, 0)),
                   pl.BlockSpec((rd, d), lambda b, h: (b * npair + h, 0))],
        out_shape=[jax.ShapeDtypeStruct((batch * seq, WIDTH), BF16),
                   jax.ShapeDtypeStruct((d, f), BF16),
                   jax.ShapeDtypeStruct((f, d), BF16)],
        scratch_shapes=[pltpu.VMEM((seq // ATTN_TQ, 2 * ATTN_TQ, 2 * LANES), BF16),
                        pltpu.VMEM((seq, 2 * LANES), BF16),
                        pltpu.VMEM((LANES, seq), BF16),
                        pltpu.VMEM((2, seq, 2 * ATTN_TQ), F32),
                        pltpu.VMEM((seq, 2 * ATTN_TQ), BF16)],
        compiler_params=_params(("parallel", "parallel")),
        name="fox_attention",
    )(proj, proj, proj, c, c, qg, kg, hm, wu, wd)


def _merge_kernel(xn_ref, ya_ref, yb_ref, yc_ref, yd_ref, wg0_ref, wg1_ref, wg2_ref, wg3_ref, wb_ref, o_ref):
    xn = xn_ref[...]
    acc = None
    for k, (y_ref, wg_ref) in enumerate(((ya_ref, wg0_ref), (yb_ref, wg1_ref),
                                         (yc_ref, wg2_ref), (yd_ref, wg3_ref))):
        gate = jax.nn.sigmoid(_dot_nt(xn, wg_ref[...].astype(BF16)))
        term = gate * _dot(y_ref[...], wb_ref[0, k].astype(BF16))
        acc = term if acc is None else acc + term
    o_ref[...] = acc.astype(o_ref.dtype)


def _merge(xn, ys, w_in_t, wb, layer, *, tm, tn):
    t, d = xn.shape
    nblk = d // tn

    def wg_spec(k):
        return pl.BlockSpec((pl.Squeezed(), pl.Element(tn), pl.Element(d)),
                            lambda i, j: (layer, pl.multiple_of(N_MIX + k * d + j * tn, SUBLANES), 0))

    y_spec = pl.BlockSpec((tm, WIDTH), lambda i, j: (i, 0))
    return pl.pallas_call(
        _merge_kernel,
        grid=(t // tm, nblk),
        in_specs=[pl.BlockSpec((tm, d), lambda i, j: (i, 0)), y_spec, y_spec, y_spec, y_spec,
                  wg_spec(0), wg_spec(1), wg_spec(2), wg_spec(3),
                  pl.BlockSpec((1, N_BRANCH, WIDTH, tn), lambda i, j: (layer, 0, 0, j))],
        out_specs=pl.BlockSpec((tm, tn), lambda i, j: (i, j)),
        out_shape=jax.ShapeDtypeStruct((t, d), BF16),
        compiler_params=_params(("parallel", "arbitrary")),
        name="merge",
    )(xn, *ys, w_in_t, w_in_t, w_in_t, w_in_t, wb)


def _outproj_kernel(m_ref, x_ref, w_ref, g_ref, x1_ref, xn_ref, wb_scr):
    @pl.when(pl.program_id(0) == 0)
    def _():
        wb_scr[...] = w_ref[0].astype(BF16)

    x1 = x_ref[...] + _dot(m_ref[...], wb_scr[...])
    x1_ref[...] = x1
    ms = jnp.mean(x1 * x1, axis=-1, keepdims=True)
    xn_ref[...] = ((x1 * lax.rsqrt(ms + EPS)) * g_ref[...]).astype(xn_ref.dtype)


def _outproj(merged, x, w_out, g, layer, *, tm):
    t, d = x.shape
    row = pl.BlockSpec((tm, d), lambda i: (i, 0))
    return pl.pallas_call(
        _outproj_kernel,
        grid=(t // tm,),
        in_specs=[row, row,
                  pl.BlockSpec((1, d, d), lambda i: (layer, 0, 0), pipeline_mode=pl.Buffered(1)),
                  pl.BlockSpec((1, d), lambda i: (0, 0))],
        out_specs=[row, row],
        out_shape=[jax.ShapeDtypeStruct((t, d), F32), jax.ShapeDtypeStruct((t, d), BF16)],
        scratch_shapes=[pltpu.VMEM((d, d), BF16)],
        compiler_params=_params(("arbitrary",)),
        name="outproj",
    )(merged, x, w_out, g)


def _mlp_kernel(xn_ref, x_ref, wu_ref, wd_ref, o_ref):
    @pl.when(pl.program_id(1) == 0)
    def _():
        o_ref[...] = x_ref[...]

    h = jnp.maximum(_dot(xn_ref[...], wu_ref[...]), 0.0)
    o_ref[...] += _dot((h * h).astype(BF16), wd_ref[...])


def _mlp(xn, x, wu, wd, *, tm, tf):
    t, d = x.shape
    f = wu.shape[1]
    row = pl.BlockSpec((tm, d), lambda i, j: (i, 0))
    return pl.pallas_call(
        _mlp_kernel,
        grid=(t // tm, f // tf),
        in_specs=[row, row,
                  pl.BlockSpec((d, tf), lambda i, j: (0, j)),
                  pl.BlockSpec((tf, d), lambda i, j: (j, 0))],
        out_specs=row,
        out_shape=jax.ShapeDtypeStruct((t, d), F32),
        compiler_params=_params(("parallel", "arbitrary")),
        name="mlp",
    )(xn, x, wu, wd)


def _block_diag_pairs(w):
    n2, bd, _ = w.shape
    per = LANES // bd
    w4 = w.reshape(n2 // per, per, bd, bd)
    eye = jnp.eye(per, dtype=w.dtype)
    return jnp.einsum('jaik,ab->jaibk', w4, eye).reshape(n2 // per, LANES, LANES)


def _row(v):
    return v.reshape(1, -1).astype(F32)


def kernel(x, norm_mix_g, w_in, lru_conv_w, lru_conv_b, lru_wr, lru_br, lru_wi, lru_bi, lru_lambda, pool_w, pool_scale, sconv_w, q_norm_g, k_norm_g, forget_b, w_branch, w_out, norm_mlp_g, w_mlp_up, w_mlp_down):
    batch, seq, d = x.shape
    assert d == D_MODEL and seq % ATTN_TQ == 0
    depth = w_in.shape[0]
    t = batch * seq
    xf = x.reshape(t, d)

    lane = jnp.arange(LANES)
    same_head = (lane[:, None] // HEAD_DIM == lane[None, :] // HEAD_DIM).astype(BF16)

    w_in_t = jnp.swapaxes(w_in, 1, 2)
    w_mix = _cast_mix_weights(w_in_t, rows=N_MIX_PAD // 8)

    for l in range(depth):
        proj, xn = _inproj(xf, _row(norm_mix_g[l]), w_mix, l, tm=512)

        y_a = _lru(proj, lru_conv_w[l], _row(lru_conv_b[l]), _block_diag_pairs(lru_wr[l]).astype(BF16),
                   _row(lru_br[l]), _block_diag_pairs(lru_wi[l]).astype(BF16), _row(lru_bi[l]),
                   _row(lru_lambda[l]), batch=batch, seq=seq)
        y_b = _pool(proj, pool_w[l].astype(BF16), _row(pool_scale[l]), batch=batch, seq=seq)
        y_c = _sconv(proj, sconv_w[l], batch=batch, seq=seq)
        fb = jnp.pad(_row(forget_b[l]), ((0, 0), (0, LANES - HEADS)))
        c = _forget_cumsum(proj, fb, batch=batch, seq=seq)
        qg = jnp.tile(_row(q_norm_g[l]), (1, LANES // HEAD_DIM))
        kg = jnp.tile(_row(k_norm_g[l]), (1, LANES // HEAD_DIM))
        y_d, wu_bf, wd_bf = _attention(proj, c, qg, kg, same_head, w_mlp_up, w_mlp_down, l,
                                       batch=batch, seq=seq)

        merged = _merge(xn, (y_a, y_b, y_c, y_d), w_in_t, w_branch, l, tm=1024, tn=256)
        x1, xn2 = _outproj(merged, xf, w_out, _row(norm_mlp_g[l]), l, tm=512)
        xf = _mlp(xn2, x1, wu_bf, wd_bf, tm=512, tf=2048)

    return xf.reshape(batch, seq, d)
```

```python
import functools

import jax
import jax.numpy as jnp
from jax import lax
from jax.experimental import pallas as pl
from jax.experimental.pallas import tpu as pltpu

F32 = jnp.float32
BF16 = jnp.bfloat16

D_MODEL = 2048
WIDTH = 512
N_BRANCH = 4
LRU_BLOCK_DIM = 64
LRU_CONV = 4
LRU_C = 8.0
POOL_GROUPS = 4
SCONV_K = 3
HEADS = 8
HEAD_DIM = 64
D_FF = 4 * D_MODEL
EPS = 1e-6

LANES = 128
SUBLANES = 8
VMEM_LIMIT = 56 * 1024 * 1024

N_MIX = 2 * WIDTH + 3 * WIDTH + 3 * WIDTH + HEADS
N_MIX_PAD = -(-N_MIX // LANES) * LANES
COL_LRU = 0
COL_POOL = WIDTH
COL_SCONV = 2 * WIDTH
COL_QKV = 5 * WIDTH
COL_FORGET = 8 * WIDTH


def _params(semantics):
    return pltpu.CompilerParams(dimension_semantics=semantics, vmem_limit_bytes=VMEM_LIMIT)


def _dot(a, b):
    return jnp.dot(a, b, preferred_element_type=F32)


def _dot_nt(a, b):
    return lax.dot_general(a, b, (((1,), (1,)), ((), ())), preferred_element_type=F32)


def _shift_rows(x, k, fill):
    n, c = x.shape
    if k >= n:
        return jnp.full((n, c), fill, x.dtype)
    if k % SUBLANES == 0:
        return jnp.concatenate([jnp.full((k, c), fill, x.dtype), x[:n - k]], axis=0)
    r = pltpu.roll(x, k, 0)
    row = lax.broadcasted_iota(jnp.int32, (SUBLANES, c), 0)
    top = jnp.where(row < k, jnp.asarray(fill, x.dtype), r[:SUBLANES])
    return jnp.concatenate([top, r[SUBLANES:]], axis=0)


def _cumsum_rows(x):
    k = 1
    while k < x.shape[0]:
        x = x + _shift_rows(x, k, 0.0)
        k *= 2
    return x


def _linear_scan_rows(a, b):
    k = 1
    n = a.shape[0]
    while k < n:
        b = a * _shift_rows(b, k, 0.0) + b
        if 2 * k < n:
            a = a * _shift_rows(a, k, 1.0)
        k *= 2
    return b


def _softplus(x):
    return jnp.maximum(x, 0.0) + jnp.log1p(jnp.exp(-jnp.abs(x)))


def _wcast_kernel(w_ref, o_ref):
    o_ref[...] = w_ref[...].astype(o_ref.dtype)


def _cast_mix_weights(w_in_t, *, rows):
    depth, _, d = w_in_t.shape
    spec = pl.BlockSpec((1, rows, d), lambda l, j: (l, j, 0))
    return pl.pallas_call(
        _wcast_kernel,
        grid=(depth, N_MIX_PAD // rows),
        in_specs=[spec],
        out_specs=spec,
        out_shape=jax.ShapeDtypeStruct((depth, N_MIX_PAD, d), BF16),
        compiler_params=_params(("parallel", "parallel")),
        name="cast_mix_weights",
    )(w_in_t)


def _inproj_kernel(x_ref, g_ref, w_ref, proj_ref, xn_ref):
    x = x_ref[...]
    ms = jnp.mean(x * x, axis=-1, keepdims=True)
    xn = ((x * lax.rsqrt(ms + EPS)) * g_ref[...]).astype(xn_ref.dtype)
    xn_ref[...] = xn
    proj_ref[...] = _dot_nt(xn, w_ref[0])


def _inproj(x, g, w_mix, layer, *, tm):
    t, d = x.shape
    n = w_mix.shape[1]
    return pl.pallas_call(
        _inproj_kernel,
        grid=(t // tm,),
        in_specs=[
            pl.BlockSpec((tm, d), lambda i: (i, 0)),
            pl.BlockSpec((1, d), lambda i: (0, 0)),
            pl.BlockSpec((1, n, d), lambda i: (layer, 0, 0), pipeline_mode=pl.Buffered(1)),
        ],
        out_specs=[
            pl.BlockSpec((tm, n), lambda i: (i, 0)),
            pl.BlockSpec((tm, d), lambda i: (i, 0)),
        ],
        out_shape=[
            jax.ShapeDtypeStruct((t, n), F32),
            jax.ShapeDtypeStruct((t, d), BF16),
        ],
        compiler_params=_params(("parallel",)),
        name="inproj",
    )(x, g, w_mix)


def _lru_slab(x, cw, cb, wr, br, wi, bi, lam):
    u = cw[LRU_CONV - 1:LRU_CONV] * x + cb
    for k in range(1, LRU_CONV):
        u = u + cw[LRU_CONV - 1 - k:LRU_CONV - k] * _shift_rows(x, k, 0.0)
    ub = u.astype(BF16)
    r = jax.nn.sigmoid(_dot(ub, wr) + br)
    gi = jax.nn.sigmoid(_dot(ub, wi) + bi)
    log_a = (-LRU_C * r) * _softplus(-lam)
    a = jnp.exp(log_a)
    inp = jnp.sqrt(-jnp.tanh(log_a) * (a * a + 1.0)) * (gi * u)
    return _linear_scan_rows(a, inp)


def _pool_slab(x, group, w, scale):
    w2 = x + _shift_rows(x, 1, 0.0)
    w4 = w2 + _shift_rows(w2, 2, 0.0)
    w8 = w4 + _shift_rows(w4, 4, 0.0)
    w16 = w8 + _shift_rows(w8, 8, 0.0)
    wsum = jnp.where(group == 0, w2, jnp.where(group == 1, w4, jnp.where(group == 2, w8, w16)))
    win = jnp.left_shift(2, group)
    t = lax.broadcasted_iota(jnp.int32, x.shape, 0)
    count = jnp.minimum(t + 1, win).astype(F32)
    pooled = wsum / count - x
    return _dot(pooled.astype(BF16), w) * scale


def _sconv_slab(gate_b, gate_c, xc, w):
    z = gate_c * xc
    acc = w[SCONV_K - 1:SCONV_K] * z
    for k in range(1, SCONV_K):
        acc = acc + w[SCONV_K - 1 - k:SCONV_K - k] * _shift_rows(z, k, 0.0)
    return gate_b * acc


def _mixers_kernel(xa_ref, xp_ref, gb_ref, gc_ref, xc_ref, f_ref,
                   cw_ref, cb_ref, wr_ref, br_ref, wi_ref, bi_ref, lam_ref,
                   pw_ref, ps_ref, sw_ref, fb_ref,
                   ya_ref, yb_ref, yc_ref, c_ref):
    slab = pl.program_id(1)
    ya_ref[...] = _lru_slab(xa_ref[...], cw_ref[...], cb_ref[...], wr_ref[0], br_ref[...],
                            wi_ref[0], bi_ref[...], lam_ref[...]).astype(ya_ref.dtype)
    yb_ref[...] = _pool_slab(xp_ref[...], slab, pw_ref[0], ps_ref[...]).astype(yb_ref.dtype)
    yc_ref[...] = _sconv_slab(gb_ref[...], gc_ref[...], xc_ref[...], sw_ref[...]).astype(yc_ref.dtype)

    @pl.when(slab == 0)
    def _():
        z = f_ref[...] + fb_ref[...]
        log_f = jnp.minimum(z, 0.0) - jnp.log1p(jnp.exp(-jnp.abs(z)))
        head_lane = lax.broadcasted_iota(jnp.int32, (1, LANES), 1) < HEADS
        c_ref[...] = jnp.where(head_lane, _cumsum_rows(log_f), 0.0)


def _mixers(proj, cw, cb, wr_bd, br, wi_bd, bi, lam, pool_w, pool_scale, sconv_w, fb, *, batch, seq):
    nc = WIDTH // LANES

    def col(c0, p=0):
        return pl.BlockSpec((seq, LANES), lambda b, c: (b, (c0 + p * WIDTH) // LANES + c))

    vec = pl.BlockSpec((1, LANES), lambda b, c: (0, c))
    mat = pl.BlockSpec((1, LANES, LANES), lambda b, c: (c, 0, 0))
    y_spec = pl.BlockSpec((seq, LANES), lambda b, c: (b, c))
    y_shape = jax.ShapeDtypeStruct((batch * seq, WIDTH), BF16)
    return pl.pallas_call(
        _mixers_kernel,
        grid=(batch, nc),
        in_specs=[col(COL_LRU), col(COL_POOL), col(COL_SCONV, 0), col(COL_SCONV, 1), col(COL_SCONV, 2),
                  pl.BlockSpec((seq, LANES), lambda b, c: (b, COL_FORGET // LANES)),
                  pl.BlockSpec((LRU_CONV, LANES), lambda b, c: (0, c)), vec, mat, vec, mat, vec, vec,
                  mat, vec,
                  pl.BlockSpec((SCONV_K, LANES), lambda b, c: (0, c)),
                  pl.BlockSpec((1, LANES), lambda b, c: (0, 0))],
        out_specs=[y_spec, y_spec, y_spec, pl.BlockSpec((seq, LANES), lambda b, c: (b, 0))],
        out_shape=[y_shape, y_shape, y_shape, jax.ShapeDtypeStruct((batch * seq, LANES), F32)],
        compiler_params=_params(("parallel", "arbitrary")),
        name="mixers",
    )(proj, proj, proj, proj, proj, proj, cw, cb, wr_bd, br, wi_bd, bi, lam, pool_w, pool_scale, sconv_w, fb)


ATTN_TQ = 256
ATTN_ROWS = 64
ATTN_BIAS_LANES = 12


def _attn_kernel(q_ref, k_ref, v_ref, c_ref, sel_ref, qg_ref, kg_ref, hm_ref,
                 wu_ref, wd_ref, o_ref, wub_ref, wdb_ref, q_scr, k_scr, vt_scr, s_scr, p_scr):
    wub_ref[...] = wu_ref[0].astype(BF16)
    wdb_ref[...] = wd_ref[0].astype(BF16)

    seq = q_ref.shape[0]
    hm = hm_ref[...]

    def head_norm(x, g):
        sq = x * x
        hi = sq.astype(BF16)
        lo = (sq - hi.astype(F32)).astype(BF16)
        ms = (_dot(hi, hm) + _dot(lo, hm)) * (1.0 / HEAD_DIM)
        return (x * lax.rsqrt(ms + EPS)) * g

    tq = ATTN_TQ
    nblk = seq // tq
    qg, kg = qg_ref[...], kg_ref[...]
    lane = lax.broadcasted_iota(jnp.int32, (1, LANES), 1)
    first = lane < HEAD_DIM

    def lane_in(lo, hi):
        return (lane >= lo) & (lane < hi)

    sel_hi, sel_mid, sel_lo = sel_ref[0, 0], sel_ref[0, 1], sel_ref[0, 2]

    for j in range(nblk):
        rows = slice(j * tq, (j + 1) * tq)
        c = c_ref[rows, :]
        hi = c.astype(BF16)
        mid = (c - hi.astype(F32)).astype(BF16)
        lo = ((c - hi.astype(F32)) - mid.astype(F32)).astype(BF16)
        pieces = _dot(hi, sel_hi) + _dot(mid, sel_mid) + _dot(lo, sel_lo)
        k_extra = jnp.where(lane_in(0, 6), -pieces, jnp.where(lane_in(6, ATTN_BIAS_LANES), 1.0, 0.0))
        qa_extra = jnp.where(lane_in(0, 3), 1.0, jnp.where(lane_in(6, 9), pieces, 0.0))
        qb_extra = jnp.where(lane_in(3, 6), 1.0, jnp.where(lane_in(9, ATTN_BIAS_LANES), pieces, 0.0))
        qn = head_norm(q_ref[rows, :], qg) * (HEAD_DIM ** -0.5)
        q_scr[j, :tq, :LANES] = jnp.where(first, qn, 0.0).astype(BF16)
        q_scr[j, :tq, LANES:] = qa_extra.astype(BF16)
        q_scr[j, tq:, :LANES] = jnp.where(first, 0.0, qn).astype(BF16)
        q_scr[j, tq:, LANES:] = qb_extra.astype(BF16)
        k_scr[rows, :LANES] = head_norm(k_ref[rows, :], kg).astype(BF16)
        k_scr[rows, LANES:] = k_extra.astype(BF16)
        vt_scr[:, rows] = v_ref[rows, :].T.astype(BF16)

    rb = ATTN_ROWS
    nq = 2 * tq
    key_pos = lax.broadcasted_iota(jnp.int32, (tq, nq), 0)
    lane_q = lax.broadcasted_iota(jnp.int32, (tq, nq), 1)
    causal = key_pos <= jnp.where(lane_q < tq, lane_q, lane_q - tq)

    def fold8(x, op):
        out = x[:SUBLANES]
        for g in range(1, x.shape[0] // SUBLANES):
            out = op(out, x[g * SUBLANES:(g + 1) * SUBLANES])
        return out

    for i in range(nblk):
        q0, kend = i * tq, (i + 1) * tq
        s_blk = s_scr.at[i % 2]
        p_blk = p_scr.at[i % 2]

        m8 = None
        for kb in range(i + 1):
            rows = slice(kb * tq, (kb + 1) * tq)
            s = _dot_nt(k_scr[rows, :], q_scr[i])
            if kb == i:
                s = jnp.where(causal, s, -jnp.inf)
            s_blk[rows, :] = s
            part = fold8(s, jnp.maximum)
            m8 = part if m8 is None else jnp.maximum(m8, part)
        m = jnp.max(m8, axis=0, keepdims=True)

        l8 = jnp.zeros((SUBLANES, nq), F32)
        for c in range(kend // rb):
            p = jnp.exp(s_blk[c * rb:(c + 1) * rb, :] - m)
            p_blk[c * rb:(c + 1) * rb, :] = p.astype(BF16)
            l8 = l8 + fold8(p, jnp.add)
        l = jnp.sum(l8, axis=0, keepdims=True)

        o_t = _dot(vt_scr[:, :kend], p_blk[:kend, :]) / l
        o = o_t.T
        o_ref[q0:kend, :] = jnp.where(first, o[:tq], o[tq:]).astype(o_ref.dtype)


def _bias_lane_selectors():
    pair = jnp.arange(HEADS // 2)[:, None, None, None]
    p = jnp.arange(3)[None, :, None, None]
    row = jnp.arange(LANES)[None, None, :, None]
    col = jnp.arange(LANES)[None, None, None, :]
    head_a = (row == 2 * pair) & ((col == p) | (col == 6 + p))
    head_b = (row == 2 * pair + 1) & ((col == 3 + p) | (col == 9 + p))
    return (head_a | head_b).astype(BF16)


def _attention(proj, c, qg, kg, hm, wu, wd, layer, *, batch, seq):
    npair = HEADS // 2
    col0 = COL_QKV // LANES
    steps = batch * npair
    _, d, f = wu.shape
    ru, rd = d // steps, f // steps

    def part(p):
        return pl.BlockSpec((seq, LANES), lambda b, h: (b, col0 + p * npair + h))

    vec = pl.BlockSpec((1, LANES), lambda b, h: (0, 0))
    return pl.pallas_call(
        _attn_kernel,
        grid=(batch, npair),
        in_specs=[part(0), part(1), part(2),
                  pl.BlockSpec((seq, LANES), lambda b, h: (b, 0)),
                  pl.BlockSpec((1, 3, LANES, LANES), lambda b, h: (h, 0, 0, 0)),
                  vec, vec, pl.BlockSpec((LANES, LANES), lambda b, h: (0, 0)),
                  pl.BlockSpec((1, ru, f), lambda b, h: (layer, b * npair + h, 0)),
                  pl.BlockSpec((1, rd, d), lambda b, h: (layer, b * npair + h, 0))],
        out_specs=[pl.BlockSpec((seq, LANES), lambda b, h: (b, h)),
                   pl.BlockSpec((ru, f), lambda b, h: (b * npair + h, 0)),
                   pl.BlockSpec((rd, d), lambda b, h: (b * npair + h, 0))],
        out_shape=[jax.ShapeDtypeStruct((batch * seq, WIDTH), BF16),
                   jax.ShapeDtypeStruct((d, f), BF16),
                   jax.ShapeDtypeStruct((f, d), BF16)],
        scratch_shapes=[pltpu.VMEM((seq // ATTN_TQ, 2 * ATTN_TQ, 2 * LANES), BF16),
                        pltpu.VMEM((seq, 2 * LANES), BF16),
                        pltpu.VMEM((LANES, seq), BF16),
                        pltpu.VMEM((2, seq, 2 * ATTN_TQ), F32),
                        pltpu.VMEM((2, seq, 2 * ATTN_TQ), BF16)],
        compiler_params=_params(("parallel", "parallel")),
        name="fox_attention",
    )(proj, proj, proj, c, _bias_lane_selectors(), qg, kg, hm, wu, wd)


def _merge_kernel(xn_ref, ya_ref, yb_ref, yc_ref, yd_ref, wg0_ref, wg1_ref, wg2_ref, wg3_ref, wb_ref, o_ref):
    xn = xn_ref[...]
    acc = None
    for k, (y_ref, wg_ref) in enumerate(((ya_ref, wg0_ref), (yb_ref, wg1_ref),
                                         (yc_ref, wg2_ref), (yd_ref, wg3_ref))):
        gate = jax.nn.sigmoid(_dot_nt(xn, wg_ref[...].astype(BF16)))
        term = gate * _dot(y_ref[...], wb_ref[0, k].astype(BF16))
        acc = term if acc is None else acc + term
    o_ref[...] = acc.astype(o_ref.dtype)


def _merge(xn, ys, w_in_t, wb, layer, *, tm, tn):
    t, d = xn.shape
    nblk = d // tn

    def wg_spec(k):
        return pl.BlockSpec((pl.Squeezed(), pl.Element(tn), pl.Element(d)),
                            lambda i, j: (layer, pl.multiple_of(N_MIX + k * d + j * tn, SUBLANES), 0))

    y_spec = pl.BlockSpec((tm, WIDTH), lambda i, j: (i, 0))
    return pl.pallas_call(
        _merge_kernel,
        grid=(t // tm, nblk),
        in_specs=[pl.BlockSpec((tm, d), lambda i, j: (i, 0)), y_spec, y_spec, y_spec, y_spec,
                  wg_spec(0), wg_spec(1), wg_spec(2), wg_spec(3),
                  pl.BlockSpec((1, N_BRANCH, WIDTH, tn), lambda i, j: (layer, 0, 0, j))],
        out_specs=pl.BlockSpec((tm, tn), lambda i, j: (i, j)),
        out_shape=jax.ShapeDtypeStruct((t, d), BF16),
        compiler_params=_params(("parallel", "arbitrary")),
        name="merge",
    )(xn, *ys, w_in_t, w_in_t, w_in_t, w_in_t, wb)


def _outproj_kernel(m_ref, x_ref, w_ref, g_ref, x1_ref, xn_ref, wb_scr):
    @pl.when(pl.program_id(0) == 0)
    def _():
        wb_scr[...] = w_ref[0].astype(BF16)

    x1 = x_ref[...] + _dot(m_ref[...], wb_scr[...])
    x1_ref[...] = x1
    ms = jnp.mean(x1 * x1, axis=-1, keepdims=True)
    xn_ref[...] = ((x1 * lax.rsqrt(ms + EPS)) * g_ref[...]).astype(xn_ref.dtype)


def _outproj(merged, x, w_out, g, layer, *, tm):
    t, d = x.shape
    row = pl.BlockSpec((tm, d), lambda i: (i, 0))
    return pl.pallas_call(
        _outproj_kernel,
        grid=(t // tm,),
        in_specs=[row, row,
                  pl.BlockSpec((1, d, d), lambda i: (layer, 0, 0), pipeline_mode=pl.Buffered(1)),
                  pl.BlockSpec((1, d), lambda i: (0, 0))],
        out_specs=[row, row],
        out_shape=[jax.ShapeDtypeStruct((t, d), F32), jax.ShapeDtypeStruct((t, d), BF16)],
        scratch_shapes=[pltpu.VMEM((d, d), BF16)],
        compiler_params=_params(("arbitrary",)),
        name="outproj",
    )(merged, x, w_out, g)


def _mlp_kernel(xn_ref, x_ref, wu_ref, wd_ref, o_ref):
    @pl.when(pl.program_id(1) == 0)
    def _():
        o_ref[...] = x_ref[...]

    h = jnp.maximum(_dot(xn_ref[...], wu_ref[...]), 0.0)
    o_ref[...] += _dot((h * h).astype(BF16), wd_ref[...])


def _mlp(xn, x, wu, wd, *, tm, tf):
    t, d = x.shape
    f = wu.shape[1]
    row = pl.BlockSpec((tm, d), lambda i, j: (i, 0))
    return pl.pallas_call(
        _mlp_kernel,
        grid=(t // tm, f // tf),
        in_specs=[row, row,
                  pl.BlockSpec((d, tf), lambda i, j: (0, j)),
                  pl.BlockSpec((tf, d), lambda i, j: (j, 0))],
        out_specs=row,
        out_shape=jax.ShapeDtypeStruct((t, d), F32),
        compiler_params=_params(("parallel", "arbitrary")),
        name="mlp",
    )(xn, x, wu, wd)


def _block_diag_pairs(w):
    n2, bd, _ = w.shape
    per = LANES // bd
    w4 = w.reshape(n2 // per, per, bd, bd)
    eye = jnp.eye(per, dtype=w.dtype)
    return jnp.einsum('jaik,ab->jaibk', w4, eye).reshape(n2 // per, LANES, LANES)


def _row(v):
    return v.reshape(1, -1).astype(F32)


def kernel(x, norm_mix_g, w_in, lru_conv_w, lru_conv_b, lru_wr, lru_br, lru_wi, lru_bi, lru_lambda, pool_w, pool_scale, sconv_w, q_norm_g, k_norm_g, forget_b, w_branch, w_out, norm_mlp_g, w_mlp_up, w_mlp_down):
    batch, seq, d = x.shape
    assert d == D_MODEL and seq % ATTN_TQ == 0
    depth = w_in.shape[0]
    t = batch * seq
    xf = x.reshape(t, d)

    lane = jnp.arange(LANES)
    same_head = (lane[:, None] // HEAD_DIM == lane[None, :] // HEAD_DIM).astype(BF16)

    w_in_t = jnp.swapaxes(w_in, 1, 2)
    w_mix = _cast_mix_weights(w_in_t, rows=N_MIX_PAD // 8)

    for l in range(depth):
        proj, xn = _inproj(xf, _row(norm_mix_g[l]), w_mix, l, tm=512)

        fb = jnp.pad(_row(forget_b[l]), ((0, 0), (0, LANES - HEADS)))
        y_a, y_b, y_c, c = _mixers(
            proj, lru_conv_w[l], _row(lru_conv_b[l]), _block_diag_pairs(lru_wr[l]).astype(BF16),
            _row(lru_br[l]), _block_diag_pairs(lru_wi[l]).astype(BF16), _row(lru_bi[l]),
            _row(lru_lambda[l]), pool_w[l].astype(BF16), _row(pool_scale[l]), sconv_w[l], fb,
            batch=batch, seq=seq)
        qg = jnp.tile(_row(q_norm_g[l]), (1, LANES // HEAD_DIM))
        kg = jnp.tile(_row(k_norm_g[l]), (1, LANES // HEAD_DIM))
        y_d, wu_bf, wd_bf = _attention(proj, c, qg, kg, same_head, w_mlp_up, w_mlp_down, l,
                                       batch=batch, seq=seq)

        merged = _merge(xn, (y_a, y_b, y_c, y_d), w_in_t, w_branch, l, tm=1024, tn=256)
        x1, xn2 = _outproj(merged, xf, w_out, _row(norm_mlp_g[l]), l, tm=512)
        xf = _mlp(xn2, x1, wu_bf, wd_bf, tm=512, tf=2048)

    return xf.reshape(batch, seq, d)
```

```python
import functools

import jax
import jax.numpy as jnp
from jax import lax
from jax.experimental import pallas as pl
from jax.experimental.pallas import tpu as pltpu

F32 = jnp.float32
BF16 = jnp.bfloat16

D_MODEL = 2048
WIDTH = 512
N_BRANCH = 4
LRU_BLOCK_DIM = 64
LRU_CONV = 4
LRU_C = 8.0
POOL_GROUPS = 4
SCONV_K = 3
HEADS = 8
HEAD_DIM = 64
D_FF = 4 * D_MODEL
EPS = 1e-6

LANES = 128
SUBLANES = 8
VMEM_LIMIT = 56 * 1024 * 1024

N_MIX = 2 * WIDTH + 3 * WIDTH + 3 * WIDTH + HEADS
N_MIX_PAD = -(-N_MIX // LANES) * LANES
COL_LRU = 0
COL_POOL = WIDTH
COL_SCONV = 2 * WIDTH
COL_QKV = 5 * WIDTH
COL_FORGET = 8 * WIDTH


def _params(semantics):
    return pltpu.CompilerParams(dimension_semantics=semantics, vmem_limit_bytes=VMEM_LIMIT)


def _dot(a, b):
    return jnp.dot(a, b, preferred_element_type=F32)


def _dot_nt(a, b):
    return lax.dot_general(a, b, (((1,), (1,)), ((), ())), preferred_element_type=F32)


def _shift_rows(x, k, fill):
    n, c = x.shape
    if k >= n:
        return jnp.full((n, c), fill, x.dtype)
    if k % SUBLANES == 0:
        return jnp.concatenate([jnp.full((k, c), fill, x.dtype), x[:n - k]], axis=0)
    r = pltpu.roll(x, k, 0)
    row = lax.broadcasted_iota(jnp.int32, (SUBLANES, c), 0)
    top = jnp.where(row < k, jnp.asarray(fill, x.dtype), r[:SUBLANES])
    return jnp.concatenate([top, r[SUBLANES:]], axis=0)


def _cumsum_rows(x):
    k = 1
    while k < x.shape[0]:
        x = x + _shift_rows(x, k, 0.0)
        k *= 2
    return x


def _linear_scan_rows(a, b):
    k = 1
    n = a.shape[0]
    while k < n:
        b = a * _shift_rows(b, k, 0.0) + b
        if 2 * k < n:
            a = a * _shift_rows(a, k, 1.0)
        k *= 2
    return b


def _softplus(x):
    return jnp.maximum(x, 0.0) + jnp.log1p(jnp.exp(-jnp.abs(x)))


def _wcast_kernel(w_ref, o_ref):
    o_ref[...] = w_ref[...].astype(o_ref.dtype)


def _cast_mix_weights(w_in_t, *, rows):
    depth, _, d = w_in_t.shape
    spec = pl.BlockSpec((1, rows, d), lambda l, j: (l, j, 0))
    return pl.pallas_call(
        _wcast_kernel,
        grid=(depth, N_MIX_PAD // rows),
        in_specs=[spec],
        out_specs=spec,
        out_shape=jax.ShapeDtypeStruct((depth, N_MIX_PAD, d), BF16),
        compiler_params=_params(("parallel", "parallel")),
        name="cast_mix_weights",
    )(w_in_t)


def _inproj_kernel(x_ref, g_ref, w_ref, proj_ref, xn_ref):
    x = x_ref[...]
    ms = jnp.mean(x * x, axis=-1, keepdims=True)
    xn = ((x * lax.rsqrt(ms + EPS)) * g_ref[...]).astype(xn_ref.dtype)
    xn_ref[...] = xn
    proj_ref[...] = _dot_nt(xn, w_ref[0])


def _inproj(x, g, w_mix, layer, *, tm):
    t, d = x.shape
    n = w_mix.shape[1]
    return pl.pallas_call(
        _inproj_kernel,
        grid=(t // tm,),
        in_specs=[
            pl.BlockSpec((tm, d), lambda i: (i, 0)),
            pl.BlockSpec((1, d), lambda i: (0, 0)),
            pl.BlockSpec((1, n, d), lambda i: (layer, 0, 0), pipeline_mode=pl.Buffered(1)),
        ],
        out_specs=[
            pl.BlockSpec((tm, n), lambda i: (i, 0)),
            pl.BlockSpec((tm, d), lambda i: (i, 0)),
        ],
        out_shape=[
            jax.ShapeDtypeStruct((t, n), F32),
            jax.ShapeDtypeStruct((t, d), BF16),
        ],
        compiler_params=_params(("parallel",)),
        name="inproj",
    )(x, g, w_mix)


def _lru_slab(x, cw, cb, wr, br, wi, bi, lam):
    u = cw[LRU_CONV - 1:LRU_CONV] * x + cb
    for k in range(1, LRU_CONV):
        u = u + cw[LRU_CONV - 1 - k:LRU_CONV - k] * _shift_rows(x, k, 0.0)
    ub = u.astype(BF16)
    r = jax.nn.sigmoid(_dot(ub, wr) + br)
    gi = jax.nn.sigmoid(_dot(ub, wi) + bi)
    log_a = (-LRU_C * r) * _softplus(-lam)
    a = jnp.exp(log_a)
    inp = jnp.sqrt(-jnp.tanh(log_a) * (a * a + 1.0)) * (gi * u)
    return _linear_scan_rows(a, inp)


def _pool_slab(x, group, w, scale):
    w2 = x + _shift_rows(x, 1, 0.0)
    w4 = w2 + _shift_rows(w2, 2, 0.0)
    w8 = w4 + _shift_rows(w4, 4, 0.0)
    w16 = w8 + _shift_rows(w8, 8, 0.0)
    wsum = jnp.where(group == 0, w2, jnp.where(group == 1, w4, jnp.where(group == 2, w8, w16)))
    win = jnp.left_shift(2, group)
    t = lax.broadcasted_iota(jnp.int32, x.shape, 0)
    count = jnp.minimum(t + 1, win).astype(F32)
    pooled = wsum / count - x
    return _dot(pooled.astype(BF16), w) * scale


def _sconv_slab(gate_b, gate_c, xc, w):
    z = gate_c * xc
    acc = w[SCONV_K - 1:SCONV_K] * z
    for k in range(1, SCONV_K):
        acc = acc + w[SCONV_K - 1 - k:SCONV_K - k] * _shift_rows(z, k, 0.0)
    return gate_b * acc


def _mixers_kernel(xa_ref, xp_ref, gb_ref, gc_ref, xc_ref, f_ref,
                   cw_ref, cb_ref, wr_ref, br_ref, wi_ref, bi_ref, lam_ref,
                   pw_ref, ps_ref, sw_ref, fb_ref,
                   ya_ref, yb_ref, yc_ref, c_ref):
    slab = pl.program_id(1)
    ya_ref[...] = _lru_slab(xa_ref[...], cw_ref[...], cb_ref[...], wr_ref[0], br_ref[...],
                            wi_ref[0], bi_ref[...], lam_ref[...]).astype(ya_ref.dtype)
    yb_ref[...] = _pool_slab(xp_ref[...], slab, pw_ref[0], ps_ref[...]).astype(yb_ref.dtype)
    yc_ref[...] = _sconv_slab(gb_ref[...], gc_ref[...], xc_ref[...], sw_ref[...]).astype(yc_ref.dtype)

    @pl.when(slab == 0)
    def _():
        z = f_ref[...] + fb_ref[...]
        log_f = jnp.minimum(z, 0.0) - jnp.log1p(jnp.exp(-jnp.abs(z)))
        head_lane = lax.broadcasted_iota(jnp.int32, (1, LANES), 1) < HEADS
        c_ref[...] = jnp.where(head_lane, _cumsum_rows(log_f), 0.0)


def _mixers(proj, cw, cb, wr_bd, br, wi_bd, bi, lam, pool_w, pool_scale, sconv_w, fb, *, batch, seq):
    nc = WIDTH // LANES

    def col(c0, p=0):
        return pl.BlockSpec((seq, LANES), lambda b, c: (b, (c0 + p * WIDTH) // LANES + c))

    vec = pl.BlockSpec((1, LANES), lambda b, c: (0, c))
    mat = pl.BlockSpec((1, LANES, LANES), lambda b, c: (c, 0, 0))
    y_spec = pl.BlockSpec((seq, LANES), lambda b, c: (b, c))
    y_shape = jax.ShapeDtypeStruct((batch * seq, WIDTH), BF16)
    return pl.pallas_call(
        _mixers_kernel,
        grid=(batch, nc),
        in_specs=[col(COL_LRU), col(COL_POOL), col(COL_SCONV, 0), col(COL_SCONV, 1), col(COL_SCONV, 2),
                  pl.BlockSpec((seq, LANES), lambda b, c: (b, COL_FORGET // LANES)),
                  pl.BlockSpec((LRU_CONV, LANES), lambda b, c: (0, c)), vec, mat, vec, mat, vec, vec,
                  mat, vec,
                  pl.BlockSpec((SCONV_K, LANES), lambda b, c: (0, c)),
                  pl.BlockSpec((1, LANES), lambda b, c: (0, 0))],
        out_specs=[y_spec, y_spec, y_spec, pl.BlockSpec((seq, LANES), lambda b, c: (b, 0))],
        out_shape=[y_shape, y_shape, y_shape, jax.ShapeDtypeStruct((batch * seq, LANES), F32)],
        compiler_params=_params(("parallel", "arbitrary")),
        name="mixers",
    )(proj, proj, proj, proj, proj, proj, cw, cb, wr_bd, br, wi_bd, bi, lam, pool_w, pool_scale, sconv_w, fb)


ATTN_TQ = 512
LOG2E = 1.4426950408889634
ATTN_ROWS = 64
ATTN_BIAS_LANES = 12


def _attn_kernel(q_ref, k_ref, v_ref, c_ref, sel_ref, qg_ref, kg_ref, hm_ref,
                 wu_ref, wd_ref, o_ref, wub_ref, wdb_ref, q_scr, k_scr, vt_scr, s_scr, p_scr):
    wub_ref[...] = wu_ref[0].astype(BF16)
    wdb_ref[...] = wd_ref[0].astype(BF16)

    seq = q_ref.shape[0]
    hm = hm_ref[...]

    def head_norm(x, g):
        sq = x * x
        hi = sq.astype(BF16)
        lo = (sq - hi.astype(F32)).astype(BF16)
        ms = (_dot(hi, hm) + _dot(lo, hm)) * (1.0 / HEAD_DIM)
        return (x * lax.rsqrt(ms + EPS)) * g

    tq = ATTN_TQ
    nblk = seq // tq
    qg, kg = qg_ref[...], kg_ref[...]
    lane = lax.broadcasted_iota(jnp.int32, (1, LANES), 1)
    first = lane < HEAD_DIM

    def lane_in(lo, hi):
        return (lane >= lo) & (lane < hi)

    sel_hi, sel_mid, sel_lo = sel_ref[0, 0], sel_ref[0, 1], sel_ref[0, 2]

    for j in range(nblk):
        rows = slice(j * tq, (j + 1) * tq)
        c = c_ref[rows, :] * LOG2E
        hi = c.astype(BF16)
        mid = (c - hi.astype(F32)).astype(BF16)
        lo = ((c - hi.astype(F32)) - mid.astype(F32)).astype(BF16)
        pieces = _dot(hi, sel_hi) + _dot(mid, sel_mid) + _dot(lo, sel_lo)
        k_extra = jnp.where(lane_in(0, 6), -pieces, jnp.where(lane_in(6, ATTN_BIAS_LANES), 1.0, 0.0))
        qa_extra = jnp.where(lane_in(0, 3), 1.0, jnp.where(lane_in(6, 9), pieces, 0.0))
        qb_extra = jnp.where(lane_in(3, 6), 1.0, jnp.where(lane_in(9, ATTN_BIAS_LANES), pieces, 0.0))
        qn = head_norm(q_ref[rows, :], qg) * (HEAD_DIM ** -0.5 * LOG2E)
        q_scr[j, :tq, :LANES] = jnp.where(first, qn, 0.0).astype(BF16)
        q_scr[j, :tq, LANES:] = qa_extra.astype(BF16)
        q_scr[j, tq:, :LANES] = jnp.where(first, 0.0, qn).astype(BF16)
        q_scr[j, tq:, LANES:] = qb_extra.astype(BF16)
        k_scr[rows, :LANES] = head_norm(k_ref[rows, :], kg).astype(BF16)
        k_scr[rows, LANES:] = k_extra.astype(BF16)
        vt_scr[:, rows] = v_ref[rows, :].T.astype(BF16)

    rb = ATTN_ROWS
    nq = 2 * tq
    key_pos = lax.broadcasted_iota(jnp.int32, (tq, nq), 0)
    lane_q = lax.broadcasted_iota(jnp.int32, (tq, nq), 1)
    causal = key_pos <= jnp.where(lane_q < tq, lane_q, lane_q - tq)

    def fold8(x, op):
        out = x[:SUBLANES]
        for g in range(1, x.shape[0] // SUBLANES):
            out = op(out, x[g * SUBLANES:(g + 1) * SUBLANES])
        return out

    for i in range(nblk):
        q0, kend = i * tq, (i + 1) * tq
        s_blk = s_scr.at[i % 2]
        p_blk = p_scr.at[i % 2]

        m8 = None
        for kb in range(i + 1):
            rows = slice(kb * tq, (kb + 1) * tq)
            s = _dot_nt(k_scr[rows, :], q_scr[i])
            if kb == i:
                s = jnp.where(causal, s, -jnp.inf)
            s_blk[rows, :] = s
            part = fold8(s, jnp.maximum)
            m8 = part if m8 is None else jnp.maximum(m8, part)
        m = jnp.max(m8, axis=0, keepdims=True)

        l8 = jnp.zeros((SUBLANES, nq), F32)
        for c in range(kend // rb):
            p = jnp.exp2(s_blk[c * rb:(c + 1) * rb, :] - m)
            p_blk[c * rb:(c + 1) * rb, :] = p.astype(BF16)
            l8 = l8 + fold8(p, jnp.add)
        l = jnp.sum(l8, axis=0, keepdims=True)

        o_t = _dot(vt_scr[:, :kend], p_blk[:kend, :]) / l
        o = o_t.T
        o_ref[q0:kend, :] = jnp.where(first, o[:tq], o[tq:]).astype(o_ref.dtype)


def _bias_lane_selectors():
    pair = jnp.arange(HEADS // 2)[:, None, None, None]
    p = jnp.arange(3)[None, :, None, None]
    row = jnp.arange(LANES)[None, None, :, None]
    col = jnp.arange(LANES)[None, None, None, :]
    head_a = (row == 2 * pair) & ((col == p) | (col == 6 + p))
    head_b = (row == 2 * pair + 1) & ((col == 3 + p) | (col == 9 + p))
    return (head_a | head_b).astype(BF16)


def _attention(proj, c, qg, kg, hm, wu, wd, layer, *, batch, seq):
    npair = HEADS // 2
    col0 = COL_QKV // LANES
    steps = batch * npair
    _, d, f = wu.shape
    ru, rd = d // steps, f // steps

    def part(p):
        return pl.BlockSpec((seq, LANES), lambda b, h: (b, col0 + p * npair + h))

    vec = pl.BlockSpec((1, LANES), lambda b, h: (0, 0))
    return pl.pallas_call(
        _attn_kernel,
        grid=(batch, npair),
        in_specs=[part(0), part(1), part(2),
                  pl.BlockSpec((seq, LANES), lambda b, h: (b, 0)),
                  pl.BlockSpec((1, 3, LANES, LANES), lambda b, h: (h, 0, 0, 0)),
                  vec, vec, pl.BlockSpec((LANES, LANES), lambda b, h: (0, 0)),
                  pl.BlockSpec((1, ru, f), lambda b, h: (layer, b * npair + h, 0)),
                  pl.BlockSpec((1, rd, d), lambda b, h: (layer, b * npair + h, 0))],
        out_specs=[pl.BlockSpec((seq, LANES), lambda b, h: (b, h)),
                   pl.BlockSpec((ru, f), lambda b, h: (b * npair + h, 0)),
                   pl.BlockSpec((rd, d), lambda b, h: (b * npair + h, 0))],
        out_shape=[jax.ShapeDtypeStruct((batch * seq, WIDTH), BF16),
                   jax.ShapeDtypeStruct((d, f), BF16),
                   jax.ShapeDtypeStruct((f, d), BF16)],
        scratch_shapes=[pltpu.VMEM((seq // ATTN_TQ, 2 * ATTN_TQ, 2 * LANES), BF16),
                        pltpu.VMEM((seq, 2 * LANES), BF16),
                        pltpu.VMEM((LANES, seq), BF16),
                        pltpu.VMEM((2, seq, 2 * ATTN_TQ), F32),
                        pltpu.VMEM((2, seq, 2 * ATTN_TQ), BF16)],
        compiler_params=_params(("parallel", "parallel")),
        name="fox_attention",
    )(proj, proj, proj, c, _bias_lane_selectors(), qg, kg, hm, wu, wd)


def _merge_kernel(xn_ref, ya_ref, yb_ref, yc_ref, yd_ref, wg0_ref, wg1_ref, wg2_ref, wg3_ref, wb_ref, o_ref):
    xn = xn_ref[...]
    acc = None
    for k, (y_ref, wg_ref) in enumerate(((ya_ref, wg0_ref), (yb_ref, wg1_ref),
                                         (yc_ref, wg2_ref), (yd_ref, wg3_ref))):
        gate = jax.nn.sigmoid(_dot_nt(xn, wg_ref[...].astype(BF16)))
        term = gate * _dot(y_ref[...], wb_ref[0, k].astype(BF16))
        acc = term if acc is None else acc + term
    o_ref[...] = acc.astype(o_ref.dtype)


def _merge(xn, ys, w_in_t, wb, layer, *, tm, tn):
    t, d = xn.shape
    nblk = d // tn

    def wg_spec(k):
        return pl.BlockSpec((pl.Squeezed(), pl.Element(tn), pl.Element(d)),
                            lambda i, j: (layer, pl.multiple_of(N_MIX + k * d + j * tn, SUBLANES), 0))

    y_spec = pl.BlockSpec((tm, WIDTH), lambda i, j: (i, 0))
    return pl.pallas_call(
        _merge_kernel,
        grid=(t // tm, nblk),
        in_specs=[pl.BlockSpec((tm, d), lambda i, j: (i, 0)), y_spec, y_spec, y_spec, y_spec,
                  wg_spec(0), wg_spec(1), wg_spec(2), wg_spec(3),
                  pl.BlockSpec((1, N_BRANCH, WIDTH, tn), lambda i, j: (layer, 0, 0, j))],
        out_specs=pl.BlockSpec((tm, tn), lambda i, j: (i, j)),
        out_shape=jax.ShapeDtypeStruct((t, d), BF16),
        compiler_params=_params(("parallel", "arbitrary")),
        name="merge",
    )(xn, *ys, w_in_t, w_in_t, w_in_t, w_in_t, wb)


def _outproj_kernel(m_ref, x_ref, w_ref, g_ref, x1_ref, xn_ref, wb_scr):
    @pl.when(pl.program_id(0) == 0)
    def _():
        wb_scr[...] = w_ref[0].astype(BF16)

    x1 = x_ref[...] + _dot(m_ref[...], wb_scr[...])
    x1_ref[...] = x1
    ms = jnp.mean(x1 * x1, axis=-1, keepdims=True)
    xn_ref[...] = ((x1 * lax.rsqrt(ms + EPS)) * g_ref[...]).astype(xn_ref.dtype)


def _outproj(merged, x, w_out, g, layer, *, tm):
    t, d = x.shape
    row = pl.BlockSpec((tm, d), lambda i: (i, 0))
    return pl.pallas_call(
        _outproj_kernel,
        grid=(t // tm,),
        in_specs=[row, row,
                  pl.BlockSpec((1, d, d), lambda i: (layer, 0, 0), pipeline_mode=pl.Buffered(1)),
                  pl.BlockSpec((1, d), lambda i: (0, 0))],
        out_specs=[row, row],
        out_shape=[jax.ShapeDtypeStruct((t, d), F32), jax.ShapeDtypeStruct((t, d), BF16)],
        scratch_shapes=[pltpu.VMEM((d, d), BF16)],
        compiler_params=_params(("arbitrary",)),
        name="outproj",
    )(merged, x, w_out, g)


def _mlp_kernel(xn_ref, x_ref, wu_ref, wd_ref, o_ref):
    @pl.when(pl.program_id(1) == 0)
    def _():
        o_ref[...] = x_ref[...]

    h = jnp.maximum(_dot(xn_ref[...], wu_ref[...]), 0.0)
    o_ref[...] += _dot((h * h).astype(BF16), wd_ref[...])


def _mlp(xn, x, wu, wd, *, tm, tf):
    t, d = x.shape
    f = wu.shape[1]
    row = pl.BlockSpec((tm, d), lambda i, j: (i, 0))
    return pl.pallas_call(
        _mlp_kernel,
        grid=(t // tm, f // tf),
        in_specs=[row, row,
                  pl.BlockSpec((d, tf), lambda i, j: (0, j)),
                  pl.BlockSpec((tf, d), lambda i, j: (j, 0))],
        out_specs=row,
        out_shape=jax.ShapeDtypeStruct((t, d), F32),
        compiler_params=_params(("parallel", "arbitrary")),
        name="mlp",
    )(xn, x, wu, wd)


def _block_diag_pairs(w):
    n2, bd, _ = w.shape
    per = LANES // bd
    w4 = w.reshape(n2 // per, per, bd, bd)
    eye = jnp.eye(per, dtype=w.dtype)
    return jnp.einsum('jaik,ab->jaibk', w4, eye).reshape(n2 // per, LANES, LANES)


def _row(v):
    return v.reshape(1, -1).astype(F32)


def kernel(x, norm_mix_g, w_in, lru_conv_w, lru_conv_b, lru_wr, lru_br, lru_wi, lru_bi, lru_lambda, pool_w, pool_scale, sconv_w, q_norm_g, k_norm_g, forget_b, w_branch, w_out, norm_mlp_g, w_mlp_up, w_mlp_down):
    batch, seq, d = x.shape
    assert d == D_MODEL and seq % ATTN_TQ == 0
    depth = w_in.shape[0]
    t = batch * seq
    xf = x.reshape(t, d)

    lane = jnp.arange(LANES)
    same_head = (lane[:, None] // HEAD_DIM == lane[None, :] // HEAD_DIM).astype(BF16)

    w_in_t = jnp.swapaxes(w_in, 1, 2)
    w_mix = _cast_mix_weights(w_in_t, rows=N_MIX_PAD // 8)

    for l in range(depth):
        proj, xn = _inproj(xf, _row(norm_mix_g[l]), w_mix, l, tm=512)

        fb = jnp.pad(_row(forget_b[l]), ((0, 0), (0, LANES - HEADS)))
        y_a, y_b, y_c, c = _mixers(
            proj, lru_conv_w[l], _row(lru_conv_b[l]), _block_diag_pairs(lru_wr[l]).astype(BF16),
            _row(lru_br[l]), _block_diag_pairs(lru_wi[l]).astype(BF16), _row(lru_bi[l]),
            _row(lru_lambda[l]), pool_w[l].astype(BF16), _row(pool_scale[l]), sconv_w[l], fb,
            batch=batch, seq=seq)
        qg = jnp.tile(_row(q_norm_g[l]), (1, LANES // HEAD_DIM))
        kg = jnp.tile(_row(k_norm_g[l]), (1, LANES // HEAD_DIM))
        y_d, wu_bf, wd_bf = _attention(proj, c, qg, kg, same_head, w_mlp_up, w_mlp_down, l,
                                       batch=batch, seq=seq)

        merged = _merge(xn, (y_a, y_b, y_c, y_d), w_in_t, w_branch, l, tm=1024, tn=256)
        x1, xn2 = _outproj(merged, xf, w_out, _row(norm_mlp_g[l]), l, tm=512)
        xf = _mlp(xn2, x1, wu_bf, wd_bf, tm=512, tf=2048)

    return xf.reshape(batch, seq, d)
```

```python
import functools

import jax
import jax.numpy as jnp
from jax import lax
from jax.experimental import pallas as pl
from jax.experimental.pallas import tpu as pltpu

F32 = jnp.float32
BF16 = jnp.bfloat16

D_MODEL = 2048
WIDTH = 512
N_BRANCH = 4
LRU_BLOCK_DIM = 64
LRU_CONV = 4
LRU_C = 8.0
POOL_GROUPS = 4
SCONV_K = 3
HEADS = 8
HEAD_DIM = 64
D_FF = 4 * D_MODEL
EPS = 1e-6

LANES = 128
SUBLANES = 8
VMEM_LIMIT = 56 * 1024 * 1024

N_MIX = 2 * WIDTH + 3 * WIDTH + 3 * WIDTH + HEADS
N_MIX_PAD = -(-N_MIX // LANES) * LANES
COL_LRU = 0
COL_POOL = WIDTH
COL_SCONV = 2 * WIDTH
COL_QKV = 5 * WIDTH
COL_FORGET = 8 * WIDTH


def _params(semantics):
    return pltpu.CompilerParams(dimension_semantics=semantics, vmem_limit_bytes=VMEM_LIMIT)


def _dot(a, b):
    return jnp.dot(a, b, preferred_element_type=F32)


def _dot_nt(a, b):
    return lax.dot_general(a, b, (((1,), (1,)), ((), ())), preferred_element_type=F32)


def _shift_rows(x, k, fill):
    n, c = x.shape
    if k >= n:
        return jnp.full((n, c), fill, x.dtype)
    if k % SUBLANES == 0:
        return jnp.concatenate([jnp.full((k, c), fill, x.dtype), x[:n - k]], axis=0)
    r = pltpu.roll(x, k, 0)
    row = lax.broadcasted_iota(jnp.int32, (SUBLANES, c), 0)
    top = jnp.where(row < k, jnp.asarray(fill, x.dtype), r[:SUBLANES])
    return jnp.concatenate([top, r[SUBLANES:]], axis=0)


SCAN_PAD = 4


def _scan_scratch_rows(seq):
    return seq + SUBLANES * SCAN_PAD


def _chunk_rows(tau, stride):
    return pl.ds(tau, SUBLANES, stride=stride)


def _linear_scan_rows(a, b, a_scr, b_scr):
    n, c = a.shape
    length = n // SUBLANES
    stride = length + SCAN_PAD
    for ch in range(SUBLANES):
        a_scr[ch * stride:ch * stride + length, :] = a[ch * length:(ch + 1) * length]
        b_scr[ch * stride:ch * stride + length, :] = b[ch * length:(ch + 1) * length]

    h = b_scr[_chunk_rows(0, stride), :]
    prod = a_scr[_chunk_rows(0, stride), :]
    for tau in range(1, length):
        rows = _chunk_rows(tau, stride)
        at = a_scr[rows, :]
        h = at * h + b_scr[rows, :]
        prod = at * prod
        b_scr[rows, :] = h
        a_scr[rows, :] = prod

    out = []
    state = jnp.zeros((1, c), F32)
    for ch in range(SUBLANES):
        lo = ch * stride
        out.append(b_scr[lo:lo + length, :] + a_scr[lo:lo + length, :] * state)
        state = h[ch:ch + 1] + prod[ch:ch + 1] * state
    return jnp.concatenate(out, axis=0)


def _cumsum_rows(x, scr):
    n, c = x.shape
    length = n // SUBLANES
    stride = length + SCAN_PAD
    for ch in range(SUBLANES):
        scr[ch * stride:ch * stride + length, :] = x[ch * length:(ch + 1) * length]
    h = scr[_chunk_rows(0, stride), :]
    for tau in range(1, length):
        rows = _chunk_rows(tau, stride)
        h = h + scr[rows, :]
        scr[rows, :] = h
    out = []
    state = jnp.zeros((1, c), F32)
    for ch in range(SUBLANES):
        lo = ch * stride
        out.append(scr[lo:lo + length, :] + state)
        state = state + h[ch:ch + 1]
    return jnp.concatenate(out, axis=0)


def _softplus(x):
    return jnp.maximum(x, 0.0) + jnp.log1p(jnp.exp(-jnp.abs(x)))


def _wcast_kernel(w_ref, o_ref):
    o_ref[...] = w_ref[...].astype(o_ref.dtype)


def _cast_mix_weights(w_in_t, *, rows):
    depth, _, d = w_in_t.shape
    spec = pl.BlockSpec((1, rows, d), lambda l, j: (l, j, 0))
    return pl.pallas_call(
        _wcast_kernel,
        grid=(depth, N_MIX_PAD // rows),
        in_specs=[spec],
        out_specs=spec,
        out_shape=jax.ShapeDtypeStruct((depth, N_MIX_PAD, d), BF16),
        compiler_params=_params(("parallel", "parallel")),
        name="cast_mix_weights",
    )(w_in_t)


def _inproj_kernel(x_ref, g_ref, w_ref, proj_ref, xn_ref):
    x = x_ref[...]
    ms = jnp.mean(x * x, axis=-1, keepdims=True)
    xn = ((x * lax.rsqrt(ms + EPS)) * g_ref[...]).astype(xn_ref.dtype)
    xn_ref[...] = xn
    proj_ref[...] = _dot_nt(xn, w_ref[0])


def _inproj(x, g, w_mix, layer, *, tm):
    t, d = x.shape
    n = w_mix.shape[1]
    return pl.pallas_call(
        _inproj_kernel,
        grid=(t // tm,),
        in_specs=[
            pl.BlockSpec((tm, d), lambda i: (i, 0)),
            pl.BlockSpec((1, d), lambda i: (0, 0)),
            pl.BlockSpec((1, n, d), lambda i: (layer, 0, 0), pipeline_mode=pl.Buffered(1)),
        ],
        out_specs=[
            pl.BlockSpec((tm, n), lambda i: (i, 0)),
            pl.BlockSpec((tm, d), lambda i: (i, 0)),
        ],
        out_shape=[
            jax.ShapeDtypeStruct((t, n), F32),
            jax.ShapeDtypeStruct((t, d), BF16),
        ],
        compiler_params=_params(("parallel",)),
        name="inproj",
    )(x, g, w_mix)


def _lru_slab(x, cw, cb, wr, br, wi, bi, lam, a_scr, b_scr):
    u = cw[LRU_CONV - 1:LRU_CONV] * x + cb
    for k in range(1, LRU_CONV):
        u = u + cw[LRU_CONV - 1 - k:LRU_CONV - k] * _shift_rows(x, k, 0.0)
    ub = u.astype(BF16)
    r = jax.nn.sigmoid(_dot(ub, wr) + br)
    gi = jax.nn.sigmoid(_dot(ub, wi) + bi)
    log_a = (-LRU_C * r) * _softplus(-lam)
    a = jnp.exp(log_a)
    inp = jnp.sqrt(-jnp.tanh(log_a) * (a * a + 1.0)) * (gi * u)
    return _linear_scan_rows(a, inp, a_scr, b_scr)


def _pool_slab(x, group, w, scale):
    w2 = x + _shift_rows(x, 1, 0.0)
    w4 = w2 + _shift_rows(w2, 2, 0.0)
    w8 = w4 + _shift_rows(w4, 4, 0.0)
    w16 = w8 + _shift_rows(w8, 8, 0.0)
    wsum = jnp.where(group == 0, w2, jnp.where(group == 1, w4, jnp.where(group == 2, w8, w16)))
    win = jnp.left_shift(2, group)
    t = lax.broadcasted_iota(jnp.int32, x.shape, 0)
    count = jnp.minimum(t + 1, win).astype(F32)
    pooled = wsum / count - x
    return _dot(pooled.astype(BF16), w) * scale


def _sconv_slab(gate_b, gate_c, xc, w):
    z = gate_c * xc
    acc = w[SCONV_K - 1:SCONV_K] * z
    for k in range(1, SCONV_K):
        acc = acc + w[SCONV_K - 1 - k:SCONV_K - k] * _shift_rows(z, k, 0.0)
    return gate_b * acc


def _mixers_kernel(xa_ref, xp_ref, gb_ref, gc_ref, xc_ref, f_ref,
                   cw_ref, cb_ref, wr_ref, br_ref, wi_ref, bi_ref, lam_ref,
                   pw_ref, ps_ref, sw_ref, fb_ref,
                   ya_ref, yb_ref, yc_ref, c_ref, a_scr, b_scr):
    slab = pl.program_id(1)
    ya_ref[...] = _lru_slab(xa_ref[...], cw_ref[...], cb_ref[...], wr_ref[0], br_ref[...],
                            wi_ref[0], bi_ref[...], lam_ref[...], a_scr, b_scr).astype(ya_ref.dtype)
    yb_ref[...] = _pool_slab(xp_ref[...], slab, pw_ref[0], ps_ref[...]).astype(yb_ref.dtype)
    yc_ref[...] = _sconv_slab(gb_ref[...], gc_ref[...], xc_ref[...], sw_ref[...]).astype(yc_ref.dtype)

    @pl.when(slab == 0)
    def _():
        z = f_ref[...] + fb_ref[...]
        log_f = jnp.minimum(z, 0.0) - jnp.log1p(jnp.exp(-jnp.abs(z)))
        head_lane = lax.broadcasted_iota(jnp.int32, (1, LANES), 1) < HEADS
        c_ref[...] = jnp.where(head_lane, _cumsum_rows(log_f, b_scr), 0.0)


def _mixers(proj, cw, cb, wr_bd, br, wi_bd, bi, lam, pool_w, pool_scale, sconv_w, fb, *, batch, seq):
    nc = WIDTH // LANES

    def col(c0, p=0):
        return pl.BlockSpec((seq, LANES), lambda b, c: (b, (c0 + p * WIDTH) // LANES + c))

    vec = pl.BlockSpec((1, LANES), lambda b, c: (0, c))
    mat = pl.BlockSpec((1, LANES, LANES), lambda b, c: (c, 0, 0))
    y_spec = pl.BlockSpec((seq, LANES), lambda b, c: (b, c))
    y_shape = jax.ShapeDtypeStruct((batch * seq, WIDTH), BF16)
    return pl.pallas_call(
        _mixers_kernel,
        grid=(batch, nc),
        in_specs=[col(COL_LRU), col(COL_POOL), col(COL_SCONV, 0), col(COL_SCONV, 1), col(COL_SCONV, 2),
                  pl.BlockSpec((seq, LANES), lambda b, c: (b, COL_FORGET // LANES)),
                  pl.BlockSpec((LRU_CONV, LANES), lambda b, c: (0, c)), vec, mat, vec, mat, vec, vec,
                  mat, vec,
                  pl.BlockSpec((SCONV_K, LANES), lambda b, c: (0, c)),
                  pl.BlockSpec((1, LANES), lambda b, c: (0, 0))],
        out_specs=[y_spec, y_spec, y_spec, pl.BlockSpec((seq, LANES), lambda b, c: (b, 0))],
        out_shape=[y_shape, y_shape, y_shape, jax.ShapeDtypeStruct((batch * seq, LANES), F32)],
        scratch_shapes=[pltpu.VMEM((_scan_scratch_rows(seq), LANES), F32)] * 2,
        compiler_params=_params(("parallel", "arbitrary")),
        name="mixers",
    )(proj, proj, proj, proj, proj, proj, cw, cb, wr_bd, br, wi_bd, bi, lam, pool_w, pool_scale, sconv_w, fb)


ATTN_TQ = 512
LOG2E = 1.4426950408889634
ATTN_ROWS = 64
ATTN_BIAS_LANES = 12


def _attn_kernel(q_ref, k_ref, v_ref, c_ref, sel_ref, qg_ref, kg_ref, hm_ref,
                 wu_ref, wd_ref, o_ref, wub_ref, wdb_ref, q_scr, k_scr, vt_scr, s_scr, p_scr):
    wub_ref[...] = wu_ref[0].astype(BF16)
    wdb_ref[...] = wd_ref[0].astype(BF16)

    seq = q_ref.shape[0]
    hm = hm_ref[...]

    def head_norm(x, g):
        sq = x * x
        hi = sq.astype(BF16)
        lo = (sq - hi.astype(F32)).astype(BF16)
        ms = (_dot(hi, hm) + _dot(lo, hm)) * (1.0 / HEAD_DIM)
        return (x * lax.rsqrt(ms + EPS)) * g

    tq = ATTN_TQ
    nblk = seq // tq
    qg, kg = qg_ref[...], kg_ref[...]
    lane = lax.broadcasted_iota(jnp.int32, (1, LANES), 1)
    first = lane < HEAD_DIM

    def lane_in(lo, hi):
        return (lane >= lo) & (lane < hi)

    sel_hi, sel_mid, sel_lo = sel_ref[0, 0], sel_ref[0, 1], sel_ref[0, 2]

    for j in range(nblk):
        rows = slice(j * tq, (j + 1) * tq)
        c = c_ref[rows, :] * LOG2E
        hi = c.astype(BF16)
        mid = (c - hi.astype(F32)).astype(BF16)
        lo = ((c - hi.astype(F32)) - mid.astype(F32)).astype(BF16)
        pieces = _dot(hi, sel_hi) + _dot(mid, sel_mid) + _dot(lo, sel_lo)
        k_extra = jnp.where(lane_in(0, 6), -pieces, jnp.where(lane_in(6, ATTN_BIAS_LANES), 1.0, 0.0))
        qa_extra = jnp.where(lane_in(0, 3), 1.0, jnp.where(lane_in(6, 9), pieces, 0.0))
        qb_extra = jnp.where(lane_in(3, 6), 1.0, jnp.where(lane_in(9, ATTN_BIAS_LANES), pieces, 0.0))
        qn = head_norm(q_ref[rows, :], qg) * (HEAD_DIM ** -0.5 * LOG2E)
        q_scr[j, :tq, :LANES] = jnp.where(first, qn, 0.0).astype(BF16)
        q_scr[j, :tq, LANES:] = qa_extra.astype(BF16)
        q_scr[j, tq:, :LANES] = jnp.where(first, 0.0, qn).astype(BF16)
        q_scr[j, tq:, LANES:] = qb_extra.astype(BF16)
        k_scr[rows, :LANES] = head_norm(k_ref[rows, :], kg).astype(BF16)
        k_scr[rows, LANES:] = k_extra.astype(BF16)
        vt_scr[:, rows] = v_ref[rows, :].T.astype(BF16)

    rb = ATTN_ROWS
    nq = 2 * tq
    key_pos = lax.broadcasted_iota(jnp.int32, (tq, nq), 0)
    lane_q = lax.broadcasted_iota(jnp.int32, (tq, nq), 1)
    causal = key_pos <= jnp.where(lane_q < tq, lane_q, lane_q - tq)

    def fold8(x, op):
        out = x[:SUBLANES]
        for g in range(1, x.shape[0] // SUBLANES):
            out = op(out, x[g * SUBLANES:(g + 1) * SUBLANES])
        return out

    for i in range(nblk):
        q0, kend = i * tq, (i + 1) * tq
        s_blk = s_scr.at[i % 2]
        p_blk = p_scr.at[i % 2]

        m8 = None
        for kb in range(i + 1):
            rows = slice(kb * tq, (kb + 1) * tq)
            s = _dot_nt(k_scr[rows, :], q_scr[i])
            if kb == i:
                s = jnp.where(causal, s, -jnp.inf)
            s_blk[rows, :] = s
            part = fold8(s, jnp.maximum)
            m8 = part if m8 is None else jnp.maximum(m8, part)
        m = jnp.max(m8, axis=0, keepdims=True)

        l8 = jnp.zeros((SUBLANES, nq), F32)
        for c in range(kend // rb):
            p = jnp.exp2(s_blk[c * rb:(c + 1) * rb, :] - m)
            p_blk[c * rb:(c + 1) * rb, :] = p.astype(BF16)
            l8 = l8 + fold8(p, jnp.add)
        l = jnp.sum(l8, axis=0, keepdims=True)

        o_t = _dot(vt_scr[:, :kend], p_blk[:kend, :]) / l
        o = o_t.T
        o_ref[q0:kend, :] = jnp.where(first, o[:tq], o[tq:]).astype(o_ref.dtype)


def _bias_lane_selectors():
    pair = jnp.arange(HEADS // 2)[:, None, None, None]
    p = jnp.arange(3)[None, :, None, None]
    row = jnp.arange(LANES)[None, None, :, None]
    col = jnp.arange(LANES)[None, None, None, :]
    head_a = (row == 2 * pair) & ((col == p) | (col == 6 + p))
    head_b = (row == 2 * pair + 1) & ((col == 3 + p) | (col == 9 + p))
    return (head_a | head_b).astype(BF16)


def _attention(proj, c, qg, kg, hm, wu, wd, layer, *, batch, seq):
    npair = HEADS // 2
    col0 = COL_QKV // LANES
    steps = batch * npair
    _, d, f = wu.shape
    ru, rd = d // steps, f // steps

    def part(p):
        return pl.BlockSpec((seq, LANES), lambda b, h: (b, col0 + p * npair + h))

    vec = pl.BlockSpec((1, LANES), lambda b, h: (0, 0))
    return pl.pallas_call(
        _attn_kernel,
        grid=(batch, npair),
        in_specs=[part(0), part(1), part(2),
                  pl.BlockSpec((seq, LANES), lambda b, h: (b, 0)),
                  pl.BlockSpec((1, 3, LANES, LANES), lambda b, h: (h, 0, 0, 0)),
                  vec, vec, pl.BlockSpec((LANES, LANES), lambda b, h: (0, 0)),
                  pl.BlockSpec((1, ru, f), lambda b, h: (layer, b * npair + h, 0)),
                  pl.BlockSpec((1, rd, d), lambda b, h: (layer, b * npair + h, 0))],
        out_specs=[pl.BlockSpec((seq, LANES), lambda b, h: (b, h)),
                   pl.BlockSpec((ru, f), lambda b, h: (b * npair + h, 0)),
                   pl.BlockSpec((rd, d), lambda b, h: (b * npair + h, 0))],
        out_shape=[jax.ShapeDtypeStruct((batch * seq, WIDTH), BF16),
                   jax.ShapeDtypeStruct((d, f), BF16),
                   jax.ShapeDtypeStruct((f, d), BF16)],
        scratch_shapes=[pltpu.VMEM((seq // ATTN_TQ, 2 * ATTN_TQ, 2 * LANES), BF16),
                        pltpu.VMEM((seq, 2 * LANES), BF16),
                        pltpu.VMEM((LANES, seq), BF16),
                        pltpu.VMEM((2, seq, 2 * ATTN_TQ), F32),
                        pltpu.VMEM((2, seq, 2 * ATTN_TQ), BF16)],
        compiler_params=_params(("parallel", "parallel")),
        name="fox_attention",
    )(proj, proj, proj, c, _bias_lane_selectors(), qg, kg, hm, wu, wd)


def _merge_kernel(xn_ref, ya_ref, yb_ref, yc_ref, yd_ref, wg0_ref, wg1_ref, wg2_ref, wg3_ref, wb_ref, o_ref):
    xn = xn_ref[...]
    acc = None
    for k, (y_ref, wg_ref) in enumerate(((ya_ref, wg0_ref), (yb_ref, wg1_ref),
                                         (yc_ref, wg2_ref), (yd_ref, wg3_ref))):
        gate = jax.nn.sigmoid(_dot_nt(xn, wg_ref[...].astype(BF16)))
        term = gate * _dot(y_ref[...], wb_ref[0, k].astype(BF16))
        acc = term if acc is None else acc + term
    o_ref[...] = acc.astype(o_ref.dtype)


def _merge(xn, ys, w_in_t, wb, layer, *, tm, tn):
    t, d = xn.shape
    nblk = d // tn

    def wg_spec(k):
        return pl.BlockSpec((pl.Squeezed(), pl.Element(tn), pl.Element(d)),
                            lambda i, j: (layer, pl.multiple_of(N_MIX + k * d + j * tn, SUBLANES), 0))

    y_spec = pl.BlockSpec((tm, WIDTH), lambda i, j: (i, 0))
    return pl.pallas_call(
        _merge_kernel,
        grid=(t // tm, nblk),
        in_specs=[pl.BlockSpec((tm, d), lambda i, j: (i, 0)), y_spec, y_spec, y_spec, y_spec,
                  wg_spec(0), wg_spec(1), wg_spec(2), wg_spec(3),
                  pl.BlockSpec((1, N_BRANCH, WIDTH, tn), lambda i, j: (layer, 0, 0, j))],
        out_specs=pl.BlockSpec((tm, tn), lambda i, j: (i, j)),
        out_shape=jax.ShapeDtypeStruct((t, d), BF16),
        compiler_params=_params(("parallel", "arbitrary")),
        name="merge",
    )(xn, *ys, w_in_t, w_in_t, w_in_t, w_in_t, wb)


def _outproj_kernel(m_ref, x_ref, w_ref, g_ref, x1_ref, xn_ref, wb_scr):
    @pl.when(pl.program_id(0) == 0)
    def _():
        wb_scr[...] = w_ref[0].astype(BF16)

    x1 = x_ref[...] + _dot(m_ref[...], wb_scr[...])
    x1_ref[...] = x1
    ms = jnp.mean(x1 * x1, axis=-1, keepdims=True)
    xn_ref[...] = ((x1 * lax.rsqrt(ms + EPS)) * g_ref[...]).astype(xn_ref.dtype)


def _outproj(merged, x, w_out, g, layer, *, tm):
    t, d = x.shape
    row = pl.BlockSpec((tm, d), lambda i: (i, 0))
    return pl.pallas_call(
        _outproj_kernel,
        grid=(t // tm,),
        in_specs=[row, row,
                  pl.BlockSpec((1, d, d), lambda i: (layer, 0, 0), pipeline_mode=pl.Buffered(1)),
                  pl.BlockSpec((1, d), lambda i: (0, 0))],
        out_specs=[row, row],
        out_shape=[jax.ShapeDtypeStruct((t, d), F32), jax.ShapeDtypeStruct((t, d), BF16)],
        scratch_shapes=[pltpu.VMEM((d, d), BF16)],
        compiler_params=_params(("arbitrary",)),
        name="outproj",
    )(merged, x, w_out, g)


def _mlp_kernel(xn_ref, x_ref, wu_ref, wd_ref, o_ref):
    @pl.when(pl.program_id(1) == 0)
    def _():
        o_ref[...] = x_ref[...]

    h = jnp.maximum(_dot(xn_ref[...], wu_ref[...]), 0.0)
    o_ref[...] += _dot((h * h).astype(BF16), wd_ref[...])


def _mlp(xn, x, wu, wd, *, tm, tf):
    t, d = x.shape
    f = wu.shape[1]
    row = pl.BlockSpec((tm, d), lambda i, j: (i, 0))
    return pl.pallas_call(
        _mlp_kernel,
        grid=(t // tm, f // tf),
        in_specs=[row, row,
                  pl.BlockSpec((d, tf), lambda i, j: (0, j)),
                  pl.BlockSpec((tf, d), lambda i, j: (j, 0))],
        out_specs=row,
        out_shape=jax.ShapeDtypeStruct((t, d), F32),
        compiler_params=_params(("parallel", "arbitrary")),
        name="mlp",
    )(xn, x, wu, wd)


def _block_diag_pairs(w):
    n2, bd, _ = w.shape
    per = LANES // bd
    w4 = w.reshape(n2 // per, per, bd, bd)
    eye = jnp.eye(per, dtype=w.dtype)
    return jnp.einsum('jaik,ab->jaibk', w4, eye).reshape(n2 // per, LANES, LANES)


def _row(v):
    return v.reshape(1, -1).astype(F32)


def kernel(x, norm_mix_g, w_in, lru_conv_w, lru_conv_b, lru_wr, lru_br, lru_wi, lru_bi, lru_lambda, pool_w, pool_scale, sconv_w, q_norm_g, k_norm_g, forget_b, w_branch, w_out, norm_mlp_g, w_mlp_up, w_mlp_down):
    batch, seq, d = x.shape
    assert d == D_MODEL and seq % ATTN_TQ == 0
    depth = w_in.shape[0]
    t = batch * seq
    xf = x.reshape(t, d)

    lane = jnp.arange(LANES)
    same_head = (lane[:, None] // HEAD_DIM == lane[None, :] // HEAD_DIM).astype(BF16)

    w_in_t = jnp.swapaxes(w_in, 1, 2)
    w_mix = _cast_mix_weights(w_in_t, rows=N_MIX_PAD // 8)

    for l in range(depth):
        proj, xn = _inproj(xf, _row(norm_mix_g[l]), w_mix, l, tm=512)

        fb = jnp.pad(_row(forget_b[l]), ((0, 0), (0, LANES - HEADS)))
        y_a, y_b, y_c, c = _mixers(
            proj, lru_conv_w[l], _row(lru_conv_b[l]), _block_diag_pairs(lru_wr[l]).astype(BF16),
            _row(lru_br[l]), _block_diag_pairs(lru_wi[l]).astype(BF16), _row(lru_bi[l]),
            _row(lru_lambda[l]), pool_w[l].astype(BF16), _row(pool_scale[l]), sconv_w[l], fb,
            batch=batch, seq=seq)
        qg = jnp.tile(_row(q_norm_g[l]), (1, LANES // HEAD_DIM))
        kg = jnp.tile(_row(k_norm_g[l]), (1, LANES // HEAD_DIM))
        y_d, wu_bf, wd_bf = _attention(proj, c, qg, kg, same_head, w_mlp_up, w_mlp_down, l,
                                       batch=batch, seq=seq)

        merged = _merge(xn, (y_a, y_b, y_c, y_d), w_in_t, w_branch, l, tm=1024, tn=256)
        x1, xn2 = _outproj(merged, xf, w_out, _row(norm_mlp_g[l]), l, tm=512)
        xf = _mlp(xn2, x1, wu_bf, wd_bf, tm=512, tf=2048)

    return xf.reshape(batch, seq, d)
```

```python
import functools

import jax
import jax.numpy as jnp
from jax import lax
from jax.experimental import pallas as pl
from jax.experimental.pallas import tpu as pltpu

F32 = jnp.float32
BF16 = jnp.bfloat16

D_MODEL = 2048
WIDTH = 512
N_BRANCH = 4
LRU_BLOCK_DIM = 64
LRU_CONV = 4
LRU_C = 8.0
POOL_GROUPS = 4
SCONV_K = 3
HEADS = 8
HEAD_DIM = 64
D_FF = 4 * D_MODEL
EPS = 1e-6

LANES = 128
SUBLANES = 8
VMEM_LIMIT = 56 * 1024 * 1024

N_MIX = 2 * WIDTH + 3 * WIDTH + 3 * WIDTH + HEADS
N_MIX_PAD = -(-N_MIX // LANES) * LANES
COL_LRU = 0
COL_POOL = WIDTH
COL_SCONV = 2 * WIDTH
COL_QKV = 5 * WIDTH
COL_FORGET = 8 * WIDTH


def _params(semantics):
    return pltpu.CompilerParams(dimension_semantics=semantics, vmem_limit_bytes=VMEM_LIMIT)


def _dot(a, b):
    return jnp.dot(a, b, preferred_element_type=F32)


def _dot_nt(a, b):
    return lax.dot_general(a, b, (((1,), (1,)), ((), ())), preferred_element_type=F32)


def _shift_rows(x, k, fill):
    n, c = x.shape
    if k >= n:
        return jnp.full((n, c), fill, x.dtype)
    if k % SUBLANES == 0:
        return jnp.concatenate([jnp.full((k, c), fill, x.dtype), x[:n - k]], axis=0)
    r = pltpu.roll(x, k, 0)
    row = lax.broadcasted_iota(jnp.int32, (SUBLANES, c), 0)
    top = jnp.where(row < k, jnp.asarray(fill, x.dtype), r[:SUBLANES])
    return jnp.concatenate([top, r[SUBLANES:]], axis=0)


SCAN_PAD = 4


def _scan_scratch_rows(seq):
    return seq + SUBLANES * SCAN_PAD


def _chunk_rows(tau, stride):
    return pl.ds(tau, SUBLANES, stride=stride)


def _linear_scan_rows(a, b, a_scr, b_scr):
    n, c = a.shape
    length = n // SUBLANES
    stride = length + SCAN_PAD
    for ch in range(SUBLANES):
        a_scr[ch * stride:ch * stride + length, :] = a[ch * length:(ch + 1) * length]
        b_scr[ch * stride:ch * stride + length, :] = b[ch * length:(ch + 1) * length]

    h = b_scr[_chunk_rows(0, stride), :]
    prod = a_scr[_chunk_rows(0, stride), :]
    for tau in range(1, length):
        rows = _chunk_rows(tau, stride)
        at = a_scr[rows, :]
        h = at * h + b_scr[rows, :]
        prod = at * prod
        b_scr[rows, :] = h
        a_scr[rows, :] = prod

    out = []
    state = jnp.zeros((1, c), F32)
    for ch in range(SUBLANES):
        lo = ch * stride
        out.append(b_scr[lo:lo + length, :] + a_scr[lo:lo + length, :] * state)
        state = h[ch:ch + 1] + prod[ch:ch + 1] * state
    return jnp.concatenate(out, axis=0)


def _cumsum_rows(x, scr):
    n, c = x.shape
    length = n // SUBLANES
    stride = length + SCAN_PAD
    for ch in range(SUBLANES):
        scr[ch * stride:ch * stride + length, :] = x[ch * length:(ch + 1) * length]
    h = scr[_chunk_rows(0, stride), :]
    for tau in range(1, length):
        rows = _chunk_rows(tau, stride)
        h = h + scr[rows, :]
        scr[rows, :] = h
    out = []
    state = jnp.zeros((1, c), F32)
    for ch in range(SUBLANES):
        lo = ch * stride
        out.append(scr[lo:lo + length, :] + state)
        state = state + h[ch:ch + 1]
    return jnp.concatenate(out, axis=0)


def _softplus(x):
    return jnp.maximum(x, 0.0) + jnp.log1p(jnp.exp(-jnp.abs(x)))


def _wcast_kernel(w_ref, o_ref):
    o_ref[...] = w_ref[...].astype(o_ref.dtype)


def _cast_mix_weights(w_in_t, *, rows):
    depth, _, d = w_in_t.shape
    spec = pl.BlockSpec((1, rows, d), lambda l, j: (l, j, 0))
    return pl.pallas_call(
        _wcast_kernel,
        grid=(depth, N_MIX_PAD // rows),
        in_specs=[spec],
        out_specs=spec,
        out_shape=jax.ShapeDtypeStruct((depth, N_MIX_PAD, d), BF16),
        compiler_params=_params(("parallel", "parallel")),
        name="cast_mix_weights",
    )(w_in_t)


def _inproj_kernel(x_ref, g_ref, w_ref, proj_ref, xn_ref):
    x = x_ref[...]
    ms = jnp.mean(x * x, axis=-1, keepdims=True)
    xn = ((x * lax.rsqrt(ms + EPS)) * g_ref[...]).astype(xn_ref.dtype)
    xn_ref[...] = xn
    proj_ref[...] = _dot_nt(xn, w_ref[0])


def _inproj(x, g, w_mix, layer, *, tm):
    t, d = x.shape
    n = w_mix.shape[1]
    return pl.pallas_call(
        _inproj_kernel,
        grid=(t // tm,),
        in_specs=[
            pl.BlockSpec((tm, d), lambda i: (i, 0)),
            pl.BlockSpec((None, 1, d), lambda i: (layer, 0, 0)),
            pl.BlockSpec((1, n, d), lambda i: (layer, 0, 0), pipeline_mode=pl.Buffered(1)),
        ],
        out_specs=[
            pl.BlockSpec((tm, n), lambda i: (i, 0)),
            pl.BlockSpec((tm, d), lambda i: (i, 0)),
        ],
        out_shape=[
            jax.ShapeDtypeStruct((t, n), F32),
            jax.ShapeDtypeStruct((t, d), BF16),
        ],
        compiler_params=_params(("parallel",)),
        name="inproj",
    )(x, g, w_mix)


def _lru_slab(x, cw, cb, wr, br, wi, bi, lam, a_scr, b_scr):
    u = cw[LRU_CONV - 1:LRU_CONV] * x + cb
    for k in range(1, LRU_CONV):
        u = u + cw[LRU_CONV - 1 - k:LRU_CONV - k] * _shift_rows(x, k, 0.0)
    ub = u.astype(BF16)
    r = jax.nn.sigmoid(_dot(ub, wr) + br)
    gi = jax.nn.sigmoid(_dot(ub, wi) + bi)
    log_a = (-LRU_C * r) * _softplus(-lam)
    a = jnp.exp(log_a)
    inp = jnp.sqrt(-jnp.tanh(log_a) * (a * a + 1.0)) * (gi * u)
    return _linear_scan_rows(a, inp, a_scr, b_scr)


def _pool_slab(x, group, w, scale):
    w2 = x + _shift_rows(x, 1, 0.0)
    w4 = w2 + _shift_rows(w2, 2, 0.0)
    w8 = w4 + _shift_rows(w4, 4, 0.0)
    w16 = w8 + _shift_rows(w8, 8, 0.0)
    wsum = jnp.where(group == 0, w2, jnp.where(group == 1, w4, jnp.where(group == 2, w8, w16)))
    win = jnp.left_shift(2, group)
    t = lax.broadcasted_iota(jnp.int32, x.shape, 0)
    count = jnp.minimum(t + 1, win).astype(F32)
    pooled = wsum / count - x
    return _dot(pooled.astype(BF16), w) * scale


def _sconv_slab(gate_b, gate_c, xc, w):
    z = gate_c * xc
    acc = w[SCONV_K - 1:SCONV_K] * z
    for k in range(1, SCONV_K):
        acc = acc + w[SCONV_K - 1 - k:SCONV_K - k] * _shift_rows(z, k, 0.0)
    return gate_b * acc


def _mixers_kernel(xa_ref, xp_ref, gb_ref, gc_ref, xc_ref, f_ref,
                   cw_ref, cb_ref, wr_ref, br_ref, wi_ref, bi_ref, lam_ref,
                   pw_ref, ps_ref, sw_ref, fb_ref,
                   ya_ref, yb_ref, yc_ref, c_ref, a_scr, b_scr):
    slab = pl.program_id(1)
    ya_ref[...] = _lru_slab(xa_ref[...], cw_ref[...], cb_ref[...], wr_ref[0], br_ref[...],
                            wi_ref[0], bi_ref[...], lam_ref[...], a_scr, b_scr).astype(ya_ref.dtype)
    yb_ref[...] = _pool_slab(xp_ref[...], slab, pw_ref[0], ps_ref[...]).astype(yb_ref.dtype)
    yc_ref[...] = _sconv_slab(gb_ref[...], gc_ref[...], xc_ref[...], sw_ref[...]).astype(yc_ref.dtype)

    @pl.when(slab == 0)
    def _():
        z = f_ref[...] + fb_ref[...]
        log_f = jnp.minimum(z, 0.0) - jnp.log1p(jnp.exp(-jnp.abs(z)))
        head_lane = lax.broadcasted_iota(jnp.int32, (1, LANES), 1) < HEADS
        c_ref[...] = jnp.where(head_lane, _cumsum_rows(log_f, b_scr), 0.0)


def _mixers(proj, cw, cb, wr_bd, br, wi_bd, bi, lam, pool_w, pool_scale, sconv_w, fb, layer, *, batch, seq):
    nc = WIDTH // LANES

    def col(c0, p=0):
        return pl.BlockSpec((seq, LANES), lambda b, c: (b, (c0 + p * WIDTH) // LANES + c))

    vec = pl.BlockSpec((None, 1, LANES), lambda b, c: (layer, 0, c))
    mat = pl.BlockSpec((None, 1, LANES, LANES), lambda b, c: (layer, c, 0, 0))
    y_spec = pl.BlockSpec((seq, LANES), lambda b, c: (b, c))
    y_shape = jax.ShapeDtypeStruct((batch * seq, WIDTH), BF16)
    return pl.pallas_call(
        _mixers_kernel,
        grid=(batch, nc),
        in_specs=[col(COL_LRU), col(COL_POOL), col(COL_SCONV, 0), col(COL_SCONV, 1), col(COL_SCONV, 2),
                  pl.BlockSpec((seq, LANES), lambda b, c: (b, COL_FORGET // LANES)),
                  pl.BlockSpec((None, LRU_CONV, LANES), lambda b, c: (layer, 0, c)),
                  vec, mat, vec, mat, vec, vec,
                  mat, vec,
                  pl.BlockSpec((None, SCONV_K, LANES), lambda b, c: (layer, 0, c)),
                  pl.BlockSpec((None, 1, LANES), lambda b, c: (layer, 0, 0))],
        out_specs=[y_spec, y_spec, y_spec, pl.BlockSpec((seq, LANES), lambda b, c: (b, 0))],
        out_shape=[y_shape, y_shape, y_shape, jax.ShapeDtypeStruct((batch * seq, LANES), F32)],
        scratch_shapes=[pltpu.VMEM((_scan_scratch_rows(seq), LANES), F32)] * 2,
        compiler_params=_params(("parallel", "arbitrary")),
        name="mixers",
    )(proj, proj, proj, proj, proj, proj, cw, cb, wr_bd, br, wi_bd, bi, lam, pool_w, pool_scale, sconv_w, fb)


ATTN_TQ = 512
LOG2E = 1.4426950408889634
ATTN_ROWS = 64
ATTN_BIAS_LANES = 12


def _attn_kernel(q_ref, k_ref, v_ref, c_ref, sel_ref, qg_ref, kg_ref, hm_ref,
                 wu_ref, wd_ref, wo_ref, o_ref, wub_ref, wdb_ref, wob_ref,
                 q_scr, k_scr, vt_scr, s_scr, p_scr):
    wub_ref[...] = wu_ref[0].astype(BF16)
    wdb_ref[...] = wd_ref[0].astype(BF16)
    wob_ref[...] = wo_ref[0].astype(BF16)

    seq = q_ref.shape[0]
    hm = hm_ref[...]

    def head_norm(x, g):
        sq = x * x
        hi = sq.astype(BF16)
        lo = (sq - hi.astype(F32)).astype(BF16)
        ms = _dot(hi, hm) + _dot(lo, hm)
        return (x * lax.rsqrt(ms + EPS)) * g

    tq = ATTN_TQ
    nblk = seq // tq
    qg, kg = qg_ref[...] * (HEAD_DIM ** -0.5 * LOG2E), kg_ref[...]
    lane = lax.broadcasted_iota(jnp.int32, (1, LANES), 1)
    first = lane < HEAD_DIM

    def lane_in(lo, hi):
        return (lane >= lo) & (lane < hi)

    sel_hi, sel_mid, sel_lo = sel_ref[0, 0], sel_ref[0, 1], sel_ref[0, 2]

    for j in range(nblk):
        rows = slice(j * tq, (j + 1) * tq)
        c = c_ref[rows, :] * LOG2E
        hi = c.astype(BF16)
        mid = (c - hi.astype(F32)).astype(BF16)
        lo = ((c - hi.astype(F32)) - mid.astype(F32)).astype(BF16)
        pieces = _dot(hi, sel_hi) + _dot(mid, sel_mid) + _dot(lo, sel_lo)
        k_extra = jnp.where(lane_in(0, 6), -pieces, jnp.where(lane_in(6, ATTN_BIAS_LANES), 1.0, 0.0))
        qa_extra = jnp.where(lane_in(0, 3), 1.0, jnp.where(lane_in(6, 9), pieces, 0.0))
        qb_extra = jnp.where(lane_in(3, 6), 1.0, jnp.where(lane_in(9, ATTN_BIAS_LANES), pieces, 0.0))
        qn = head_norm(q_ref[rows, :], qg)
        q_scr[j, :tq, :LANES] = jnp.where(first, qn, 0.0).astype(BF16)
        q_scr[j, :tq, LANES:] = qa_extra.astype(BF16)
        q_scr[j, tq:, :LANES] = jnp.where(first, 0.0, qn).astype(BF16)
        q_scr[j, tq:, LANES:] = qb_extra.astype(BF16)
        k_scr[rows, :LANES] = head_norm(k_ref[rows, :], kg).astype(BF16)
        k_scr[rows, LANES:] = k_extra.astype(BF16)
        vt_scr[:, rows] = v_ref[rows, :].T.astype(BF16)

    rb = ATTN_ROWS
    nq = 2 * tq
    key_pos = lax.broadcasted_iota(jnp.int32, (tq, nq), 0)
    lane_q = lax.broadcasted_iota(jnp.int32, (tq, nq), 1)
    causal = key_pos <= jnp.where(lane_q < tq, lane_q, lane_q - tq)

    def fold8(x, op):
        out = x[:SUBLANES]
        for g in range(1, x.shape[0] // SUBLANES):
            out = op(out, x[g * SUBLANES:(g + 1) * SUBLANES])
        return out

    for i in range(nblk):
        q0, kend = i * tq, (i + 1) * tq
        s_blk = s_scr.at[i % 2]
        p_blk = p_scr.at[i % 2]

        m8 = None
        for kb in range(i + 1):
            rows = slice(kb * tq, (kb + 1) * tq)
            s = _dot_nt(k_scr[rows, :], q_scr[i])
            if kb == i:
                s = jnp.where(causal, s, -jnp.inf)
            s_blk[rows, :] = s
            part = fold8(s, jnp.maximum)
            m8 = part if m8 is None else jnp.maximum(m8, part)
        m = jnp.max(m8, axis=0, keepdims=True)

        l8 = jnp.zeros((SUBLANES, nq), F32)
        for c in range(kend // rb):
            p = jnp.exp2(s_blk[c * rb:(c + 1) * rb, :] - m)
            p_blk[c * rb:(c + 1) * rb, :] = p.astype(BF16)
            l8 = l8 + fold8(p, jnp.add)
        l = jnp.sum(l8, axis=0, keepdims=True)

        o_t = _dot(vt_scr[:, :kend], p_blk[:kend, :]) / l
        o = o_t.T
        o_ref[q0:kend, :] = jnp.where(first, o[:tq], o[tq:]).astype(o_ref.dtype)


def _bias_lane_selectors():
    pair = jnp.arange(HEADS // 2)[:, None, None, None]
    p = jnp.arange(3)[None, :, None, None]
    row = jnp.arange(LANES)[None, None, :, None]
    col = jnp.arange(LANES)[None, None, None, :]
    head_a = (row == 2 * pair) & ((col == p) | (col == 6 + p))
    head_b = (row == 2 * pair + 1) & ((col == 3 + p) | (col == 9 + p))
    return (head_a | head_b).astype(BF16)


def _attention(proj, c, qg, kg, hm, wu, wd, wo, layer, *, batch, seq):
    npair = HEADS // 2
    col0 = COL_QKV // LANES
    steps = batch * npair
    _, d, f = wu.shape
    ru, rd, ro = d // steps, f // steps, d // steps

    def part(p):
        return pl.BlockSpec((seq, LANES), lambda b, h: (b, col0 + p * npair + h))

    def slab(rows, cols):
        return (pl.BlockSpec((1, rows, cols), lambda b, h: (layer, b * npair + h, 0)),
                pl.BlockSpec((rows, cols), lambda b, h: (b * npair + h, 0)))

    (wu_in, wu_out), (wd_in, wd_out), (wo_in, wo_out) = slab(ru, f), slab(rd, d), slab(ro, d)
    vec = pl.BlockSpec((None, 1, LANES), lambda b, h: (layer, 0, 0))
    return pl.pallas_call(
        _attn_kernel,
        grid=(batch, npair),
        in_specs=[part(0), part(1), part(2),
                  pl.BlockSpec((seq, LANES), lambda b, h: (b, 0)),
                  pl.BlockSpec((1, 3, LANES, LANES), lambda b, h: (h, 0, 0, 0)),
                  vec, vec, pl.BlockSpec((LANES, LANES), lambda b, h: (0, 0)),
                  wu_in, wd_in, wo_in],
        out_specs=[pl.BlockSpec((seq, LANES), lambda b, h: (b, h)), wu_out, wd_out, wo_out],
        out_shape=[jax.ShapeDtypeStruct((batch * seq, WIDTH), BF16),
                   jax.ShapeDtypeStruct((d, f), BF16),
                   jax.ShapeDtypeStruct((f, d), BF16),
                   jax.ShapeDtypeStruct((d, d), BF16)],
        scratch_shapes=[pltpu.VMEM((seq // ATTN_TQ, 2 * ATTN_TQ, 2 * LANES), BF16),
                        pltpu.VMEM((seq, 2 * LANES), BF16),
                        pltpu.VMEM((LANES, seq), BF16),
                        pltpu.VMEM((2, seq, 2 * ATTN_TQ), F32),
                        pltpu.VMEM((2, seq, 2 * ATTN_TQ), BF16)],
        compiler_params=_params(("parallel", "parallel")),
        name="fox_attention",
    )(proj, proj, proj, c, _bias_lane_selectors(), qg, kg, hm, wu, wd, wo)


def _merge_kernel(xn_ref, ya_ref, yb_ref, yc_ref, yd_ref, wg0_ref, wg1_ref, wg2_ref, wg3_ref, wb_ref, o_ref):
    xn = xn_ref[...]
    acc = None
    for k, (y_ref, wg_ref) in enumerate(((ya_ref, wg0_ref), (yb_ref, wg1_ref),
                                         (yc_ref, wg2_ref), (yd_ref, wg3_ref))):
        gate = jax.nn.sigmoid(_dot_nt(xn, wg_ref[...].astype(BF16)))
        term = gate * _dot(y_ref[...], wb_ref[0, k].astype(BF16))
        acc = term if acc is None else acc + term
    o_ref[...] = acc.astype(o_ref.dtype)


def _merge(xn, ys, w_in_t, wb, layer, *, tm, tn):
    t, d = xn.shape
    nblk = d // tn

    def wg_spec(k):
        return pl.BlockSpec((pl.Squeezed(), pl.Element(tn), pl.Element(d)),
                            lambda i, j: (layer, pl.multiple_of(N_MIX + k * d + j * tn, SUBLANES), 0))

    y_spec = pl.BlockSpec((tm, WIDTH), lambda i, j: (i, 0))
    return pl.pallas_call(
        _merge_kernel,
        grid=(t // tm, nblk),
        in_specs=[pl.BlockSpec((tm, d), lambda i, j: (i, 0)), y_spec, y_spec, y_spec, y_spec,
                  wg_spec(0), wg_spec(1), wg_spec(2), wg_spec(3),
                  pl.BlockSpec((1, N_BRANCH, WIDTH, tn), lambda i, j: (layer, 0, 0, j))],
        out_specs=pl.BlockSpec((tm, tn), lambda i, j: (i, j)),
        out_shape=jax.ShapeDtypeStruct((t, d), BF16),
        compiler_params=_params(("parallel", "arbitrary")),
        name="merge",
    )(xn, *ys, w_in_t, w_in_t, w_in_t, w_in_t, wb)


def _outproj_kernel(m_ref, x_ref, w_ref, g_ref, x1_ref, xn_ref):
    x1 = x_ref[...] + _dot(m_ref[...], w_ref[...])
    x1_ref[...] = x1
    ms = jnp.mean(x1 * x1, axis=-1, keepdims=True)
    xn_ref[...] = ((x1 * lax.rsqrt(ms + EPS)) * g_ref[...]).astype(xn_ref.dtype)


def _outproj(merged, x, w_out, g, layer, *, tm):
    t, d = x.shape
    row = pl.BlockSpec((tm, d), lambda i: (i, 0))
    return pl.pallas_call(
        _outproj_kernel,
        grid=(t // tm,),
        in_specs=[row, row,
                  pl.BlockSpec((d, d), lambda i: (0, 0), pipeline_mode=pl.Buffered(1)),
                  pl.BlockSpec((None, 1, d), lambda i: (layer, 0, 0))],
        out_specs=[row, row],
        out_shape=[jax.ShapeDtypeStruct((t, d), F32), jax.ShapeDtypeStruct((t, d), BF16)],
        compiler_params=_params(("parallel",)),
        name="outproj",
    )(merged, x, w_out, g)


def _mlp_kernel(xn_ref, x_ref, wu_ref, wd_ref, o_ref):
    @pl.when(pl.program_id(1) == 0)
    def _():
        o_ref[...] = x_ref[...]

    h = jnp.maximum(_dot(xn_ref[...], wu_ref[...]), 0.0)
    o_ref[...] += _dot((h * h).astype(BF16), wd_ref[...])


def _mlp(xn, x, wu, wd, *, tm, tf):
    t, d = x.shape
    f = wu.shape[1]
    row = pl.BlockSpec((tm, d), lambda i, j: (i, 0))
    return pl.pallas_call(
        _mlp_kernel,
        grid=(t // tm, f // tf),
        in_specs=[row, row,
                  pl.BlockSpec((d, tf), lambda i, j: (0, j)),
                  pl.BlockSpec((tf, d), lambda i, j: (j, 0))],
        out_specs=row,
        out_shape=jax.ShapeDtypeStruct((t, d), F32),
        compiler_params=_params(("parallel", "arbitrary")),
        name="mlp",
    )(xn, x, wu, wd)


def _block_diag_pairs(w):
    depth, n2, bd, _ = w.shape
    per = LANES // bd
    w5 = w.reshape(depth, n2 // per, per, bd, bd)
    eye = jnp.eye(per, dtype=w.dtype)
    return jnp.einsum('ljaik,ab->ljaibk', w5, eye).reshape(depth, n2 // per, LANES, LANES)


def _rows(v):
    return v.reshape(v.shape[0], 1, -1).astype(F32)


def kernel(x, norm_mix_g, w_in, lru_conv_w, lru_conv_b, lru_wr, lru_br, lru_wi, lru_bi, lru_lambda, pool_w, pool_scale, sconv_w, q_norm_g, k_norm_g, forget_b, w_branch, w_out, norm_mlp_g, w_mlp_up, w_mlp_down):
    batch, seq, d = x.shape
    assert d == D_MODEL and seq % ATTN_TQ == 0
    depth = w_in.shape[0]
    t = batch * seq
    xf = x.reshape(t, d)

    lane = jnp.arange(LANES)
    same_head = ((lane[:, None] // HEAD_DIM == lane[None, :] // HEAD_DIM) / HEAD_DIM).astype(BF16)

    w_in_t = jnp.swapaxes(w_in, 1, 2)
    w_mix = _cast_mix_weights(w_in_t, rows=N_MIX_PAD // 8)

    g_mix, g_mlp = _rows(norm_mix_g), _rows(norm_mlp_g)
    lru_cb, lru_rb, lru_ib, lru_lam = _rows(lru_conv_b), _rows(lru_br), _rows(lru_bi), _rows(lru_lambda)
    lru_wr_bd = _block_diag_pairs(lru_wr).astype(BF16)
    lru_wi_bd = _block_diag_pairs(lru_wi).astype(BF16)
    pool_wb, pool_sc = pool_w.astype(BF16), _rows(pool_scale)
    fb = jnp.pad(_rows(forget_b), ((0, 0), (0, 0), (0, LANES - HEADS)))
    qg = jnp.tile(_rows(q_norm_g), (1, 1, LANES // HEAD_DIM))
    kg = jnp.tile(_rows(k_norm_g), (1, 1, LANES // HEAD_DIM))

    for l in range(depth):
        proj, xn = _inproj(xf, g_mix, w_mix, l, tm=512)
        y_a, y_b, y_c, c = _mixers(proj, lru_conv_w, lru_cb, lru_wr_bd, lru_rb, lru_wi_bd, lru_ib, lru_lam,
                                   pool_wb, pool_sc, sconv_w, fb, l, batch=batch, seq=seq)
        y_d, wu_bf, wd_bf, wo_bf = _attention(proj, c, qg, kg, same_head, w_mlp_up, w_mlp_down, w_out, l,
                                              batch=batch, seq=seq)
        merged = _merge(xn, (y_a, y_b, y_c, y_d), w_in_t, w_branch, l, tm=1024, tn=256)
        x1, xn2 = _outproj(merged, xf, wo_bf, g_mlp, l, tm=512)
        xf = _mlp(xn2, x1, wu_bf, wd_bf, tm=512, tf=2048)

    return xf.reshape(batch, seq, d)
```

```python
import functools

import jax
import jax.numpy as jnp
from jax import lax
from jax.experimental import pallas as pl
from jax.experimental.pallas import tpu as pltpu

F32 = jnp.float32
BF16 = jnp.bfloat16

D_MODEL = 2048
WIDTH = 512
N_BRANCH = 4
LRU_BLOCK_DIM = 64
LRU_CONV = 4
LRU_C = 8.0
POOL_GROUPS = 4
POOL_WINDOWS = (2, 4, 8, 16)
SCONV_K = 3
HEADS = 8
HEAD_DIM = 64
D_FF = 4 * D_MODEL
EPS = 1e-6

LANES = 128
SUBLANES = 8
VMEM_LIMIT = 56 * 1024 * 1024

N_MIX = 2 * WIDTH + 3 * WIDTH + 3 * WIDTH + HEADS
N_MIX_PAD = -(-N_MIX // LANES) * LANES
COL_LRU = 0
COL_POOL = WIDTH
COL_SCONV = 2 * WIDTH
COL_QKV = 5 * WIDTH
COL_FORGET = 8 * WIDTH


def _params(semantics):
    return pltpu.CompilerParams(dimension_semantics=semantics, vmem_limit_bytes=VMEM_LIMIT)


def _dot(a, b):
    return jnp.dot(a, b, preferred_element_type=F32)


def _dot_nt(a, b):
    return lax.dot_general(a, b, (((1,), (1,)), ((), ())), preferred_element_type=F32)


def _shift_rows(x, k, fill):
    n, c = x.shape
    if k >= n:
        return jnp.full((n, c), fill, x.dtype)
    if k % SUBLANES == 0:
        return jnp.concatenate([jnp.full((k, c), fill, x.dtype), x[:n - k]], axis=0)
    r = pltpu.roll(x, k, 0)
    row = lax.broadcasted_iota(jnp.int32, (SUBLANES, c), 0)
    top = jnp.where(row < k, jnp.asarray(fill, x.dtype), r[:SUBLANES])
    return jnp.concatenate([top, r[SUBLANES:]], axis=0)


SCAN_PAD = 4


def _scan_scratch_rows(seq):
    return seq + SUBLANES * SCAN_PAD


def _chunk_rows(tau, stride):
    return pl.ds(tau, SUBLANES, stride=stride)


def _linear_scan_rows(a, b, a_scr, b_scr):
    n, c = a.shape
    length = n // SUBLANES
    stride = length + SCAN_PAD
    for ch in range(SUBLANES):
        a_scr[ch * stride:ch * stride + length, :] = a[ch * length:(ch + 1) * length]
        b_scr[ch * stride:ch * stride + length, :] = b[ch * length:(ch + 1) * length]

    h = b_scr[_chunk_rows(0, stride), :]
    prod = a_scr[_chunk_rows(0, stride), :]
    for tau in range(1, length):
        rows = _chunk_rows(tau, stride)
        at = a_scr[rows, :]
        h = at * h + b_scr[rows, :]
        prod = at * prod
        b_scr[rows, :] = h
        a_scr[rows, :] = prod

    out = []
    state = jnp.zeros((1, c), F32)
    for ch in range(SUBLANES):
        lo = ch * stride
        out.append(b_scr[lo:lo + length, :] + a_scr[lo:lo + length, :] * state)
        state = h[ch:ch + 1] + prod[ch:ch + 1] * state
    return jnp.concatenate(out, axis=0)


def _cumsum_rows(x, scr):
    n, c = x.shape
    length = n // SUBLANES
    stride = length + SCAN_PAD
    for ch in range(SUBLANES):
        scr[ch * stride:ch * stride + length, :] = x[ch * length:(ch + 1) * length]
    h = scr[_chunk_rows(0, stride), :]
    for tau in range(1, length):
        rows = _chunk_rows(tau, stride)
        h = h + scr[rows, :]
        scr[rows, :] = h
    out = []
    state = jnp.zeros((1, c), F32)
    for ch in range(SUBLANES):
        lo = ch * stride
        out.append(scr[lo:lo + length, :] + state)
        state = state + h[ch:ch + 1]
    return jnp.concatenate(out, axis=0)


def _softplus(x):
    return jnp.maximum(x, 0.0) + jnp.log1p(jnp.exp(-jnp.abs(x)))


def _wcast_kernel(w_ref, o_ref):
    o_ref[...] = w_ref[...].astype(o_ref.dtype)


MIX_CAST_ROWS = N_MIX_PAD // 8


def _cast_mix_weights(w_in_t, layer):
    d = w_in_t.shape[2]
    return pl.pallas_call(
        _wcast_kernel,
        grid=(N_MIX_PAD // MIX_CAST_ROWS,),
        in_specs=[pl.BlockSpec((None, MIX_CAST_ROWS, d), lambda j: (layer, j, 0))],
        out_specs=pl.BlockSpec((MIX_CAST_ROWS, d), lambda j: (j, 0)),
        out_shape=jax.ShapeDtypeStruct((N_MIX_PAD, d), BF16),
        compiler_params=_params(("parallel",)),
        name="cast_mix_weights",
    )(w_in_t)


def _inproj_kernel(x_ref, g_ref, w_ref, proj_ref, xn_ref):
    x = x_ref[...]
    ms = jnp.mean(x * x, axis=-1, keepdims=True)
    xn = ((x * lax.rsqrt(ms + EPS)) * g_ref[...]).astype(xn_ref.dtype)
    xn_ref[...] = xn
    proj_ref[...] = _dot_nt(xn, w_ref[...])


def _inproj(x, g, w_mix, layer, *, tm):
    t, d = x.shape
    n = w_mix.shape[0]
    return pl.pallas_call(
        _inproj_kernel,
        grid=(t // tm,),
        in_specs=[
            pl.BlockSpec((tm, d), lambda i: (i, 0)),
            pl.BlockSpec((None, 1, d), lambda i: (layer, 0, 0)),
            pl.BlockSpec((n, d), lambda i: (0, 0), pipeline_mode=pl.Buffered(1)),
        ],
        out_specs=[
            pl.BlockSpec((tm, n), lambda i: (i, 0)),
            pl.BlockSpec((tm, d), lambda i: (i, 0)),
        ],
        out_shape=[
            jax.ShapeDtypeStruct((t, n), F32),
            jax.ShapeDtypeStruct((t, d), BF16),
        ],
        compiler_params=_params(("parallel",)),
        name="inproj",
    )(x, g, w_mix)


def _lru_slab(x, cw, cb, wr, br, wi, bi, lam, a_scr, b_scr):
    u = cw[LRU_CONV - 1:LRU_CONV] * x + cb
    for k in range(1, LRU_CONV):
        u = u + cw[LRU_CONV - 1 - k:LRU_CONV - k] * _shift_rows(x, k, 0.0)
    ub = u.astype(BF16)
    r = jax.nn.sigmoid(_dot(ub, wr) + br)
    gi = jax.nn.sigmoid(_dot(ub, wi) + bi)
    log_a = (-LRU_C * r) * _softplus(-lam)
    a = jnp.exp(log_a)
    inp = jnp.sqrt(-jnp.tanh(log_a) * (a * a + 1.0)) * (gi * u)
    return _linear_scan_rows(a, inp, a_scr, b_scr)


def _pool_slab(x, group, w, scale):
    w2 = x + _shift_rows(x, 1, 0.0)
    w4 = w2 + _shift_rows(w2, 2, 0.0)
    w8 = w4 + _shift_rows(w4, 4, 0.0)
    w16 = w8 + _shift_rows(w8, 8, 0.0)
    wsum = jnp.where(group == 0, w2, jnp.where(group == 1, w4, jnp.where(group == 2, w8, w16)))
    win = jnp.left_shift(2, group)
    inv_win = jnp.where(group == 0, 0.5, jnp.where(group == 1, 0.25, jnp.where(group == 2, 0.125, 0.0625)))
    head = max(POOL_WINDOWS)
    t = lax.broadcasted_iota(jnp.int32, (head, x.shape[1]), 0)
    count = jnp.minimum(t + 1, win).astype(F32)
    pooled = jnp.concatenate([wsum[:head] / count - x[:head], wsum[head:] * inv_win - x[head:]], axis=0)
    return _dot(pooled.astype(BF16), w) * scale


def _sconv_slab(gate_b, gate_c, xc, w):
    z = gate_c * xc
    acc = w[SCONV_K - 1:SCONV_K] * z
    for k in range(1, SCONV_K):
        acc = acc + w[SCONV_K - 1 - k:SCONV_K - k] * _shift_rows(z, k, 0.0)
    return gate_b * acc


def _mixers_kernel(xa_ref, xp_ref, gb_ref, gc_ref, xc_ref, f_ref,
                   cw_ref, cb_ref, wr_ref, br_ref, wi_ref, bi_ref, lam_ref,
                   pw_ref, ps_ref, sw_ref, fb_ref,
                   ya_ref, yb_ref, yc_ref, c_ref, a_scr, b_scr):
    slab = pl.program_id(1)
    ya_ref[...] = _lru_slab(xa_ref[...], cw_ref[...], cb_ref[...], wr_ref[0], br_ref[...],
                            wi_ref[0], bi_ref[...], lam_ref[...], a_scr, b_scr).astype(ya_ref.dtype)
    yb_ref[...] = _pool_slab(xp_ref[...], slab, pw_ref[0], ps_ref[...]).astype(yb_ref.dtype)
    yc_ref[...] = _sconv_slab(gb_ref[...], gc_ref[...], xc_ref[...], sw_ref[...]).astype(yc_ref.dtype)

    @pl.when(slab == 0)
    def _():
        z = f_ref[...] + fb_ref[...]
        log_f = jnp.minimum(z, 0.0) - jnp.log1p(jnp.exp(-jnp.abs(z)))
        head_lane = lax.broadcasted_iota(jnp.int32, (1, LANES), 1) < HEADS
        c_ref[...] = jnp.where(head_lane, _cumsum_rows(log_f, b_scr), 0.0)


def _mixers(proj, cw, cb, wr_bd, br, wi_bd, bi, lam, pool_w, pool_scale, sconv_w, fb, layer, *, batch, seq):
    nc = WIDTH // LANES

    def col(c0, p=0):
        return pl.BlockSpec((seq, LANES), lambda b, c: (b, (c0 + p * WIDTH) // LANES + c))

    vec = pl.BlockSpec((None, 1, LANES), lambda b, c: (layer, 0, c))
    mat = pl.BlockSpec((None, 1, LANES, LANES), lambda b, c: (layer, c, 0, 0))
    y_spec = pl.BlockSpec((seq, LANES), lambda b, c: (b, c))
    y_shape = jax.ShapeDtypeStruct((batch * seq, WIDTH), BF16)
    return pl.pallas_call(
        _mixers_kernel,
        grid=(batch, nc),
        in_specs=[col(COL_LRU), col(COL_POOL), col(COL_SCONV, 0), col(COL_SCONV, 1), col(COL_SCONV, 2),
                  pl.BlockSpec((seq, LANES), lambda b, c: (b, COL_FORGET // LANES)),
                  pl.BlockSpec((None, LRU_CONV, LANES), lambda b, c: (layer, 0, c)),
                  vec, mat, vec, mat, vec, vec,
                  mat, vec,
                  pl.BlockSpec((None, SCONV_K, LANES), lambda b, c: (layer, 0, c)),
                  pl.BlockSpec((None, 1, LANES), lambda b, c: (layer, 0, 0))],
        out_specs=[y_spec, y_spec, y_spec, pl.BlockSpec((seq, LANES), lambda b, c: (b, 0))],
        out_shape=[y_shape, y_shape, y_shape, jax.ShapeDtypeStruct((batch * seq, LANES), F32)],
        scratch_shapes=[pltpu.VMEM((_scan_scratch_rows(seq), LANES), F32)] * 2,
        compiler_params=_params(("parallel", "arbitrary")),
        name="mixers",
    )(proj, proj, proj, proj, proj, proj, cw, cb, wr_bd, br, wi_bd, bi, lam, pool_w, pool_scale, sconv_w, fb)


ATTN_TQ = 512
LOG2E = 1.4426950408889634
ATTN_ROWS = 64
ATTN_BIAS_LANES = 12


def _attn_kernel(q_ref, k_ref, v_ref, c_ref, sel_ref, qg_ref, kg_ref, hm_ref,
                 wu_ref, wd_ref, wo_ref, o_ref, wub_ref, wdb_ref, wob_ref,
                 q_scr, k_scr, vt_scr, s_scr, p_scr):
    wub_ref[...] = wu_ref[0].astype(BF16)
    wdb_ref[...] = wd_ref[0].astype(BF16)
    wob_ref[...] = wo_ref[0].astype(BF16)

    seq = q_ref.shape[0]
    hm = hm_ref[...]

    def head_norm(x, g):
        sq = x * x
        hi = sq.astype(BF16)
        lo = (sq - hi.astype(F32)).astype(BF16)
        ms = _dot(hi, hm) + _dot(lo, hm)
        return (x * lax.rsqrt(ms + EPS)) * g

    tq = ATTN_TQ
    nblk = seq // tq
    qg, kg = qg_ref[...] * (HEAD_DIM ** -0.5 * LOG2E), kg_ref[...]
    lane = lax.broadcasted_iota(jnp.int32, (1, LANES), 1)
    first = lane < HEAD_DIM

    def lane_in(lo, hi):
        return (lane >= lo) & (lane < hi)

    sel_hi, sel_mid, sel_lo = sel_ref[0, 0], sel_ref[0, 1], sel_ref[0, 2]

    for j in range(nblk):
        rows = slice(j * tq, (j + 1) * tq)
        c = c_ref[rows, :] * LOG2E
        hi = c.astype(BF16)
        mid = (c - hi.astype(F32)).astype(BF16)
        lo = ((c - hi.astype(F32)) - mid.astype(F32)).astype(BF16)
        pieces = _dot(hi, sel_hi) + _dot(mid, sel_mid) + _dot(lo, sel_lo)
        k_extra = jnp.where(lane_in(0, 6), -pieces, jnp.where(lane_in(6, ATTN_BIAS_LANES), 1.0, 0.0))
        qa_extra = jnp.where(lane_in(0, 3), 1.0, jnp.where(lane_in(6, 9), pieces, 0.0))
        qb_extra = jnp.where(lane_in(3, 6), 1.0, jnp.where(lane_in(9, ATTN_BIAS_LANES), pieces, 0.0))
        qn = head_norm(q_ref[rows, :], qg)
        q_scr[j, :tq, :LANES] = jnp.where(first, qn, 0.0).astype(BF16)
        q_scr[j, :tq, LANES:] = qa_extra.astype(BF16)
        q_scr[j, tq:, :LANES] = jnp.where(first, 0.0, qn).astype(BF16)
        q_scr[j, tq:, LANES:] = qb_extra.astype(BF16)
        k_scr[rows, :LANES] = head_norm(k_ref[rows, :], kg).astype(BF16)
        k_scr[rows, LANES:] = k_extra.astype(BF16)
        vt_scr[:, rows] = v_ref[rows, :].T.astype(BF16)

    rb = ATTN_ROWS
    nq = 2 * tq
    key_pos = lax.broadcasted_iota(jnp.int32, (tq, nq), 0)
    lane_q = lax.broadcasted_iota(jnp.int32, (tq, nq), 1)
    causal = key_pos <= jnp.where(lane_q < tq, lane_q, lane_q - tq)

    def fold8(x, op):
        out = x[:SUBLANES]
        for g in range(1, x.shape[0] // SUBLANES):
            out = op(out, x[g * SUBLANES:(g + 1) * SUBLANES])
        return out

    for i in range(nblk):
        q0, kend = i * tq, (i + 1) * tq
        s_blk = s_scr.at[i % 2]
        p_blk = p_scr.at[i % 2]

        m8 = None
        for kb in range(i + 1):
            rows = slice(kb * tq, (kb + 1) * tq)
            s = _dot_nt(k_scr[rows, :], q_scr[i])
            if kb == i:
                s = jnp.where(causal, s, -jnp.inf)
            s_blk[rows, :] = s
            part = fold8(s, jnp.maximum)
            m8 = part if m8 is None else jnp.maximum(m8, part)
        m = jnp.max(m8, axis=0, keepdims=True)

        l8 = jnp.zeros((SUBLANES, nq), F32)
        for c in range(kend // rb):
            p = jnp.exp2(s_blk[c * rb:(c + 1) * rb, :] - m)
            p_blk[c * rb:(c + 1) * rb, :] = p.astype(BF16)
            l8 = l8 + fold8(p, jnp.add)
        l = jnp.sum(l8, axis=0, keepdims=True)

        o_t = _dot(vt_scr[:, :kend], p_blk[:kend, :]) / l
        o = o_t.T
        o_ref[q0:kend, :] = jnp.where(first, o[:tq], o[tq:]).astype(o_ref.dtype)


def _bias_lane_selectors():
    pair = jnp.arange(HEADS // 2)[:, None, None, None]
    p = jnp.arange(3)[None, :, None, None]
    row = jnp.arange(LANES)[None, None, :, None]
    col = jnp.arange(LANES)[None, None, None, :]
    head_a = (row == 2 * pair) & ((col == p) | (col == 6 + p))
    head_b = (row == 2 * pair + 1) & ((col == 3 + p) | (col == 9 + p))
    return (head_a | head_b).astype(BF16)


def _attention(proj, c, qg, kg, hm, wu, wd, wo, layer, *, batch, seq):
    npair = HEADS // 2
    col0 = COL_QKV // LANES
    steps = batch * npair
    _, d, f = wu.shape
    ru, rd, ro = d // steps, f // steps, d // steps

    def part(p):
        return pl.BlockSpec((seq, LANES), lambda b, h: (b, col0 + p * npair + h))

    def slab(rows, cols):
        return (pl.BlockSpec((1, rows, cols), lambda b, h: (layer, b * npair + h, 0)),
                pl.BlockSpec((rows, cols), lambda b, h: (b * npair + h, 0)))

    (wu_in, wu_out), (wd_in, wd_out), (wo_in, wo_out) = slab(ru, f), slab(rd, d), slab(ro, d)
    vec = pl.BlockSpec((None, 1, LANES), lambda b, h: (layer, 0, 0))
    return pl.pallas_call(
        _attn_kernel,
        grid=(batch, npair),
        in_specs=[part(0), part(1), part(2),
                  pl.BlockSpec((seq, LANES), lambda b, h: (b, 0)),
                  pl.BlockSpec((1, 3, LANES, LANES), lambda b, h: (h, 0, 0, 0)),
                  vec, vec, pl.BlockSpec((LANES, LANES), lambda b, h: (0, 0)),
                  wu_in, wd_in, wo_in],
        out_specs=[pl.BlockSpec((seq, LANES), lambda b, h: (b, h)), wu_out, wd_out, wo_out],
        out_shape=[jax.ShapeDtypeStruct((batch * seq, WIDTH), BF16),
                   jax.ShapeDtypeStruct((d, f), BF16),
                   jax.ShapeDtypeStruct((f, d), BF16),
                   jax.ShapeDtypeStruct((d, d), BF16)],
        scratch_shapes=[pltpu.VMEM((seq // ATTN_TQ, 2 * ATTN_TQ, 2 * LANES), BF16),
                        pltpu.VMEM((seq, 2 * LANES), BF16),
                        pltpu.VMEM((LANES, seq), BF16),
                        pltpu.VMEM((2, seq, 2 * ATTN_TQ), F32),
                        pltpu.VMEM((2, seq, 2 * ATTN_TQ), BF16)],
        compiler_params=_params(("parallel", "parallel")),
        name="fox_attention",
    )(proj, proj, proj, c, _bias_lane_selectors(), qg, kg, hm, wu, wd, wo)


def _merge_kernel(xn_ref, ya_ref, yb_ref, yc_ref, yd_ref, wg0_ref, wg1_ref, wg2_ref, wg3_ref, wb_ref, o_ref):
    xn = xn_ref[...]
    acc = None
    for k, (y_ref, wg_ref) in enumerate(((ya_ref, wg0_ref), (yb_ref, wg1_ref),
                                         (yc_ref, wg2_ref), (yd_ref, wg3_ref))):
        gate = jax.nn.sigmoid(_dot_nt(xn, wg_ref[...].astype(BF16)))
        term = gate * _dot(y_ref[...], wb_ref[0, k].astype(BF16))
        acc = term if acc is None else acc + term
    o_ref[...] = acc.astype(o_ref.dtype)


def _merge(xn, ys, w_in_t, wb, layer, *, tm, tn):
    t, d = xn.shape
    nblk = d // tn

    def wg_spec(k):
        return pl.BlockSpec((pl.Squeezed(), pl.Element(tn), pl.Element(d)),
                            lambda i, j: (layer, pl.multiple_of(N_MIX + k * d + j * tn, SUBLANES), 0))

    y_spec = pl.BlockSpec((tm, WIDTH), lambda i, j: (i, 0))
    return pl.pallas_call(
        _merge_kernel,
        grid=(t // tm, nblk),
        in_specs=[pl.BlockSpec((tm, d), lambda i, j: (i, 0)), y_spec, y_spec, y_spec, y_spec,
                  wg_spec(0), wg_spec(1), wg_spec(2), wg_spec(3),
                  pl.BlockSpec((1, N_BRANCH, WIDTH, tn), lambda i, j: (layer, 0, 0, j))],
        out_specs=pl.BlockSpec((tm, tn), lambda i, j: (i, j)),
        out_shape=jax.ShapeDtypeStruct((t, d), BF16),
        compiler_params=_params(("parallel", "arbitrary")),
        name="merge",
    )(xn, *ys, w_in_t, w_in_t, w_in_t, w_in_t, wb)


def _outproj_kernel(m_ref, x_ref, w_ref, g_ref, wn_ref, x1_ref, xn_ref, wnb_ref):
    wnb_ref[...] = wn_ref[...].astype(BF16)

    x1 = x_ref[...] + _dot(m_ref[...], w_ref[...])
    x1_ref[...] = x1
    ms = jnp.mean(x1 * x1, axis=-1, keepdims=True)
    xn_ref[...] = ((x1 * lax.rsqrt(ms + EPS)) * g_ref[...]).astype(xn_ref.dtype)


def _outproj(merged, x, w_out, g, w_in_t, layer, *, tm):
    t, d = x.shape
    n = t // tm
    next_layer = min(layer + 1, w_in_t.shape[0] - 1)
    n_slabs = max(k for k in (8, 4, 2, 1) if n % k == 0)
    rows, per_slab = N_MIX_PAD // n_slabs, n // n_slabs
    row = pl.BlockSpec((tm, d), lambda i: (i, 0))
    return pl.pallas_call(
        _outproj_kernel,
        grid=(n,),
        in_specs=[row, row,
                  pl.BlockSpec((d, d), lambda i: (0, 0), pipeline_mode=pl.Buffered(1)),
                  pl.BlockSpec((None, 1, d), lambda i: (layer, 0, 0)),
                  pl.BlockSpec((None, rows, d), lambda i: (next_layer, i // per_slab, 0))],
        out_specs=[row, row, pl.BlockSpec((rows, d), lambda i: (i // per_slab, 0))],
        out_shape=[jax.ShapeDtypeStruct((t, d), F32), jax.ShapeDtypeStruct((t, d), BF16),
                   jax.ShapeDtypeStruct((N_MIX_PAD, d), BF16)],
        compiler_params=_params(("arbitrary",)),
        name="outproj",
    )(merged, x, w_out, g, w_in_t)


def _mlp_kernel(xn_ref, x_ref, wu_ref, wd_ref, o_ref):
    @pl.when(pl.program_id(1) == 0)
    def _():
        o_ref[...] = x_ref[...]

    h = jnp.maximum(_dot(xn_ref[...], wu_ref[...]), 0.0)
    o_ref[...] += _dot((h * h).astype(BF16), wd_ref[...])


def _mlp(xn, x, wu, wd, *, tm, tf):
    t, d = x.shape
    f = wu.shape[1]
    row = pl.BlockSpec((tm, d), lambda i, j: (i, 0))
    return pl.pallas_call(
        _mlp_kernel,
        grid=(t // tm, f // tf),
        in_specs=[row, row,
                  pl.BlockSpec((d, tf), lambda i, j: (0, j)),
                  pl.BlockSpec((tf, d), lambda i, j: (j, 0))],
        out_specs=row,
        out_shape=jax.ShapeDtypeStruct((t, d), F32),
        compiler_params=_params(("parallel", "arbitrary")),
        name="mlp",
    )(xn, x, wu, wd)


def _block_diag_pairs(w):
    depth, n2, bd, _ = w.shape
    per = LANES // bd
    w5 = w.reshape(depth, n2 // per, per, bd, bd)
    eye = jnp.eye(per, dtype=w.dtype)
    return jnp.einsum('ljaik,ab->ljaibk', w5, eye).reshape(depth, n2 // per, LANES, LANES)


def _rows(v):
    return v.reshape(v.shape[0], 1, -1).astype(F32)


def kernel(x, norm_mix_g, w_in, lru_conv_w, lru_conv_b, lru_wr, lru_br, lru_wi, lru_bi, lru_lambda, pool_w, pool_scale, sconv_w, q_norm_g, k_norm_g, forget_b, w_branch, w_out, norm_mlp_g, w_mlp_up, w_mlp_down):
    batch, seq, d = x.shape
    assert d == D_MODEL and seq % ATTN_TQ == 0
    depth = w_in.shape[0]
    t = batch * seq
    xf = x.reshape(t, d)

    lane = jnp.arange(LANES)
    same_head = ((lane[:, None] // HEAD_DIM == lane[None, :] // HEAD_DIM) / HEAD_DIM).astype(BF16)

    w_in_t = jnp.swapaxes(w_in, 1, 2)
    w_mix = _cast_mix_weights(w_in_t, 0)

    g_mix, g_mlp = _rows(norm_mix_g), _rows(norm_mlp_g)
    lru_cb, lru_rb, lru_ib, lru_lam = _rows(lru_conv_b), _rows(lru_br), _rows(lru_bi), _rows(lru_lambda)
    lru_wr_bd = _block_diag_pairs(lru_wr).astype(BF16)
    lru_wi_bd = _block_diag_pairs(lru_wi).astype(BF16)
    pool_wb, pool_sc = pool_w.astype(BF16), _rows(pool_scale)
    fb = jnp.pad(_rows(forget_b), ((0, 0), (0, 0), (0, LANES - HEADS)))
    qg = jnp.tile(_rows(q_norm_g), (1, 1, LANES // HEAD_DIM))
    kg = jnp.tile(_rows(k_norm_g), (1, 1, LANES // HEAD_DIM))

    for l in range(depth):
        proj, xn = _inproj(xf, g_mix, w_mix, l, tm=512)
        y_a, y_b, y_c, c = _mixers(proj, lru_conv_w, lru_cb, lru_wr_bd, lru_rb, lru_wi_bd, lru_ib, lru_lam,
                                   pool_wb, pool_sc, sconv_w, fb, l, batch=batch, seq=seq)
        y_d, wu_bf, wd_bf, wo_bf = _attention(proj, c, qg, kg, same_head, w_mlp_up, w_mlp_down, w_out, l,
                                              batch=batch, seq=seq)
        merged = _merge(xn, (y_a, y_b, y_c, y_d), w_in_t, w_branch, l, tm=1024, tn=256)
        x1, xn2, w_mix = _outproj(merged, xf, wo_bf, g_mlp, w_in_t, l, tm=512)
        xf = _mlp(xn2, x1, wu_bf, wd_bf, tm=512, tf=2048)

    return xf.reshape(batch, seq, d)
```

```python
import functools

import jax
import jax.numpy as jnp
from jax import lax
from jax.experimental import pallas as pl
from jax.experimental.pallas import tpu as pltpu

F32 = jnp.float32
BF16 = jnp.bfloat16

D_MODEL = 2048
WIDTH = 512
N_BRANCH = 4
LRU_BLOCK_DIM = 64
LRU_CONV = 4
LRU_C = 8.0
POOL_GROUPS = 4
POOL_WINDOWS = (2, 4, 8, 16)
SCONV_K = 3
HEADS = 8
HEAD_DIM = 64
D_FF = 4 * D_MODEL
EPS = 1e-6

LANES = 128
SUBLANES = 8
VMEM_LIMIT = 56 * 1024 * 1024

N_MIX = 2 * WIDTH + 3 * WIDTH + 3 * WIDTH + HEADS
N_MIX_PAD = -(-N_MIX // LANES) * LANES
COL_LRU = 0
COL_POOL = WIDTH
COL_SCONV = 2 * WIDTH
COL_QKV = 5 * WIDTH
COL_FORGET = 8 * WIDTH


def _params(semantics):
    return pltpu.CompilerParams(dimension_semantics=semantics, vmem_limit_bytes=VMEM_LIMIT)


def _dot(a, b):
    return jnp.dot(a, b, preferred_element_type=F32)


def _dot_nt(a, b):
    return lax.dot_general(a, b, (((1,), (1,)), ((), ())), preferred_element_type=F32)


def _shift_rows(x, k, fill):
    n, c = x.shape
    if k >= n:
        return jnp.full((n, c), fill, x.dtype)
    if k % SUBLANES == 0:
        return jnp.concatenate([jnp.full((k, c), fill, x.dtype), x[:n - k]], axis=0)
    r = pltpu.roll(x, k, 0)
    row = lax.broadcasted_iota(jnp.int32, (SUBLANES, c), 0)
    top = jnp.where(row < k, jnp.asarray(fill, x.dtype), r[:SUBLANES])
    return jnp.concatenate([top, r[SUBLANES:]], axis=0)


SCAN_PAD = 4


def _scan_scratch_rows(seq):
    return seq + SUBLANES * SCAN_PAD


def _chunk_rows(tau, stride):
    return pl.ds(tau, SUBLANES, stride=stride)


def _linear_scan_rows(a, b, a_scr, b_scr):
    n, c = a.shape
    length = n // SUBLANES
    stride = length + SCAN_PAD
    for ch in range(SUBLANES):
        a_scr[ch * stride:ch * stride + length, :] = a[ch * length:(ch + 1) * length]
        b_scr[ch * stride:ch * stride + length, :] = b[ch * length:(ch + 1) * length]

    h = b_scr[_chunk_rows(0, stride), :]
    prod = a_scr[_chunk_rows(0, stride), :]
    for tau in range(1, length):
        rows = _chunk_rows(tau, stride)
        at = a_scr[rows, :]
        h = at * h + b_scr[rows, :]
        prod = at * prod
        b_scr[rows, :] = h
        a_scr[rows, :] = prod

    out = []
    state = jnp.zeros((1, c), F32)
    for ch in range(SUBLANES):
        lo = ch * stride
        out.append(b_scr[lo:lo + length, :] + a_scr[lo:lo + length, :] * state)
        state = h[ch:ch + 1] + prod[ch:ch + 1] * state
    return jnp.concatenate(out, axis=0)


def _cumsum_rows(x, scr):
    n, c = x.shape
    length = n // SUBLANES
    stride = length + SCAN_PAD
    for ch in range(SUBLANES):
        scr[ch * stride:ch * stride + length, :] = x[ch * length:(ch + 1) * length]
    h = scr[_chunk_rows(0, stride), :]
    for tau in range(1, length):
        rows = _chunk_rows(tau, stride)
        h = h + scr[rows, :]
        scr[rows, :] = h
    out = []
    state = jnp.zeros((1, c), F32)
    for ch in range(SUBLANES):
        lo = ch * stride
        out.append(scr[lo:lo + length, :] + state)
        state = state + h[ch:ch + 1]
    return jnp.concatenate(out, axis=0)


def _softplus(x):
    return jnp.maximum(x, 0.0) + jnp.log1p(jnp.exp(-jnp.abs(x)))


def _wcast_kernel(w_ref, o_ref):
    o_ref[...] = w_ref[...].astype(o_ref.dtype)


def _cast_mix_weights(w_in_t, *, rows):
    depth, _, d = w_in_t.shape
    spec = pl.BlockSpec((1, rows, d), lambda l, j: (l, j, 0))
    return pl.pallas_call(
        _wcast_kernel,
        grid=(depth, N_MIX_PAD // rows),
        in_specs=[spec],
        out_specs=spec,
        out_shape=jax.ShapeDtypeStruct((depth, N_MIX_PAD, d), BF16),
        compiler_params=_params(("parallel", "parallel")),
        name="cast_mix_weights",
    )(w_in_t)


def _inproj_kernel(x_ref, g_ref, w_ref, proj_ref, xn_ref):
    x = x_ref[...]
    ms = jnp.mean(x * x, axis=-1, keepdims=True)
    xn = ((x * lax.rsqrt(ms + EPS)) * g_ref[...]).astype(xn_ref.dtype)
    xn_ref[...] = xn
    proj_ref[...] = _dot_nt(xn, w_ref[0])


def _inproj(x, g, w_mix, layer, *, tm):
    t, d = x.shape
    n = w_mix.shape[1]
    return pl.pallas_call(
        _inproj_kernel,
        grid=(t // tm,),
        in_specs=[
            pl.BlockSpec((tm, d), lambda i: (i, 0)),
            pl.BlockSpec((None, 1, d), lambda i: (layer, 0, 0)),
            pl.BlockSpec((1, n, d), lambda i: (layer, 0, 0), pipeline_mode=pl.Buffered(1)),
        ],
        out_specs=[
            pl.BlockSpec((tm, n), lambda i: (i, 0)),
            pl.BlockSpec((tm, d), lambda i: (i, 0)),
        ],
        out_shape=[
            jax.ShapeDtypeStruct((t, n), F32),
            jax.ShapeDtypeStruct((t, d), BF16),
        ],
        compiler_params=_params(("parallel",)),
        name="inproj",
    )(x, g, w_mix)


def _lru_slab(x, cw, cb, wr, br, wi, bi, lam, a_scr, b_scr):
    u = cw[LRU_CONV - 1:LRU_CONV] * x + cb
    for k in range(1, LRU_CONV):
        u = u + cw[LRU_CONV - 1 - k:LRU_CONV - k] * _shift_rows(x, k, 0.0)
    ub = u.astype(BF16)
    r = jax.nn.sigmoid(_dot(ub, wr) + br)
    gi = jax.nn.sigmoid(_dot(ub, wi) + bi)
    log_a = (-LRU_C * r) * _softplus(-lam)
    a = jnp.exp(log_a)
    inp = jnp.sqrt(-jnp.tanh(log_a) * (a * a + 1.0)) * (gi * u)
    return _linear_scan_rows(a, inp, a_scr, b_scr)


def _pool_slab(x, group, w, scale):
    w2 = x + _shift_rows(x, 1, 0.0)
    w4 = w2 + _shift_rows(w2, 2, 0.0)
    w8 = w4 + _shift_rows(w4, 4, 0.0)
    w16 = w8 + _shift_rows(w8, 8, 0.0)
    wsum = jnp.where(group == 0, w2, jnp.where(group == 1, w4, jnp.where(group == 2, w8, w16)))
    win = jnp.left_shift(2, group)
    inv_win = jnp.where(group == 0, 0.5, jnp.where(group == 1, 0.25, jnp.where(group == 2, 0.125, 0.0625)))
    head = max(POOL_WINDOWS)
    t = lax.broadcasted_iota(jnp.int32, (head, x.shape[1]), 0)
    count = jnp.minimum(t + 1, win).astype(F32)
    pooled = jnp.concatenate([wsum[:head] / count - x[:head], wsum[head:] * inv_win - x[head:]], axis=0)
    return _dot(pooled.astype(BF16), w) * scale


def _sconv_slab(gate_b, gate_c, xc, w):
    z = gate_c * xc
    acc = w[SCONV_K - 1:SCONV_K] * z
    for k in range(1, SCONV_K):
        acc = acc + w[SCONV_K - 1 - k:SCONV_K - k] * _shift_rows(z, k, 0.0)
    return gate_b * acc


def _mixers_kernel(xa_ref, xp_ref, gb_ref, gc_ref, xc_ref, f_ref,
                   cw_ref, cb_ref, wr_ref, br_ref, wi_ref, bi_ref, lam_ref,
                   pw_ref, ps_ref, sw_ref, fb_ref,
                   ya_ref, yb_ref, yc_ref, c_ref, a_scr, b_scr):
    slab = pl.program_id(1)
    ya_ref[...] = _lru_slab(xa_ref[...], cw_ref[...], cb_ref[...], wr_ref[0], br_ref[...],
                            wi_ref[0], bi_ref[...], lam_ref[...], a_scr, b_scr).astype(ya_ref.dtype)
    yb_ref[...] = _pool_slab(xp_ref[...], slab, pw_ref[0], ps_ref[...]).astype(yb_ref.dtype)
    yc_ref[...] = _sconv_slab(gb_ref[...], gc_ref[...], xc_ref[...], sw_ref[...]).astype(yc_ref.dtype)

    @pl.when(slab == 0)
    def _():
        z = f_ref[...] + fb_ref[...]
        log_f = jnp.minimum(z, 0.0) - jnp.log1p(jnp.exp(-jnp.abs(z)))
        head_lane = lax.broadcasted_iota(jnp.int32, (1, LANES), 1) < HEADS
        c_ref[...] = jnp.where(head_lane, _cumsum_rows(log_f, b_scr), 0.0)


def _mixers(proj, cw, cb, wr_bd, br, wi_bd, bi, lam, pool_w, pool_scale, sconv_w, fb, layer, *, batch, seq):
    nc = WIDTH // LANES

    def col(c0, p=0):
        return pl.BlockSpec((seq, LANES), lambda b, c: (b, (c0 + p * WIDTH) // LANES + c))

    vec = pl.BlockSpec((None, 1, LANES), lambda b, c: (layer, 0, c))
    mat = pl.BlockSpec((None, 1, LANES, LANES), lambda b, c: (layer, c, 0, 0))
    y_spec = pl.BlockSpec((seq, LANES), lambda b, c: (b, c))
    y_shape = jax.ShapeDtypeStruct((batch * seq, WIDTH), BF16)
    return pl.pallas_call(
        _mixers_kernel,
        grid=(batch, nc),
        in_specs=[col(COL_LRU), col(COL_POOL), col(COL_SCONV, 0), col(COL_SCONV, 1), col(COL_SCONV, 2),
                  pl.BlockSpec((seq, LANES), lambda b, c: (b, COL_FORGET // LANES)),
                  pl.BlockSpec((None, LRU_CONV, LANES), lambda b, c: (layer, 0, c)),
                  vec, mat, vec, mat, vec, vec,
                  mat, vec,
                  pl.BlockSpec((None, SCONV_K, LANES), lambda b, c: (layer, 0, c)),
                  pl.BlockSpec((None, 1, LANES), lambda b, c: (layer, 0, 0))],
        out_specs=[y_spec, y_spec, y_spec, pl.BlockSpec((seq, LANES), lambda b, c: (b, 0))],
        out_shape=[y_shape, y_shape, y_shape, jax.ShapeDtypeStruct((batch * seq, LANES), F32)],
        scratch_shapes=[pltpu.VMEM((_scan_scratch_rows(seq), LANES), F32)] * 2,
        compiler_params=_params(("parallel", "arbitrary")),
        name="mixers",
    )(proj, proj, proj, proj, proj, proj, cw, cb, wr_bd, br, wi_bd, bi, lam, pool_w, pool_scale, sconv_w, fb)


ATTN_TQ = 512
LOG2E = 1.4426950408889634
ATTN_ROWS = 64
ATTN_BIAS_LANES = 12


def _attn_kernel(q_ref, k_ref, v_ref, c_ref, sel_ref, qg_ref, kg_ref, hm_ref,
                 wu_ref, wd_ref, wo_ref, o_ref, wub_ref, wdb_ref, wob_ref,
                 q_scr, k_scr, vt_scr, s_scr, p_scr):
    wub_ref[...] = wu_ref[0].astype(BF16)
    wdb_ref[...] = wd_ref[0].astype(BF16)
    wob_ref[...] = wo_ref[0].astype(BF16)

    seq = q_ref.shape[0]
    hm = hm_ref[...]

    def head_norm(x, g):
        sq = x * x
        hi = sq.astype(BF16)
        lo = (sq - hi.astype(F32)).astype(BF16)
        ms = _dot(hi, hm) + _dot(lo, hm)
        return (x * lax.rsqrt(ms + EPS)) * g

    tq = ATTN_TQ
    nblk = seq // tq
    qg, kg = qg_ref[...] * (HEAD_DIM ** -0.5 * LOG2E), kg_ref[...]
    lane = lax.broadcasted_iota(jnp.int32, (1, LANES), 1)
    first = lane < HEAD_DIM

    def lane_in(lo, hi):
        return (lane >= lo) & (lane < hi)

    sel_hi, sel_mid, sel_lo = sel_ref[0, 0], sel_ref[0, 1], sel_ref[0, 2]

    for j in range(nblk):
        rows = slice(j * tq, (j + 1) * tq)
        c = c_ref[rows, :] * LOG2E
        hi = c.astype(BF16)
        mid = (c - hi.astype(F32)).astype(BF16)
        lo = ((c - hi.astype(F32)) - mid.astype(F32)).astype(BF16)
        pieces = _dot(hi, sel_hi) + _dot(mid, sel_mid) + _dot(lo, sel_lo)
        k_extra = jnp.where(lane_in(0, 6), -pieces, jnp.where(lane_in(6, ATTN_BIAS_LANES), 1.0, 0.0))
        qa_extra = jnp.where(lane_in(0, 3), 1.0, jnp.where(lane_in(6, 9), pieces, 0.0))
        qb_extra = jnp.where(lane_in(3, 6), 1.0, jnp.where(lane_in(9, ATTN_BIAS_LANES), pieces, 0.0))
        qn = head_norm(q_ref[rows, :], qg)
        q_scr[j, :tq, :LANES] = jnp.where(first, qn, 0.0).astype(BF16)
        q_scr[j, :tq, LANES:] = qa_extra.astype(BF16)
        q_scr[j, tq:, :LANES] = jnp.where(first, 0.0, qn).astype(BF16)
        q_scr[j, tq:, LANES:] = qb_extra.astype(BF16)
        k_scr[rows, :LANES] = head_norm(k_ref[rows, :], kg).astype(BF16)
        k_scr[rows, LANES:] = k_extra.astype(BF16)
        vt_scr[:, rows] = v_ref[rows, :].T.astype(BF16)

    rb = ATTN_ROWS
    nq = 2 * tq
    key_pos = lax.broadcasted_iota(jnp.int32, (tq, nq), 0)
    lane_q = lax.broadcasted_iota(jnp.int32, (tq, nq), 1)
    causal = key_pos <= jnp.where(lane_q < tq, lane_q, lane_q - tq)

    def fold8(x, op):
        out = x[:SUBLANES]
        for g in range(1, x.shape[0] // SUBLANES):
            out = op(out, x[g * SUBLANES:(g + 1) * SUBLANES])
        return out

    for i in range(nblk):
        q0, kend = i * tq, (i + 1) * tq
        s_blk = s_scr.at[i % 2]
        p_blk = p_scr.at[i % 2]

        m8 = None
        for kb in range(i + 1):
            rows = slice(kb * tq, (kb + 1) * tq)
            s = _dot_nt(k_scr[rows, :], q_scr[i])
            if kb == i:
                s = jnp.where(causal, s, -jnp.inf)
            s_blk[rows, :] = s
            part = fold8(s, jnp.maximum)
            m8 = part if m8 is None else jnp.maximum(m8, part)
        m = jnp.max(m8, axis=0, keepdims=True)

        l8 = jnp.zeros((SUBLANES, nq), F32)
        for c in range(kend // rb):
            p = jnp.exp2(s_blk[c * rb:(c + 1) * rb, :] - m)
            p_blk[c * rb:(c + 1) * rb, :] = p.astype(BF16)
            l8 = l8 + fold8(p, jnp.add)
        l = jnp.sum(l8, axis=0, keepdims=True)

        o_t = _dot(vt_scr[:, :kend], p_blk[:kend, :]) / l
        o = o_t.T
        o_ref[q0:kend, :] = jnp.where(first, o[:tq], o[tq:]).astype(o_ref.dtype)


def _bias_lane_selectors():
    pair = jnp.arange(HEADS // 2)[:, None, None, None]
    p = jnp.arange(3)[None, :, None, None]
    row = jnp.arange(LANES)[None, None, :, None]
    col = jnp.arange(LANES)[None, None, None, :]
    head_a = (row == 2 * pair) & ((col == p) | (col == 6 + p))
    head_b = (row == 2 * pair + 1) & ((col == 3 + p) | (col == 9 + p))
    return (head_a | head_b).astype(BF16)


def _attention(proj, c, qg, kg, hm, wu, wd, wo, layer, *, batch, seq):
    npair = HEADS // 2
    col0 = COL_QKV // LANES
    steps = batch * npair
    _, d, f = wu.shape
    ru, rd, ro = d // steps, f // steps, d // steps

    def part(p):
        return pl.BlockSpec((seq, LANES), lambda b, h: (b, col0 + p * npair + h))

    def slab(rows, cols):
        return (pl.BlockSpec((1, rows, cols), lambda b, h: (layer, b * npair + h, 0)),
                pl.BlockSpec((rows, cols), lambda b, h: (b * npair + h, 0)))

    (wu_in, wu_out), (wd_in, wd_out), (wo_in, wo_out) = slab(ru, f), slab(rd, d), slab(ro, d)
    vec = pl.BlockSpec((None, 1, LANES), lambda b, h: (layer, 0, 0))
    return pl.pallas_call(
        _attn_kernel,
        grid=(batch, npair),
        in_specs=[part(0), part(1), part(2),
                  pl.BlockSpec((seq, LANES), lambda b, h: (b, 0)),
                  pl.BlockSpec((1, 3, LANES, LANES), lambda b, h: (h, 0, 0, 0)),
                  vec, vec, pl.BlockSpec((LANES, LANES), lambda b, h: (0, 0)),
                  wu_in, wd_in, wo_in],
        out_specs=[pl.BlockSpec((seq, LANES), lambda b, h: (b, h)), wu_out, wd_out, wo_out],
        out_shape=[jax.ShapeDtypeStruct((batch * seq, WIDTH), BF16),
                   jax.ShapeDtypeStruct((d, f), BF16),
                   jax.ShapeDtypeStruct((f, d), BF16),
                   jax.ShapeDtypeStruct((d, d), BF16)],
        scratch_shapes=[pltpu.VMEM((seq // ATTN_TQ, 2 * ATTN_TQ, 2 * LANES), BF16),
                        pltpu.VMEM((seq, 2 * LANES), BF16),
                        pltpu.VMEM((LANES, seq), BF16),
                        pltpu.VMEM((2, seq, 2 * ATTN_TQ), F32),
                        pltpu.VMEM((2, seq, 2 * ATTN_TQ), BF16)],
        compiler_params=_params(("parallel", "parallel")),
        name="fox_attention",
    )(proj, proj, proj, c, _bias_lane_selectors(), qg, kg, hm, wu, wd, wo)


def _merge_kernel(xn_ref, ya_ref, yb_ref, yc_ref, yd_ref, wg0_ref, wg1_ref, wg2_ref, wg3_ref, wb_ref, o_ref):
    xn = xn_ref[...]
    acc = None
    for k, (y_ref, wg_ref) in enumerate(((ya_ref, wg0_ref), (yb_ref, wg1_ref),
                                         (yc_ref, wg2_ref), (yd_ref, wg3_ref))):
        gate = jax.nn.sigmoid(_dot_nt(xn, wg_ref[...].astype(BF16)))
        term = gate * _dot(y_ref[...], wb_ref[0, k].astype(BF16))
        acc = term if acc is None else acc + term
    o_ref[...] = acc.astype(o_ref.dtype)


def _merge(xn, ys, w_in_t, wb, layer, *, tm, tn):
    t, d = xn.shape
    nblk = d // tn

    def wg_spec(k):
        return pl.BlockSpec((pl.Squeezed(), pl.Element(tn), pl.Element(d)),
                            lambda i, j: (layer, pl.multiple_of(N_MIX + k * d + j * tn, SUBLANES), 0))

    y_spec = pl.BlockSpec((tm, WIDTH), lambda i, j: (i, 0))
    return pl.pallas_call(
        _merge_kernel,
        grid=(t // tm, nblk),
        in_specs=[pl.BlockSpec((tm, d), lambda i, j: (i, 0)), y_spec, y_spec, y_spec, y_spec,
                  wg_spec(0), wg_spec(1), wg_spec(2), wg_spec(3),
                  pl.BlockSpec((1, N_BRANCH, WIDTH, tn), lambda i, j: (layer, 0, 0, j))],
        out_specs=pl.BlockSpec((tm, tn), lambda i, j: (i, j)),
        out_shape=jax.ShapeDtypeStruct((t, d), BF16),
        compiler_params=_params(("parallel", "arbitrary")),
        name="merge",
    )(xn, *ys, w_in_t, w_in_t, w_in_t, w_in_t, wb)


def _outproj_kernel(m_ref, x_ref, w_ref, g_ref, x1_ref, xn_ref):
    x1 = x_ref[...] + _dot(m_ref[...], w_ref[...])
    x1_ref[...] = x1
    ms = jnp.mean(x1 * x1, axis=-1, keepdims=True)
    xn_ref[...] = ((x1 * lax.rsqrt(ms + EPS)) * g_ref[...]).astype(xn_ref.dtype)


def _outproj(merged, x, w_out, g, layer, *, tm):
    t, d = x.shape
    row = pl.BlockSpec((tm, d), lambda i: (i, 0))
    return pl.pallas_call(
        _outproj_kernel,
        grid=(t // tm,),
        in_specs=[row, row,
                  pl.BlockSpec((d, d), lambda i: (0, 0), pipeline_mode=pl.Buffered(1)),
                  pl.BlockSpec((None, 1, d), lambda i: (layer, 0, 0))],
        out_specs=[row, row],
        out_shape=[jax.ShapeDtypeStruct((t, d), F32), jax.ShapeDtypeStruct((t, d), BF16)],
        compiler_params=_params(("parallel",)),
        name="outproj",
    )(merged, x, w_out, g)


def _mlp_kernel(xn_ref, x_ref, wu_ref, wd_ref, o_ref):
    @pl.when(pl.program_id(1) == 0)
    def _():
        o_ref[...] = x_ref[...]

    h = jnp.maximum(_dot(xn_ref[...], wu_ref[...]), 0.0)
    o_ref[...] += _dot((h * h).astype(BF16), wd_ref[...])


def _mlp(xn, x, wu, wd, *, tm, tf):
    t, d = x.shape
    f = wu.shape[1]
    row = pl.BlockSpec((tm, d), lambda i, j: (i, 0))
    return pl.pallas_call(
        _mlp_kernel,
        grid=(t // tm, f // tf),
        in_specs=[row, row,
                  pl.BlockSpec((d, tf), lambda i, j: (0, j)),
                  pl.BlockSpec((tf, d), lambda i, j: (j, 0))],
        out_specs=row,
        out_shape=jax.ShapeDtypeStruct((t, d), F32),
        compiler_params=_params(("parallel", "arbitrary")),
        name="mlp",
    )(xn, x, wu, wd)


def _block_diag_pairs(w):
    depth, n2, bd, _ = w.shape
    per = LANES // bd
    w5 = w.reshape(depth, n2 // per, per, bd, bd)
    eye = jnp.eye(per, dtype=w.dtype)
    return jnp.einsum('ljaik,ab->ljaibk', w5, eye).reshape(depth, n2 // per, LANES, LANES)


def _rows(v):
    return v.reshape(v.shape[0], 1, -1).astype(F32)


def kernel(x, norm_mix_g, w_in, lru_conv_w, lru_conv_b, lru_wr, lru_br, lru_wi, lru_bi, lru_lambda, pool_w, pool_scale, sconv_w, q_norm_g, k_norm_g, forget_b, w_branch, w_out, norm_mlp_g, w_mlp_up, w_mlp_down):
    batch, seq, d = x.shape
    assert d == D_MODEL and seq % ATTN_TQ == 0
    depth = w_in.shape[0]
    t = batch * seq
    xf = x.reshape(t, d)

    lane = jnp.arange(LANES)
    same_head = ((lane[:, None] // HEAD_DIM == lane[None, :] // HEAD_DIM) / HEAD_DIM).astype(BF16)

    w_in_t = jnp.swapaxes(w_in, 1, 2)
    w_mix = _cast_mix_weights(w_in_t, rows=N_MIX_PAD // 8)

    g_mix, g_mlp = _rows(norm_mix_g), _rows(norm_mlp_g)
    lru_cb, lru_rb, lru_ib, lru_lam = _rows(lru_conv_b), _rows(lru_br), _rows(lru_bi), _rows(lru_lambda)
    lru_wr_bd = _block_diag_pairs(lru_wr).astype(BF16)
    lru_wi_bd = _block_diag_pairs(lru_wi).astype(BF16)
    pool_wb, pool_sc = pool_w.astype(BF16), _rows(pool_scale)
    fb = jnp.pad(_rows(forget_b), ((0, 0), (0, 0), (0, LANES - HEADS)))
    qg = jnp.tile(_rows(q_norm_g), (1, 1, LANES // HEAD_DIM))
    kg = jnp.tile(_rows(k_norm_g), (1, 1, LANES // HEAD_DIM))

    for l in range(depth):
        proj, xn = _inproj(xf, g_mix, w_mix, l, tm=512)
        y_a, y_b, y_c, c = _mixers(proj, lru_conv_w, lru_cb, lru_wr_bd, lru_rb, lru_wi_bd, lru_ib, lru_lam,
                                   pool_wb, pool_sc, sconv_w, fb, l, batch=batch, seq=seq)
        y_d, wu_bf, wd_bf, wo_bf = _attention(proj, c, qg, kg, same_head, w_mlp_up, w_mlp_down, w_out, l,
                                              batch=batch, seq=seq)
        merged = _merge(xn, (y_a, y_b, y_c, y_d), w_in_t, w_branch, l, tm=1024, tn=256)
        x1, xn2 = _outproj(merged, xf, wo_bf, g_mlp, l, tm=512)
        xf = _mlp(xn2, x1, wu_bf, wd_bf, tm=512, tf=2048)

    return xf.reshape(batch, seq, d)
```

```python
import jax
import jax.numpy as jnp
from jax import lax
from jax.experimental import pallas as pl
from jax.experimental.pallas import tpu as pltpu

F32 = jnp.float32
BF16 = jnp.bfloat16

D_MODEL = 2048
WIDTH = 512
N_BRANCH = 4
LRU_CONV = 4
LRU_C = 8.0
POOL_WINDOWS = (2, 4, 8, 16)
SCONV_K = 3
HEADS = 8
HEAD_DIM = 64
D_FF = 4 * D_MODEL
EPS = 1e-6

LANES = 128
SUBLANES = 8
VMEM_LIMIT = 56 * 1024 * 1024

N_MIX = 2 * WIDTH + 3 * WIDTH + 3 * WIDTH + HEADS
N_MIX_PAD = -(-N_MIX // LANES) * LANES
COL_LRU = 0
COL_POOL = WIDTH
COL_SCONV = 2 * WIDTH
COL_QKV = 5 * WIDTH
COL_FORGET = 8 * WIDTH


def _params(semantics):
    return pltpu.CompilerParams(dimension_semantics=semantics, vmem_limit_bytes=VMEM_LIMIT)


def _dot(a, b):
    return jnp.dot(a, b, preferred_element_type=F32)


def _dot_nt(a, b):
    return lax.dot_general(a, b, (((1,), (1,)), ((), ())), preferred_element_type=F32)


def _shift_rows(x, k, fill):
    n, c = x.shape
    if k >= n:
        return jnp.full((n, c), fill, x.dtype)
    if k % SUBLANES == 0:
        return jnp.concatenate([jnp.full((k, c), fill, x.dtype), x[:n - k]], axis=0)
    r = pltpu.roll(x, k, 0)
    row = lax.broadcasted_iota(jnp.int32, (SUBLANES, c), 0)
    top = jnp.where(row < k, jnp.asarray(fill, x.dtype), r[:SUBLANES])
    return jnp.concatenate([top, r[SUBLANES:]], axis=0)


SCAN_PAD = 4


def _scan_scratch_rows(seq):
    return seq + SUBLANES * SCAN_PAD


def _chunk_rows(tau, stride):
    return pl.ds(tau, SUBLANES, stride=stride)


def _linear_scan_rows(a, b, a_scr, b_scr):
    n, c = a.shape
    length = n // SUBLANES
    stride = length + SCAN_PAD
    for ch in range(SUBLANES):
        a_scr[ch * stride:ch * stride + length, :] = a[ch * length:(ch + 1) * length]
        b_scr[ch * stride:ch * stride + length, :] = b[ch * length:(ch + 1) * length]

    h = b_scr[_chunk_rows(0, stride), :]
    prod = a_scr[_chunk_rows(0, stride), :]
    for tau in range(1, length):
        rows = _chunk_rows(tau, stride)
        at = a_scr[rows, :]
        h = at * h + b_scr[rows, :]
        prod = at * prod
        b_scr[rows, :] = h
        a_scr[rows, :] = prod

    out = []
    state = jnp.zeros((1, c), F32)
    for ch in range(SUBLANES):
        lo = ch * stride
        out.append(b_scr[lo:lo + length, :] + a_scr[lo:lo + length, :] * state)
        state = h[ch:ch + 1] + prod[ch:ch + 1] * state
    return jnp.concatenate(out, axis=0)


def _cumsum_rows(x, scr):
    n, c = x.shape
    length = n // SUBLANES
    stride = length + SCAN_PAD
    for ch in range(SUBLANES):
        scr[ch * stride:ch * stride + length, :] = x[ch * length:(ch + 1) * length]
    h = scr[_chunk_rows(0, stride), :]
    for tau in range(1, length):
        rows = _chunk_rows(tau, stride)
        h = h + scr[rows, :]
        scr[rows, :] = h
    out = []
    state = jnp.zeros((1, c), F32)
    for ch in range(SUBLANES):
        lo = ch * stride
        out.append(scr[lo:lo + length, :] + state)
        state = state + h[ch:ch + 1]
    return jnp.concatenate(out, axis=0)


def _softplus(x):
    return jnp.maximum(x, 0.0) + jnp.log1p(jnp.exp(-jnp.abs(x)))


def _wcast_kernel(w_ref, o_ref):
    o_ref[...] = w_ref[...].astype(o_ref.dtype)


def _cast_mix_weights(w_in_t, *, rows):
    depth, _, d = w_in_t.shape
    spec = pl.BlockSpec((1, rows, d), lambda l, j: (l, j, 0))
    return pl.pallas_call(
        _wcast_kernel,
        grid=(depth, N_MIX_PAD // rows),
        in_specs=[spec],
        out_specs=spec,
        out_shape=jax.ShapeDtypeStruct((depth, N_MIX_PAD, d), BF16),
        compiler_params=_params(("parallel", "parallel")),
        name="cast_mix_weights",
    )(w_in_t)


def _inproj_kernel(x_ref, g_ref, w_ref, proj_ref, xn_ref):
    x = x_ref[...]
    ms = jnp.mean(x * x, axis=-1, keepdims=True)
    xn = ((x * lax.rsqrt(ms + EPS)) * g_ref[...]).astype(xn_ref.dtype)
    xn_ref[...] = xn
    proj_ref[...] = _dot_nt(xn, w_ref[0])


def _inproj(x, g, w_mix, layer, *, tm):
    t, d = x.shape
    n = w_mix.shape[1]
    return pl.pallas_call(
        _inproj_kernel,
        grid=(t // tm,),
        in_specs=[
            pl.BlockSpec((tm, d), lambda i: (i, 0)),
            pl.BlockSpec((None, 1, d), lambda i: (layer, 0, 0)),
            pl.BlockSpec((1, n, d), lambda i: (layer, 0, 0), pipeline_mode=pl.Buffered(1)),
        ],
        out_specs=[
            pl.BlockSpec((tm, n), lambda i: (i, 0)),
            pl.BlockSpec((tm, d), lambda i: (i, 0)),
        ],
        out_shape=[
            jax.ShapeDtypeStruct((t, n), F32),
            jax.ShapeDtypeStruct((t, d), BF16),
        ],
        compiler_params=_params(("parallel",)),
        name="inproj",
    )(x, g, w_mix)


def _lru_slab(x, cw, cb, wr, br, wi, bi, lam, a_scr, b_scr):
    u = cw[LRU_CONV - 1:LRU_CONV] * x + cb
    for k in range(1, LRU_CONV):
        u = u + cw[LRU_CONV - 1 - k:LRU_CONV - k] * _shift_rows(x, k, 0.0)
    ub = u.astype(BF16)
    r = jax.nn.sigmoid(_dot(ub, wr) + br)
    gi = jax.nn.sigmoid(_dot(ub, wi) + bi)
    log_a = (-LRU_C * r) * _softplus(-lam)
    a = jnp.exp(log_a)
    inp = jnp.sqrt(-jnp.tanh(log_a) * (a * a + 1.0)) * (gi * u)
    return _linear_scan_rows(a, inp, a_scr, b_scr)


def _pool_slab(x, group, w, scale):
    w2 = x + _shift_rows(x, 1, 0.0)
    w4 = w2 + _shift_rows(w2, 2, 0.0)
    w8 = w4 + _shift_rows(w4, 4, 0.0)
    w16 = w8 + _shift_rows(w8, 8, 0.0)
    wsum = jnp.where(group == 0, w2, jnp.where(group == 1, w4, jnp.where(group == 2, w8, w16)))
    win = jnp.left_shift(2, group)
    inv_win = jnp.where(group == 0, 0.5, jnp.where(group == 1, 0.25, jnp.where(group == 2, 0.125, 0.0625)))
    head = max(POOL_WINDOWS)
    t = lax.broadcasted_iota(jnp.int32, (head, x.shape[1]), 0)
    count = jnp.minimum(t + 1, win).astype(F32)
    pooled = jnp.concatenate([wsum[:head] / count - x[:head], wsum[head:] * inv_win - x[head:]], axis=0)
    return _dot(pooled.astype(BF16), w) * scale


def _sconv_slab(gate_b, gate_c, xc, w):
    z = gate_c * xc
    acc = w[SCONV_K - 1:SCONV_K] * z
    for k in range(1, SCONV_K):
        acc = acc + w[SCONV_K - 1 - k:SCONV_K - k] * _shift_rows(z, k, 0.0)
    return gate_b * acc


def _mixers_kernel(xa_ref, xp_ref, gb_ref, gc_ref, xc_ref, f_ref,
                   cw_ref, cb_ref, wr_ref, br_ref, wi_ref, bi_ref, lam_ref,
                   pw_ref, ps_ref, sw_ref, fb_ref,
                   ya_ref, yb_ref, yc_ref, c_ref, a_scr, b_scr):
    slab = pl.program_id(1)
    ya_ref[...] = _lru_slab(xa_ref[...], cw_ref[...], cb_ref[...], wr_ref[0], br_ref[...],
                            wi_ref[0], bi_ref[...], lam_ref[...], a_scr, b_scr).astype(ya_ref.dtype)
    yb_ref[...] = _pool_slab(xp_ref[...], slab, pw_ref[0], ps_ref[...]).astype(yb_ref.dtype)
    yc_ref[...] = _sconv_slab(gb_ref[...], gc_ref[...], xc_ref[...], sw_ref[...]).astype(yc_ref.dtype)

    @pl.when(slab == 0)
    def _():
        z = f_ref[...] + fb_ref[...]
        log_f = jnp.minimum(z, 0.0) - jnp.log1p(jnp.exp(-jnp.abs(z)))
        head_lane = lax.broadcasted_iota(jnp.int32, (1, LANES), 1) < HEADS
        c_ref[...] = jnp.where(head_lane, _cumsum_rows(log_f, b_scr), 0.0)


def _mixers(proj, cw, cb, wr_bd, br, wi_bd, bi, lam, pool_w, pool_scale, sconv_w, fb, layer, *, batch, seq):
    nc = WIDTH // LANES

    def col(c0, p=0):
        return pl.BlockSpec((seq, LANES), lambda b, c: (b, (c0 + p * WIDTH) // LANES + c))

    vec = pl.BlockSpec((None, 1, LANES), lambda b, c: (layer, 0, c))
    mat = pl.BlockSpec((None, 1, LANES, LANES), lambda b, c: (layer, c, 0, 0))
    y_spec = pl.BlockSpec((seq, LANES), lambda b, c: (b, c))
    y_shape = jax.ShapeDtypeStruct((batch * seq, WIDTH), BF16)
    return pl.pallas_call(
        _mixers_kernel,
        grid=(batch, nc),
        in_specs=[col(COL_LRU), col(COL_POOL), col(COL_SCONV, 0), col(COL_SCONV, 1), col(COL_SCONV, 2),
                  pl.BlockSpec((seq, LANES), lambda b, c: (b, COL_FORGET // LANES)),
                  pl.BlockSpec((None, LRU_CONV, LANES), lambda b, c: (layer, 0, c)),
                  vec, mat, vec, mat, vec, vec,
                  mat, vec,
                  pl.BlockSpec((None, SCONV_K, LANES), lambda b, c: (layer, 0, c)),
                  pl.BlockSpec((None, 1, LANES), lambda b, c: (layer, 0, 0))],
        out_specs=[y_spec, y_spec, y_spec, pl.BlockSpec((seq, LANES), lambda b, c: (b, 0))],
        out_shape=[y_shape, y_shape, y_shape, jax.ShapeDtypeStruct((batch * seq, LANES), F32)],
        scratch_shapes=[pltpu.VMEM((_scan_scratch_rows(seq), LANES), F32)] * 2,
        compiler_params=_params(("parallel", "arbitrary")),
        name="mixers",
    )(proj, proj, proj, proj, proj, proj, cw, cb, wr_bd, br, wi_bd, bi, lam, pool_w, pool_scale, sconv_w, fb)


ATTN_TQ = 512
LOG2E = 1.4426950408889634
ATTN_ROWS = 64
ATTN_BIAS_LANES = 12


def _attn_kernel(q_ref, k_ref, v_ref, c_ref, sel_ref, qg_ref, kg_ref, hm_ref,
                 wu_ref, wd_ref, wo_ref, o_ref, wub_ref, wdb_ref, wob_ref,
                 q_scr, k_scr, vt_scr, s_scr, p_scr):
    wub_ref[...] = wu_ref[0].astype(BF16)
    wdb_ref[...] = wd_ref[0].astype(BF16)
    wob_ref[...] = wo_ref[0].astype(BF16)

    seq = q_ref.shape[0]
    hm = hm_ref[...]

    def head_norm(x, g):
        sq = x * x
        hi = sq.astype(BF16)
        lo = (sq - hi.astype(F32)).astype(BF16)
        ms = _dot(hi, hm) + _dot(lo, hm)
        return (x * lax.rsqrt(ms + EPS)) * g

    tq = ATTN_TQ
    nblk = seq // tq
    qg, kg = qg_ref[...] * (HEAD_DIM ** -0.5 * LOG2E), kg_ref[...]
    lane = lax.broadcasted_iota(jnp.int32, (1, LANES), 1)
    first = lane < HEAD_DIM

    def lane_in(lo, hi):
        return (lane >= lo) & (lane < hi)

    sel_hi, sel_mid, sel_lo = sel_ref[0, 0], sel_ref[0, 1], sel_ref[0, 2]

    for j in range(nblk):
        rows = slice(j * tq, (j + 1) * tq)
        c = c_ref[rows, :] * LOG2E
        hi = c.astype(BF16)
        mid = (c - hi.astype(F32)).astype(BF16)
        lo = ((c - hi.astype(F32)) - mid.astype(F32)).astype(BF16)
        pieces = _dot(hi, sel_hi) + _dot(mid, sel_mid) + _dot(lo, sel_lo)
        k_extra = jnp.where(lane_in(0, 6), -pieces, jnp.where(lane_in(6, ATTN_BIAS_LANES), 1.0, 0.0))
        qa_extra = jnp.where(lane_in(0, 3), 1.0, jnp.where(lane_in(6, 9), pieces, 0.0))
        qb_extra = jnp.where(lane_in(3, 6), 1.0, jnp.where(lane_in(9, ATTN_BIAS_LANES), pieces, 0.0))
        qn = head_norm(q_ref[rows, :], qg)
        q_scr[j, :tq, :LANES] = jnp.where(first, qn, 0.0).astype(BF16)
        q_scr[j, :tq, LANES:] = qa_extra.astype(BF16)
        q_scr[j, tq:, :LANES] = jnp.where(first, 0.0, qn).astype(BF16)
        q_scr[j, tq:, LANES:] = qb_extra.astype(BF16)
        k_scr[rows, :LANES] = head_norm(k_ref[rows, :], kg).astype(BF16)
        k_scr[rows, LANES:] = k_extra.astype(BF16)
        vt_scr[:, rows] = v_ref[rows, :].T.astype(BF16)

    rb = ATTN_ROWS
    nq = 2 * tq
    key_pos = lax.broadcasted_iota(jnp.int32, (tq, nq), 0)
    lane_q = lax.broadcasted_iota(jnp.int32, (tq, nq), 1)
    causal = key_pos <= jnp.where(lane_q < tq, lane_q, lane_q - tq)

    def fold8(x, op):
        out = x[:SUBLANES]
        for g in range(1, x.shape[0] // SUBLANES):
            out = op(out, x[g * SUBLANES:(g + 1) * SUBLANES])
        return out

    for i in range(nblk):
        q0, kend = i * tq, (i + 1) * tq
        s_blk = s_scr.at[i % 2]
        p_blk = p_scr.at[i % 2]

        m8 = None
        for kb in range(i + 1):
            rows = slice(kb * tq, (kb + 1) * tq)
            s = _dot_nt(k_scr[rows, :], q_scr[i])
            if kb == i:
                s = jnp.where(causal, s, -jnp.inf)
            s_blk[rows, :] = s
            part = fold8(s, jnp.maximum)
            m8 = part if m8 is None else jnp.maximum(m8, part)
        m = jnp.max(m8, axis=0, keepdims=True)

        l8 = jnp.zeros((SUBLANES, nq), F32)
        for c in range(kend // rb):
            p = jnp.exp2(s_blk[c * rb:(c + 1) * rb, :] - m)
            p_blk[c * rb:(c + 1) * rb, :] = p.astype(BF16)
            l8 = l8 + fold8(p, jnp.add)
        l = jnp.sum(l8, axis=0, keepdims=True)

        o_t = _dot(vt_scr[:, :kend], p_blk[:kend, :]) / l
        o = o_t.T
        o_ref[q0:kend, :] = jnp.where(first, o[:tq], o[tq:]).astype(o_ref.dtype)


def _bias_lane_selectors():
    pair = jnp.arange(HEADS // 2)[:, None, None, None]
    p = jnp.arange(3)[None, :, None, None]
    row = jnp.arange(LANES)[None, None, :, None]
    col = jnp.arange(LANES)[None, None, None, :]
    head_a = (row == 2 * pair) & ((col == p) | (col == 6 + p))
    head_b = (row == 2 * pair + 1) & ((col == 3 + p) | (col == 9 + p))
    return (head_a | head_b).astype(BF16)


def _attention(proj, c, qg, kg, hm, wu, wd, wo, layer, *, batch, seq):
    npair = HEADS // 2
    col0 = COL_QKV // LANES
    steps = batch * npair
    _, d, f = wu.shape
    ru, rd, ro = d // steps, f // steps, d // steps

    def part(p):
        return pl.BlockSpec((seq, LANES), lambda b, h: (b, col0 + p * npair + h))

    def slab(rows, cols):
        return (pl.BlockSpec((1, rows, cols), lambda b, h: (layer, b * npair + h, 0)),
                pl.BlockSpec((rows, cols), lambda b, h: (b * npair + h, 0)))

    (wu_in, wu_out), (wd_in, wd_out), (wo_in, wo_out) = slab(ru, f), slab(rd, d), slab(ro, d)
    vec = pl.BlockSpec((None, 1, LANES), lambda b, h: (layer, 0, 0))
    return pl.pallas_call(
        _attn_kernel,
        grid=(batch, npair),
        in_specs=[part(0), part(1), part(2),
                  pl.BlockSpec((seq, LANES), lambda b, h: (b, 0)),
                  pl.BlockSpec((1, 3, LANES, LANES), lambda b, h: (h, 0, 0, 0)),
                  vec, vec, pl.BlockSpec((LANES, LANES), lambda b, h: (0, 0)),
                  wu_in, wd_in, wo_in],
        out_specs=[pl.BlockSpec((seq, LANES), lambda b, h: (b, h)), wu_out, wd_out, wo_out],
        out_shape=[jax.ShapeDtypeStruct((batch * seq, WIDTH), BF16),
                   jax.ShapeDtypeStruct((d, f), BF16),
                   jax.ShapeDtypeStruct((f, d), BF16),
                   jax.ShapeDtypeStruct((d, d), BF16)],
        scratch_shapes=[pltpu.VMEM((seq // ATTN_TQ, 2 * ATTN_TQ, 2 * LANES), BF16),
                        pltpu.VMEM((seq, 2 * LANES), BF16),
                        pltpu.VMEM((LANES, seq), BF16),
                        pltpu.VMEM((2, seq, 2 * ATTN_TQ), F32),
                        pltpu.VMEM((2, seq, 2 * ATTN_TQ), BF16)],
        compiler_params=_params(("parallel", "parallel")),
        name="fox_attention",
    )(proj, proj, proj, c, _bias_lane_selectors(), qg, kg, hm, wu, wd, wo)


def _merge_kernel(xn_ref, ya_ref, yb_ref, yc_ref, yd_ref, wg0_ref, wg1_ref, wg2_ref, wg3_ref, wb_ref, o_ref):
    xn = xn_ref[...]
    acc = None
    for k, (y_ref, wg_ref) in enumerate(((ya_ref, wg0_ref), (yb_ref, wg1_ref),
                                         (yc_ref, wg2_ref), (yd_ref, wg3_ref))):
        gate = jax.nn.sigmoid(_dot_nt(xn, wg_ref[...].astype(BF16)))
        term = gate * _dot(y_ref[...], wb_ref[0, k].astype(BF16))
        acc = term if acc is None else acc + term
    o_ref[...] = acc.astype(o_ref.dtype)


def _merge(xn, ys, w_in_t, wb, layer, *, tm, tn):
    t, d = xn.shape
    nblk = d // tn

    def wg_spec(k):
        return pl.BlockSpec((pl.Squeezed(), pl.Element(tn), pl.Element(d)),
                            lambda i, j: (layer, pl.multiple_of(N_MIX + k * d + j * tn, SUBLANES), 0))

    y_spec = pl.BlockSpec((tm, WIDTH), lambda i, j: (i, 0))
    return pl.pallas_call(
        _merge_kernel,
        grid=(t // tm, nblk),
        in_specs=[pl.BlockSpec((tm, d), lambda i, j: (i, 0)), y_spec, y_spec, y_spec, y_spec,
                  wg_spec(0), wg_spec(1), wg_spec(2), wg_spec(3),
                  pl.BlockSpec((1, N_BRANCH, WIDTH, tn), lambda i, j: (layer, 0, 0, j))],
        out_specs=pl.BlockSpec((tm, tn), lambda i, j: (i, j)),
        out_shape=jax.ShapeDtypeStruct((t, d), BF16),
        compiler_params=_params(("parallel", "arbitrary")),
        name="merge",
    )(xn, *ys, w_in_t, w_in_t, w_in_t, w_in_t, wb)


def _outproj_kernel(m_ref, x_ref, w_ref, g_ref, x1_ref, xn_ref):
    x1 = x_ref[...] + _dot(m_ref[...], w_ref[...])
    x1_ref[...] = x1
    ms = jnp.mean(x1 * x1, axis=-1, keepdims=True)
    xn_ref[...] = ((x1 * lax.rsqrt(ms + EPS)) * g_ref[...]).astype(xn_ref.dtype)


def _outproj(merged, x, w_out, g, layer, *, tm):
    t, d = x.shape
    row = pl.BlockSpec((tm, d), lambda i: (i, 0))
    return pl.pallas_call(
        _outproj_kernel,
        grid=(t // tm,),
        in_specs=[row, row,
                  pl.BlockSpec((d, d), lambda i: (0, 0), pipeline_mode=pl.Buffered(1)),
                  pl.BlockSpec((None, 1, d), lambda i: (layer, 0, 0))],
        out_specs=[row, row],
        out_shape=[jax.ShapeDtypeStruct((t, d), F32), jax.ShapeDtypeStruct((t, d), BF16)],
        compiler_params=_params(("parallel",)),
        name="outproj",
    )(merged, x, w_out, g)


def _mlp_kernel(xn_ref, x_ref, wu_ref, wd_ref, o_ref):
    @pl.when(pl.program_id(1) == 0)
    def _():
        o_ref[...] = x_ref[...]

    h = jnp.maximum(_dot(xn_ref[...], wu_ref[...]), 0.0)
    o_ref[...] += _dot((h * h).astype(BF16), wd_ref[...])


def _mlp(xn, x, wu, wd, *, tm, tf):
    t, d = x.shape
    f = wu.shape[1]
    row = pl.BlockSpec((tm, d), lambda i, j: (i, 0))
    return pl.pallas_call(
        _mlp_kernel,
        grid=(t // tm, f // tf),
        in_specs=[row, row,
                  pl.BlockSpec((d, tf), lambda i, j: (0, j)),
                  pl.BlockSpec((tf, d), lambda i, j: (j, 0))],
        out_specs=row,
        out_shape=jax.ShapeDtypeStruct((t, d), F32),
        compiler_params=_params(("parallel", "arbitrary")),
        name="mlp",
    )(xn, x, wu, wd)


def _block_diag_pairs(w):
    depth, n2, bd, _ = w.shape
    per = LANES // bd
    w5 = w.reshape(depth, n2 // per, per, bd, bd)
    eye = jnp.eye(per, dtype=w.dtype)
    return jnp.einsum('ljaik,ab->ljaibk', w5, eye).reshape(depth, n2 // per, LANES, LANES)


def _rows(v):
    return v.reshape(v.shape[0], 1, -1).astype(F32)


def kernel(x, norm_mix_g, w_in, lru_conv_w, lru_conv_b, lru_wr, lru_br, lru_wi, lru_bi, lru_lambda, pool_w, pool_scale, sconv_w, q_norm_g, k_norm_g, forget_b, w_branch, w_out, norm_mlp_g, w_mlp_up, w_mlp_down):
    batch, seq, d = x.shape
    assert d == D_MODEL and seq % ATTN_TQ == 0
    depth = w_in.shape[0]
    t = batch * seq
    xf = x.reshape(t, d)

    lane = jnp.arange(LANES)
    same_head = ((lane[:, None] // HEAD_DIM == lane[None, :] // HEAD_DIM) / HEAD_DIM).astype(BF16)

    w_in_t = jnp.swapaxes(w_in, 1, 2)
    w_mix = _cast_mix_weights(w_in_t, rows=N_MIX_PAD // 8)

    g_mix, g_mlp = _rows(norm_mix_g), _rows(norm_mlp_g)
    lru_cb, lru_rb, lru_ib, lru_lam = _rows(lru_conv_b), _rows(lru_br), _rows(lru_bi), _rows(lru_lambda)
    lru_wr_bd = _block_diag_pairs(lru_wr).astype(BF16)
    lru_wi_bd = _block_diag_pairs(lru_wi).astype(BF16)
    pool_wb, pool_sc = pool_w.astype(BF16), _rows(pool_scale)
    fb = jnp.pad(_rows(forget_b), ((0, 0), (0, 0), (0, LANES - HEADS)))
    qg = jnp.tile(_rows(q_norm_g), (1, 1, LANES // HEAD_DIM))
    kg = jnp.tile(_rows(k_norm_g), (1, 1, LANES // HEAD_DIM))

    for l in range(depth):
        proj, xn = _inproj(xf, g_mix, w_mix, l, tm=512)
        y_a, y_b, y_c, c = _mixers(proj, lru_conv_w, lru_cb, lru_wr_bd, lru_rb, lru_wi_bd, lru_ib, lru_lam,
                                   pool_wb, pool_sc, sconv_w, fb, l, batch=batch, seq=seq)
        y_d, wu_bf, wd_bf, wo_bf = _attention(proj, c, qg, kg, same_head, w_mlp_up, w_mlp_down, w_out, l,
                                              batch=batch, seq=seq)
        merged = _merge(xn, (y_a, y_b, y_c, y_d), w_in_t, w_branch, l, tm=1024, tn=256)
        x1, xn2 = _outproj(merged, xf, wo_bf, g_mlp, l, tm=512)
        xf = _mlp(xn2, x1, wu_bf, wd_bf, tm=512, tf=2048)

    return xf.reshape(batch, seq, d)
```

```python
import functools

import jax
import jax.numpy as jnp
from jax import lax
from jax.experimental import pallas as pl
from jax.experimental.pallas import tpu as pltpu

F32 = jnp.float32
BF16 = jnp.bfloat16

D_MODEL = 2048
WIDTH = 512
N_BRANCH = 4
LRU_CONV = 4
LRU_C = 8.0
POOL_WINDOWS = (2, 4, 8, 16)
SCONV_K = 3
HEADS = 8
HEAD_DIM = 64
D_FF = 4 * D_MODEL
EPS = 1e-6

LANES = 128
SUBLANES = 8
VMEM_LIMIT = 56 * 1024 * 1024

N_MIX = 2 * WIDTH + 3 * WIDTH + 3 * WIDTH + HEADS
N_MIX_PAD = -(-N_MIX // LANES) * LANES
COL_LRU = 0
COL_POOL = WIDTH
COL_SCONV = 2 * WIDTH
COL_QKV = 5 * WIDTH
COL_FORGET = 8 * WIDTH


def _params(semantics):
    return pltpu.CompilerParams(dimension_semantics=semantics, vmem_limit_bytes=VMEM_LIMIT)


def _dot(a, b):
    return jnp.dot(a, b, preferred_element_type=F32)


def _dot_nt(a, b):
    return lax.dot_general(a, b, (((1,), (1,)), ((), ())), preferred_element_type=F32)


def _shift_rows(x, k, fill):
    n, c = x.shape
    if k >= n:
        return jnp.full((n, c), fill, x.dtype)
    if k % SUBLANES == 0:
        return jnp.concatenate([jnp.full((k, c), fill, x.dtype), x[:n - k]], axis=0)
    r = pltpu.roll(x, k, 0)
    row = lax.broadcasted_iota(jnp.int32, (SUBLANES, c), 0)
    top = jnp.where(row < k, jnp.asarray(fill, x.dtype), r[:SUBLANES])
    return jnp.concatenate([top, r[SUBLANES:]], axis=0)


SCAN_PAD = 4


def _scan_scratch_rows(seq):
    return seq + SUBLANES * SCAN_PAD


def _chunk_rows(tau, stride):
    return pl.ds(tau, SUBLANES, stride=stride)


def _linear_scan_rows(a, b, a_scr, b_scr):
    n, c = a.shape
    length = n // SUBLANES
    stride = length + SCAN_PAD
    for ch in range(SUBLANES):
        a_scr[ch * stride:ch * stride + length, :] = a[ch * length:(ch + 1) * length]
        b_scr[ch * stride:ch * stride + length, :] = b[ch * length:(ch + 1) * length]

    h = b_scr[_chunk_rows(0, stride), :]
    prod = a_scr[_chunk_rows(0, stride), :]
    for tau in range(1, length):
        rows = _chunk_rows(tau, stride)
        at = a_scr[rows, :]
        h = at * h + b_scr[rows, :]
        prod = at * prod
        b_scr[rows, :] = h
        a_scr[rows, :] = prod

    out = []
    state = jnp.zeros((1, c), F32)
    for ch in range(SUBLANES):
        lo = ch * stride
        out.append(b_scr[lo:lo + length, :] + a_scr[lo:lo + length, :] * state)
        state = h[ch:ch + 1] + prod[ch:ch + 1] * state
    return jnp.concatenate(out, axis=0)


def _cumsum_rows(x, scr):
    n, c = x.shape
    length = n // SUBLANES
    stride = length + SCAN_PAD
    for ch in range(SUBLANES):
        scr[ch * stride:ch * stride + length, :] = x[ch * length:(ch + 1) * length]
    h = scr[_chunk_rows(0, stride), :]
    for tau in range(1, length):
        rows = _chunk_rows(tau, stride)
        h = h + scr[rows, :]
        scr[rows, :] = h
    out = []
    state = jnp.zeros((1, c), F32)
    for ch in range(SUBLANES):
        lo = ch * stride
        out.append(scr[lo:lo + length, :] + state)
        state = state + h[ch:ch + 1]
    return jnp.concatenate(out, axis=0)


def _softplus(x):
    return jnp.maximum(x, 0.0) + jnp.log1p(jnp.exp(-jnp.abs(x)))


def _wcast_kernel(w_ref, o_ref):
    o_ref[...] = w_ref[...].astype(o_ref.dtype)


def _cast_mix_weights(w_in_t, layer, *, rows):
    d = w_in_t.shape[2]
    return pl.pallas_call(
        _wcast_kernel,
        grid=(N_MIX_PAD // rows,),
        in_specs=[pl.BlockSpec((None, rows, d), lambda j: (layer, j, 0))],
        out_specs=pl.BlockSpec((rows, d), lambda j: (j, 0)),
        out_shape=jax.ShapeDtypeStruct((N_MIX_PAD, d), BF16),
        compiler_params=_params(("parallel",)),
        name="cast_mix_weights",
    )(w_in_t)


def _inproj_kernel(x_ref, g_ref, w_ref, proj_ref, xn_ref):
    x = x_ref[...]
    ms = jnp.mean(x * x, axis=-1, keepdims=True)
    xn = ((x * lax.rsqrt(ms + EPS)) * g_ref[...]).astype(xn_ref.dtype)
    xn_ref[...] = xn
    proj_ref[...] = _dot_nt(xn, w_ref[...])


def _inproj(x, g, w_mix, layer, *, tm):
    t, d = x.shape
    n = w_mix.shape[0]
    return pl.pallas_call(
        _inproj_kernel,
        grid=(t // tm,),
        in_specs=[
            pl.BlockSpec((tm, d), lambda i: (i, 0)),
            pl.BlockSpec((None, 1, d), lambda i: (layer, 0, 0)),
            pl.BlockSpec((n, d), lambda i: (0, 0), pipeline_mode=pl.Buffered(1)),
        ],
        out_specs=[
            pl.BlockSpec((tm, n), lambda i: (i, 0)),
            pl.BlockSpec((tm, d), lambda i: (i, 0)),
        ],
        out_shape=[
            jax.ShapeDtypeStruct((t, n), F32),
            jax.ShapeDtypeStruct((t, d), BF16),
        ],
        compiler_params=_params(("parallel",)),
        name="inproj",
    )(x, g, w_mix)


def _lru_slab(x, cw, cb, wr, br, wi, bi, lam, a_scr, b_scr):
    u = cw[LRU_CONV - 1:LRU_CONV] * x + cb
    for k in range(1, LRU_CONV):
        u = u + cw[LRU_CONV - 1 - k:LRU_CONV - k] * _shift_rows(x, k, 0.0)
    ub = u.astype(BF16)
    r = jax.nn.sigmoid(_dot(ub, wr) + br)
    gi = jax.nn.sigmoid(_dot(ub, wi) + bi)
    log_a = (-LRU_C * r) * _softplus(-lam)
    a = jnp.exp(log_a)
    inp = jnp.sqrt(-jnp.tanh(log_a) * (a * a + 1.0)) * (gi * u)
    return _linear_scan_rows(a, inp, a_scr, b_scr)


def _pool_slab(x, group, w, scale):
    w2 = x + _shift_rows(x, 1, 0.0)
    w4 = w2 + _shift_rows(w2, 2, 0.0)
    w8 = w4 + _shift_rows(w4, 4, 0.0)
    w16 = w8 + _shift_rows(w8, 8, 0.0)
    wsum = jnp.where(group == 0, w2, jnp.where(group == 1, w4, jnp.where(group == 2, w8, w16)))
    win = jnp.left_shift(2, group)
    inv_win = jnp.where(group == 0, 0.5, jnp.where(group == 1, 0.25, jnp.where(group == 2, 0.125, 0.0625)))
    head = max(POOL_WINDOWS)
    t = lax.broadcasted_iota(jnp.int32, (head, x.shape[1]), 0)
    count = jnp.minimum(t + 1, win).astype(F32)
    pooled = jnp.concatenate([wsum[:head] / count - x[:head], wsum[head:] * inv_win - x[head:]], axis=0)
    return _dot(pooled.astype(BF16), w) * scale


def _sconv_slab(gate_b, gate_c, xc, w):
    z = gate_c * xc
    acc = w[SCONV_K - 1:SCONV_K] * z
    for k in range(1, SCONV_K):
        acc = acc + w[SCONV_K - 1 - k:SCONV_K - k] * _shift_rows(z, k, 0.0)
    return gate_b * acc


def _mixers_kernel(xa_ref, xp_ref, gb_ref, gc_ref, xc_ref, f_ref,
                   cw_ref, cb_ref, wr_ref, br_ref, wi_ref, bi_ref, lam_ref,
                   pw_ref, ps_ref, sw_ref, fb_ref,
                   ya_ref, yb_ref, yc_ref, c_ref, a_scr, b_scr):
    slab = pl.program_id(1)
    ya_ref[...] = _lru_slab(xa_ref[...], cw_ref[...], cb_ref[...], wr_ref[0], br_ref[...],
                            wi_ref[0], bi_ref[...], lam_ref[...], a_scr, b_scr).astype(ya_ref.dtype)
    yb_ref[...] = _pool_slab(xp_ref[...], slab, pw_ref[0], ps_ref[...]).astype(yb_ref.dtype)
    yc_ref[...] = _sconv_slab(gb_ref[...], gc_ref[...], xc_ref[...], sw_ref[...]).astype(yc_ref.dtype)

    @pl.when(slab == 0)
    def _():
        z = f_ref[...] + fb_ref[...]
        log_f = jnp.minimum(z, 0.0) - jnp.log1p(jnp.exp(-jnp.abs(z)))
        head_lane = lax.broadcasted_iota(jnp.int32, (1, LANES), 1) < HEADS
        c_ref[...] = jnp.where(head_lane, _cumsum_rows(log_f, b_scr), 0.0)


def _mixers(proj, cw, cb, wr_bd, br, wi_bd, bi, lam, pool_w, pool_scale, sconv_w, fb, layer, *, batch, seq):
    nc = WIDTH // LANES

    def col(c0, p=0):
        return pl.BlockSpec((seq, LANES), lambda b, c: (b, (c0 + p * WIDTH) // LANES + c))

    vec = pl.BlockSpec((None, 1, LANES), lambda b, c: (layer, 0, c))
    mat = pl.BlockSpec((None, 1, LANES, LANES), lambda b, c: (layer, c, 0, 0))
    y_spec = pl.BlockSpec((seq, LANES), lambda b, c: (b, c))
    y_shape = jax.ShapeDtypeStruct((batch * seq, WIDTH), BF16)
    return pl.pallas_call(
        _mixers_kernel,
        grid=(batch, nc),
        in_specs=[col(COL_LRU), col(COL_POOL), col(COL_SCONV, 0), col(COL_SCONV, 1), col(COL_SCONV, 2),
                  pl.BlockSpec((seq, LANES), lambda b, c: (b, COL_FORGET // LANES)),
                  pl.BlockSpec((None, LRU_CONV, LANES), lambda b, c: (layer, 0, c)),
                  vec, mat, vec, mat, vec, vec,
                  mat, vec,
                  pl.BlockSpec((None, SCONV_K, LANES), lambda b, c: (layer, 0, c)),
                  pl.BlockSpec((None, 1, LANES), lambda b, c: (layer, 0, 0))],
        out_specs=[y_spec, y_spec, y_spec, pl.BlockSpec((seq, LANES), lambda b, c: (b, 0))],
        out_shape=[y_shape, y_shape, y_shape, jax.ShapeDtypeStruct((batch * seq, LANES), F32)],
        scratch_shapes=[pltpu.VMEM((_scan_scratch_rows(seq), LANES), F32)] * 2,
        compiler_params=_params(("parallel", "arbitrary")),
        name="mixers",
    )(proj, proj, proj, proj, proj, proj, cw, cb, wr_bd, br, wi_bd, bi, lam, pool_w, pool_scale, sconv_w, fb)


ATTN_TQ = 512
LOG2E = 1.4426950408889634
ATTN_ROWS = 64
ATTN_BIAS_LANES = 12


def _attn_kernel(*refs, streams):
    n_streams = len(streams)
    q_ref, k_ref, v_ref, c_ref, sel_ref, qg_ref, kg_ref, hm_ref = refs[:8]
    w_src = refs[8:8 + n_streams]
    o_ref = refs[8 + n_streams]
    w_dst = refs[9 + n_streams:9 + 2 * n_streams]
    q_scr, k_scr, vt_scr, s_scr, p_scr = refs[9 + 2 * n_streams:14 + 2 * n_streams]
    w_in = refs[14 + 2 * n_streams:14 + 3 * n_streams]
    w_out = refs[14 + 3 * n_streams:14 + 4 * n_streams]
    sem_in, sem_out = refs[14 + 4 * n_streams:]

    step = pl.program_id(0) * pl.num_programs(1) + pl.program_id(1)
    last_step = pl.num_programs(0) * pl.num_programs(1) - 1

    def slab(a, at_step):
        rows, cols = w_in[a].shape
        if streams[a][1] == "cols":
            return (pl.ds(0, rows), pl.ds(pl.multiple_of(at_step * cols, cols), cols))
        return (pl.ds(pl.multiple_of(at_step * rows, rows), rows), pl.ds(0, cols))

    def fetch(a):
        return pltpu.make_async_copy(w_src[a].at[(streams[a][0],) + slab(a, step)], w_in[a], sem_in.at[a])

    def write_back(a, at_step):
        return pltpu.make_async_copy(w_out[a], w_dst[a].at[slab(a, at_step)], sem_out.at[a])

    @pl.when(step > 0)
    def _():
        for a in range(n_streams):
            write_back(a, step - 1).wait()

    for a in range(n_streams):
        fetch(a).start()

    seq = q_ref.shape[0]
    hm = hm_ref[...]

    def head_norm(x, g):
        sq = x * x
        hi = sq.astype(BF16)
        lo = (sq - hi.astype(F32)).astype(BF16)
        ms = _dot(hi, hm) + _dot(lo, hm)
        return (x * lax.rsqrt(ms + EPS)) * g

    tq = ATTN_TQ
    nblk = seq // tq
    qg, kg = qg_ref[...] * (HEAD_DIM ** -0.5 * LOG2E), kg_ref[...]
    lane = lax.broadcasted_iota(jnp.int32, (1, LANES), 1)
    first = lane < HEAD_DIM

    def lane_in(lo, hi):
        return (lane >= lo) & (lane < hi)

    sel_hi, sel_mid, sel_lo = sel_ref[0, 0], sel_ref[0, 1], sel_ref[0, 2]

    for j in range(nblk):
        rows = slice(j * tq, (j + 1) * tq)
        c = c_ref[rows, :] * LOG2E
        hi = c.astype(BF16)
        mid = (c - hi.astype(F32)).astype(BF16)
        lo = ((c - hi.astype(F32)) - mid.astype(F32)).astype(BF16)
        pieces = _dot(hi, sel_hi) + _dot(mid, sel_mid) + _dot(lo, sel_lo)
        k_extra = jnp.where(lane_in(0, 6), -pieces, jnp.where(lane_in(6, ATTN_BIAS_LANES), 1.0, 0.0))
        qa_extra = jnp.where(lane_in(0, 3), 1.0, jnp.where(lane_in(6, 9), pieces, 0.0))
        qb_extra = jnp.where(lane_in(3, 6), 1.0, jnp.where(lane_in(9, ATTN_BIAS_LANES), pieces, 0.0))
        qn = head_norm(q_ref[rows, :], qg)
        q_scr[j, :tq, :LANES] = jnp.where(first, qn, 0.0).astype(BF16)
        q_scr[j, :tq, LANES:] = qa_extra.astype(BF16)
        q_scr[j, tq:, :LANES] = jnp.where(first, 0.0, qn).astype(BF16)
        q_scr[j, tq:, LANES:] = qb_extra.astype(BF16)
        k_scr[rows, :LANES] = head_norm(k_ref[rows, :], kg).astype(BF16)
        k_scr[rows, LANES:] = k_extra.astype(BF16)
        vt_scr[:, rows] = v_ref[rows, :].T.astype(BF16)

    rb = ATTN_ROWS
    nq = 2 * tq
    key_pos = lax.broadcasted_iota(jnp.int32, (tq, nq), 0)
    lane_q = lax.broadcasted_iota(jnp.int32, (tq, nq), 1)
    causal = key_pos <= jnp.where(lane_q < tq, lane_q, lane_q - tq)

    def fold8(x, op):
        out = x[:SUBLANES]
        for g in range(1, x.shape[0] // SUBLANES):
            out = op(out, x[g * SUBLANES:(g + 1) * SUBLANES])
        return out

    for i in range(nblk):
        q0, kend = i * tq, (i + 1) * tq
        s_blk = s_scr.at[i % 2]
        p_blk = p_scr.at[i % 2]

        m8 = None
        for kb in range(i + 1):
            rows = slice(kb * tq, (kb + 1) * tq)
            s = _dot_nt(k_scr[rows, :], q_scr[i])
            if kb == i:
                s = jnp.where(causal, s, -jnp.inf)
            s_blk[rows, :] = s
            part = fold8(s, jnp.maximum)
            m8 = part if m8 is None else jnp.maximum(m8, part)
        m = jnp.max(m8, axis=0, keepdims=True)

        l8 = jnp.zeros((SUBLANES, nq), F32)
        for c in range(kend // rb):
            p = jnp.exp2(s_blk[c * rb:(c + 1) * rb, :] - m)
            p_blk[c * rb:(c + 1) * rb, :] = p.astype(BF16)
            l8 = l8 + fold8(p, jnp.add)
        l = jnp.sum(l8, axis=0, keepdims=True)

        o_t = _dot(vt_scr[:, :kend], p_blk[:kend, :]) / l
        o = o_t.T
        o_ref[q0:kend, :] = jnp.where(first, o[:tq], o[tq:]).astype(o_ref.dtype)

    for a in range(n_streams):
        fetch(a).wait()
        w_out[a][...] = w_in[a][...].astype(BF16)
        write_back(a, step).start()

    @pl.when(step == last_step)
    def _():
        for a in range(n_streams):
            write_back(a, step).wait()


def _bias_lane_selectors():
    pair = jnp.arange(HEADS // 2)[:, None, None, None]
    p = jnp.arange(3)[None, :, None, None]
    row = jnp.arange(LANES)[None, None, :, None]
    col = jnp.arange(LANES)[None, None, None, :]
    head_a = (row == 2 * pair) & ((col == p) | (col == 6 + p))
    head_b = (row == 2 * pair + 1) & ((col == 3 + p) | (col == 9 + p))
    return (head_a | head_b).astype(BF16)


def _attention(proj, c, qg, kg, hm, weights, layer, *, batch, seq):
    npair = HEADS // 2
    col0 = COL_QKV // LANES
    steps = batch * npair

    def part(p):
        return pl.BlockSpec((seq, LANES), lambda b, h: (b, col0 + p * npair + h))

    def slab_shape(w, rows, split):
        return (rows // steps, w.shape[2]) if split == "rows" else (rows, w.shape[2] // steps)

    slabs = [slab_shape(w, rows, split) for w, _, rows, split in weights]
    any_spec = pl.BlockSpec(memory_space=pl.ANY)
    vec = pl.BlockSpec((None, 1, LANES), lambda b, h: (layer, 0, 0))
    n = len(weights)
    return pl.pallas_call(
        functools.partial(_attn_kernel, streams=tuple((lyr, split) for _, lyr, _, split in weights)),
        grid=(batch, npair),
        in_specs=[part(0), part(1), part(2),
                  pl.BlockSpec((seq, LANES), lambda b, h: (b, 0)),
                  pl.BlockSpec((1, 3, LANES, LANES), lambda b, h: (h, 0, 0, 0)),
                  vec, vec, pl.BlockSpec((LANES, LANES), lambda b, h: (0, 0))] + [any_spec] * n,
        out_specs=[pl.BlockSpec((seq, LANES), lambda b, h: (b, h))] + [any_spec] * n,
        out_shape=[jax.ShapeDtypeStruct((batch * seq, WIDTH), BF16)]
                  + [jax.ShapeDtypeStruct((rows, w.shape[2]), BF16) for w, _, rows, _ in weights],
        scratch_shapes=[pltpu.VMEM((seq // ATTN_TQ, 2 * ATTN_TQ, 2 * LANES), BF16),
                        pltpu.VMEM((seq, 2 * LANES), BF16),
                        pltpu.VMEM((LANES, seq), BF16),
                        pltpu.VMEM((2, seq, 2 * ATTN_TQ), F32),
                        pltpu.VMEM((2, seq, 2 * ATTN_TQ), BF16)]
                       + [pltpu.VMEM(sh, F32) for sh in slabs] + [pltpu.VMEM(sh, BF16) for sh in slabs]
                       + [pltpu.SemaphoreType.DMA((n,)), pltpu.SemaphoreType.DMA((n,))],
        compiler_params=_params(("arbitrary", "arbitrary")),
        name="fox_attention",
    )(proj, proj, proj, c, _bias_lane_selectors(), qg, kg, hm, *[w for w, _, _, _ in weights])


def _merge_kernel(xn_ref, ya_ref, yb_ref, yc_ref, yd_ref, wg0_ref, wg1_ref, wg2_ref, wg3_ref, wb_ref, o_ref):
    xn = xn_ref[...]
    acc = None
    for k, (y_ref, wg_ref) in enumerate(((ya_ref, wg0_ref), (yb_ref, wg1_ref),
                                         (yc_ref, wg2_ref), (yd_ref, wg3_ref))):
        gate = jax.nn.sigmoid(_dot_nt(xn, wg_ref[...].astype(BF16)))
        term = gate * _dot(y_ref[...], wb_ref[0, k].astype(BF16))
        acc = term if acc is None else acc + term
    o_ref[...] = acc.astype(o_ref.dtype)


def _merge(xn, ys, w_in_t, wb, layer, *, tm, tn):
    t, d = xn.shape
    nblk = d // tn

    def wg_spec(k):
        return pl.BlockSpec((pl.Squeezed(), pl.Element(tn), pl.Element(d)),
                            lambda i, j: (layer, pl.multiple_of(N_MIX + k * d + j * tn, SUBLANES), 0))

    y_spec = pl.BlockSpec((tm, WIDTH), lambda i, j: (i, 0))
    return pl.pallas_call(
        _merge_kernel,
        grid=(t // tm, nblk),
        in_specs=[pl.BlockSpec((tm, d), lambda i, j: (i, 0)), y_spec, y_spec, y_spec, y_spec,
                  wg_spec(0), wg_spec(1), wg_spec(2), wg_spec(3),
                  pl.BlockSpec((1, N_BRANCH, WIDTH, tn), lambda i, j: (layer, 0, 0, j))],
        out_specs=pl.BlockSpec((tm, tn), lambda i, j: (i, j)),
        out_shape=jax.ShapeDtypeStruct((t, d), BF16),
        compiler_params=_params(("parallel", "arbitrary")),
        name="merge",
    )(xn, *ys, w_in_t, w_in_t, w_in_t, w_in_t, wb)


def _outproj_kernel(m_ref, x_ref, w_ref, g_ref, x1_ref, xn_ref):
    x1 = x_ref[...] + _dot(m_ref[...], w_ref[...])
    x1_ref[...] = x1
    ms = jnp.mean(x1 * x1, axis=-1, keepdims=True)
    xn_ref[...] = ((x1 * lax.rsqrt(ms + EPS)) * g_ref[...]).astype(xn_ref.dtype)


def _outproj(merged, x, w_out, g, layer, *, tm):
    t, d = x.shape
    row = pl.BlockSpec((tm, d), lambda i: (i, 0))
    return pl.pallas_call(
        _outproj_kernel,
        grid=(t // tm,),
        in_specs=[row, row,
                  pl.BlockSpec((d, d), lambda i: (0, 0), pipeline_mode=pl.Buffered(1)),
                  pl.BlockSpec((None, 1, d), lambda i: (layer, 0, 0))],
        out_specs=[row, row],
        out_shape=[jax.ShapeDtypeStruct((t, d), F32), jax.ShapeDtypeStruct((t, d), BF16)],
        compiler_params=_params(("parallel",)),
        name="outproj",
    )(merged, x, w_out, g)


def _mlp_kernel(xn_ref, x_ref, wu_ref, wd_ref, o_ref):
    @pl.when(pl.program_id(1) == 0)
    def _():
        o_ref[...] = x_ref[...]

    h = jnp.maximum(_dot(xn_ref[...], wu_ref[...]), 0.0)
    o_ref[...] += _dot((h * h).astype(BF16), wd_ref[...])


def _mlp(xn, x, wu, wd, *, tm, tf):
    t, d = x.shape
    f = wu.shape[1]
    row = pl.BlockSpec((tm, d), lambda i, j: (i, 0))
    return pl.pallas_call(
        _mlp_kernel,
        grid=(t // tm, f // tf),
        in_specs=[row, row,
                  pl.BlockSpec((d, tf), lambda i, j: (0, j)),
                  pl.BlockSpec((tf, d), lambda i, j: (j, 0))],
        out_specs=row,
        out_shape=jax.ShapeDtypeStruct((t, d), F32),
        compiler_params=_params(("parallel", "arbitrary")),
        name="mlp",
    )(xn, x, wu, wd)


def _block_diag_pairs(w):
    depth, n2, bd, _ = w.shape
    per = LANES // bd
    w5 = w.reshape(depth, n2 // per, per, bd, bd)
    eye = jnp.eye(per, dtype=w.dtype)
    return jnp.einsum('ljaik,ab->ljaibk', w5, eye).reshape(depth, n2 // per, LANES, LANES)


def _rows(v):
    return v.reshape(v.shape[0], 1, -1).astype(F32)


def kernel(x, norm_mix_g, w_in, lru_conv_w, lru_conv_b, lru_wr, lru_br, lru_wi, lru_bi, lru_lambda, pool_w, pool_scale, sconv_w, q_norm_g, k_norm_g, forget_b, w_branch, w_out, norm_mlp_g, w_mlp_up, w_mlp_down):
    batch, seq, d = x.shape
    assert d == D_MODEL and seq % ATTN_TQ == 0
    depth = w_in.shape[0]
    t = batch * seq
    xf = x.reshape(t, d)

    lane = jnp.arange(LANES)
    same_head = ((lane[:, None] // HEAD_DIM == lane[None, :] // HEAD_DIM) / HEAD_DIM).astype(BF16)

    w_in_t = jnp.swapaxes(w_in, 1, 2)
    w_mix = _cast_mix_weights(w_in_t, 0, rows=N_MIX_PAD // 8)

    g_mix, g_mlp = _rows(norm_mix_g), _rows(norm_mlp_g)
    lru_cb, lru_rb, lru_ib, lru_lam = _rows(lru_conv_b), _rows(lru_br), _rows(lru_bi), _rows(lru_lambda)
    lru_wr_bd = _block_diag_pairs(lru_wr).astype(BF16)
    lru_wi_bd = _block_diag_pairs(lru_wi).astype(BF16)
    pool_wb, pool_sc = pool_w.astype(BF16), _rows(pool_scale)
    fb = jnp.pad(_rows(forget_b), ((0, 0), (0, 0), (0, LANES - HEADS)))
    qg = jnp.tile(_rows(q_norm_g), (1, 1, LANES // HEAD_DIM))
    kg = jnp.tile(_rows(k_norm_g), (1, 1, LANES // HEAD_DIM))

    for l in range(depth):
        proj, xn = _inproj(xf, g_mix, w_mix, l, tm=512)
        y_a, y_b, y_c, c = _mixers(proj, lru_conv_w, lru_cb, lru_wr_bd, lru_rb, lru_wi_bd, lru_ib, lru_lam,
                                   pool_wb, pool_sc, sconv_w, fb, l, batch=batch, seq=seq)
        to_convert = [(w_mlp_up, l, d, "rows"), (w_mlp_down, l, D_FF, "rows"), (w_out, l, d, "rows")]
        if l + 1 < depth:
            to_convert.append((w_in_t, l + 1, N_MIX_PAD, "cols"))
        y_d, wu_bf, wd_bf, wo_bf, *next_mix = _attention(proj, c, qg, kg, same_head, tuple(to_convert), l,
                                                         batch=batch, seq=seq)
        merged = _merge(xn, (y_a, y_b, y_c, y_d), w_in_t, w_branch, l, tm=1024, tn=256)
        x1, xn2 = _outproj(merged, xf, wo_bf, g_mlp, l, tm=512)
        xf = _mlp(xn2, x1, wu_bf, wd_bf, tm=512, tf=2048)
        if next_mix:
            w_mix = next_mix[0]

    return xf.reshape(batch, seq, d)
```

```python
import functools

import jax
import jax.numpy as jnp
from jax import lax
from jax.experimental import pallas as pl
from jax.experimental.pallas import tpu as pltpu

F32 = jnp.float32
BF16 = jnp.bfloat16

D_MODEL = 2048
WIDTH = 512
N_BRANCH = 4
LRU_CONV = 4
LRU_C = 8.0
POOL_WINDOWS = (2, 4, 8, 16)
SCONV_K = 3
HEADS = 8
HEAD_DIM = 64
D_FF = 4 * D_MODEL
EPS = 1e-6

LANES = 128
SUBLANES = 8
VMEM_LIMIT = 56 * 1024 * 1024

N_MIX = 2 * WIDTH + 3 * WIDTH + 3 * WIDTH + HEADS
N_MIX_PAD = -(-N_MIX // LANES) * LANES
COL_LRU = 0
COL_POOL = WIDTH
COL_SCONV = 2 * WIDTH
COL_QKV = 5 * WIDTH
COL_FORGET = 8 * WIDTH


def _params(semantics):
    return pltpu.CompilerParams(dimension_semantics=semantics, vmem_limit_bytes=VMEM_LIMIT)


def _dot(a, b):
    return jnp.dot(a, b, preferred_element_type=F32)


def _dot_nt(a, b):
    return lax.dot_general(a, b, (((1,), (1,)), ((), ())), preferred_element_type=F32)


def _shift_rows(x, k, fill):
    n, c = x.shape
    if k >= n:
        return jnp.full((n, c), fill, x.dtype)
    if k % SUBLANES == 0:
        return jnp.concatenate([jnp.full((k, c), fill, x.dtype), x[:n - k]], axis=0)
    r = pltpu.roll(x, k, 0)
    row = lax.broadcasted_iota(jnp.int32, (SUBLANES, c), 0)
    top = jnp.where(row < k, jnp.asarray(fill, x.dtype), r[:SUBLANES])
    return jnp.concatenate([top, r[SUBLANES:]], axis=0)


SCAN_PAD = 4


def _scan_scratch_rows(seq):
    return seq + SUBLANES * SCAN_PAD


def _chunk_rows(tau, stride):
    return pl.ds(tau, SUBLANES, stride=stride)


def _linear_scan_rows(a, b, a_scr, b_scr):
    n, c = a.shape
    length = n // SUBLANES
    stride = length + SCAN_PAD
    for ch in range(SUBLANES):
        a_scr[ch * stride:ch * stride + length, :] = a[ch * length:(ch + 1) * length]
        b_scr[ch * stride:ch * stride + length, :] = b[ch * length:(ch + 1) * length]

    h = b_scr[_chunk_rows(0, stride), :]
    prod = a_scr[_chunk_rows(0, stride), :]
    for tau in range(1, length):
        rows = _chunk_rows(tau, stride)
        at = a_scr[rows, :]
        h = at * h + b_scr[rows, :]
        prod = at * prod
        b_scr[rows, :] = h
        a_scr[rows, :] = prod

    out = []
    state = jnp.zeros((1, c), F32)
    for ch in range(SUBLANES):
        lo = ch * stride
        out.append(b_scr[lo:lo + length, :] + a_scr[lo:lo + length, :] * state)
        state = h[ch:ch + 1] + prod[ch:ch + 1] * state
    return jnp.concatenate(out, axis=0)


def _cumsum_rows(x, scr):
    n, c = x.shape
    length = n // SUBLANES
    stride = length + SCAN_PAD
    for ch in range(SUBLANES):
        scr[ch * stride:ch * stride + length, :] = x[ch * length:(ch + 1) * length]
    h = scr[_chunk_rows(0, stride), :]
    for tau in range(1, length):
        rows = _chunk_rows(tau, stride)
        h = h + scr[rows, :]
        scr[rows, :] = h
    out = []
    state = jnp.zeros((1, c), F32)
    for ch in range(SUBLANES):
        lo = ch * stride
        out.append(scr[lo:lo + length, :] + state)
        state = state + h[ch:ch + 1]
    return jnp.concatenate(out, axis=0)


def _softplus(x):
    return jnp.maximum(x, 0.0) + jnp.log1p(jnp.exp(-jnp.abs(x)))


def _wcast_kernel(w_ref, o_ref):
    o_ref[...] = w_ref[...].astype(o_ref.dtype)


def _cast_mix_weights(w_in_t, layer, *, rows):
    d = w_in_t.shape[2]
    return pl.pallas_call(
        _wcast_kernel,
        grid=(N_MIX_PAD // rows,),
        in_specs=[pl.BlockSpec((None, rows, d), lambda j: (layer, j, 0))],
        out_specs=pl.BlockSpec((rows, d), lambda j: (j, 0)),
        out_shape=jax.ShapeDtypeStruct((N_MIX_PAD, d), BF16),
        compiler_params=_params(("parallel",)),
        name="cast_mix_weights",
    )(w_in_t)


def _inproj_kernel(x_ref, g_ref, w_ref, proj_ref, xn_ref):
    x = x_ref[...]
    ms = jnp.mean(x * x, axis=-1, keepdims=True)
    xn = ((x * lax.rsqrt(ms + EPS)) * g_ref[...]).astype(xn_ref.dtype)
    xn_ref[...] = xn
    proj_ref[...] = _dot_nt(xn, w_ref[...])


def _inproj(x, g, w_mix, layer, *, tm):
    t, d = x.shape
    n = w_mix.shape[0]
    return pl.pallas_call(
        _inproj_kernel,
        grid=(t // tm,),
        in_specs=[
            pl.BlockSpec((tm, d), lambda i: (i, 0)),
            pl.BlockSpec((None, 1, d), lambda i: (layer, 0, 0)),
            pl.BlockSpec((n, d), lambda i: (0, 0), pipeline_mode=pl.Buffered(1)),
        ],
        out_specs=[
            pl.BlockSpec((tm, n), lambda i: (i, 0)),
            pl.BlockSpec((tm, d), lambda i: (i, 0)),
        ],
        out_shape=[
            jax.ShapeDtypeStruct((t, n), F32),
            jax.ShapeDtypeStruct((t, d), BF16),
        ],
        compiler_params=_params(("parallel",)),
        name="inproj",
    )(x, g, w_mix)


def _lru_slab(x, cw, cb, wr, br, wi, bi, lam, a_scr, b_scr):
    u = cw[LRU_CONV - 1:LRU_CONV] * x + cb
    for k in range(1, LRU_CONV):
        u = u + cw[LRU_CONV - 1 - k:LRU_CONV - k] * _shift_rows(x, k, 0.0)
    ub = u.astype(BF16)
    r = jax.nn.sigmoid(_dot(ub, wr) + br)
    gi = jax.nn.sigmoid(_dot(ub, wi) + bi)
    log_a = (-LRU_C * r) * _softplus(-lam)
    a = jnp.exp(log_a)
    inp = jnp.sqrt(-jnp.tanh(log_a) * (a * a + 1.0)) * (gi * u)
    return _linear_scan_rows(a, inp, a_scr, b_scr)


def _pool_slab(x, group, w, scale):
    w2 = x + _shift_rows(x, 1, 0.0)
    w4 = w2 + _shift_rows(w2, 2, 0.0)
    w8 = w4 + _shift_rows(w4, 4, 0.0)
    w16 = w8 + _shift_rows(w8, 8, 0.0)
    wsum = jnp.where(group == 0, w2, jnp.where(group == 1, w4, jnp.where(group == 2, w8, w16)))
    win = jnp.left_shift(2, group)
    inv_win = jnp.where(group == 0, 0.5, jnp.where(group == 1, 0.25, jnp.where(group == 2, 0.125, 0.0625)))
    head = max(POOL_WINDOWS)
    t = lax.broadcasted_iota(jnp.int32, (head, x.shape[1]), 0)
    count = jnp.minimum(t + 1, win).astype(F32)
    pooled = jnp.concatenate([wsum[:head] / count - x[:head], wsum[head:] * inv_win - x[head:]], axis=0)
    return _dot(pooled.astype(BF16), w) * scale


def _sconv_slab(gate_b, gate_c, xc, w):
    z = gate_c * xc
    acc = w[SCONV_K - 1:SCONV_K] * z
    for k in range(1, SCONV_K):
        acc = acc + w[SCONV_K - 1 - k:SCONV_K - k] * _shift_rows(z, k, 0.0)
    return gate_b * acc


def _mixers_kernel(xa_ref, xp_ref, gb_ref, gc_ref, xc_ref, f_ref,
                   cw_ref, cb_ref, wr_ref, br_ref, wi_ref, bi_ref, lam_ref,
                   pw_ref, ps_ref, sw_ref, fb_ref,
                   ya_ref, yb_ref, yc_ref, c_ref, a_scr, b_scr):
    slab = pl.program_id(1)
    ya_ref[...] = _lru_slab(xa_ref[...], cw_ref[...], cb_ref[...], wr_ref[0], br_ref[...],
                            wi_ref[0], bi_ref[...], lam_ref[...], a_scr, b_scr).astype(ya_ref.dtype)
    yb_ref[...] = _pool_slab(xp_ref[...], slab, pw_ref[0], ps_ref[...]).astype(yb_ref.dtype)
    yc_ref[...] = _sconv_slab(gb_ref[...], gc_ref[...], xc_ref[...], sw_ref[...]).astype(yc_ref.dtype)

    @pl.when(slab == 0)
    def _():
        z = f_ref[...] + fb_ref[...]
        log_f = jnp.minimum(z, 0.0) - jnp.log1p(jnp.exp(-jnp.abs(z)))
        head_lane = lax.broadcasted_iota(jnp.int32, (1, LANES), 1) < HEADS
        c_ref[...] = jnp.where(head_lane, _cumsum_rows(log_f, b_scr), 0.0)


def _mixers(proj, cw, cb, wr_bd, br, wi_bd, bi, lam, pool_w, pool_scale, sconv_w, fb, layer, *, batch, seq):
    nc = WIDTH // LANES

    def col(c0, p=0):
        return pl.BlockSpec((seq, LANES), lambda b, c: (b, (c0 + p * WIDTH) // LANES + c))

    vec = pl.BlockSpec((None, 1, LANES), lambda b, c: (layer, 0, c))
    mat = pl.BlockSpec((None, 1, LANES, LANES), lambda b, c: (layer, c, 0, 0))
    y_spec = pl.BlockSpec((seq, LANES), lambda b, c: (b, c))
    y_shape = jax.ShapeDtypeStruct((batch * seq, WIDTH), BF16)
    return pl.pallas_call(
        _mixers_kernel,
        grid=(batch, nc),
        in_specs=[col(COL_LRU), col(COL_POOL), col(COL_SCONV, 0), col(COL_SCONV, 1), col(COL_SCONV, 2),
                  pl.BlockSpec((seq, LANES), lambda b, c: (b, COL_FORGET // LANES)),
                  pl.BlockSpec((None, LRU_CONV, LANES), lambda b, c: (layer, 0, c)),
                  vec, mat, vec, mat, vec, vec,
                  mat, vec,
                  pl.BlockSpec((None, SCONV_K, LANES), lambda b, c: (layer, 0, c)),
                  pl.BlockSpec((None, 1, LANES), lambda b, c: (layer, 0, 0))],
        out_specs=[y_spec, y_spec, y_spec, pl.BlockSpec((seq, LANES), lambda b, c: (b, 0))],
        out_shape=[y_shape, y_shape, y_shape, jax.ShapeDtypeStruct((batch * seq, LANES), F32)],
        scratch_shapes=[pltpu.VMEM((_scan_scratch_rows(seq), LANES), F32)] * 2,
        compiler_params=_params(("parallel", "arbitrary")),
        name="mixers",
    )(proj, proj, proj, proj, proj, proj, cw, cb, wr_bd, br, wi_bd, bi, lam, pool_w, pool_scale, sconv_w, fb)


ATTN_TQ = 512
LOG2E = 1.4426950408889634
ATTN_ROWS = 64
ATTN_BIAS_LANES = 12


def _attn_kernel(*refs, streams):
    n_streams = len(streams)
    q_ref, k_ref, v_ref, c_ref, sel_ref, qg_ref, kg_ref, hm_ref = refs[:8]
    w_src = refs[8:8 + n_streams]
    o_ref = refs[8 + n_streams]
    w_dst = refs[9 + n_streams:9 + 2 * n_streams]
    q_scr, k_scr, vt_scr, s_scr, p_scr = refs[9 + 2 * n_streams:14 + 2 * n_streams]
    w_in = refs[14 + 2 * n_streams:14 + 3 * n_streams]
    w_out = refs[14 + 3 * n_streams:14 + 4 * n_streams]
    sem_in, sem_out = refs[14 + 4 * n_streams:]

    step = pl.program_id(0) * pl.num_programs(1) + pl.program_id(1)
    last_step = pl.num_programs(0) * pl.num_programs(1) - 1

    def slab(a, at_step):
        rows, cols = w_in[a].shape
        if streams[a][1] == "cols":
            return (pl.ds(0, rows), pl.ds(pl.multiple_of(at_step * cols, cols), cols))
        return (pl.ds(pl.multiple_of(at_step * rows, rows), rows), pl.ds(0, cols))

    def fetch(a):
        return pltpu.make_async_copy(w_src[a].at[(streams[a][0],) + slab(a, step)], w_in[a], sem_in.at[a])

    def write_back(a, at_step):
        return pltpu.make_async_copy(w_out[a], w_dst[a].at[slab(a, at_step)], sem_out.at[a])

    for a in range(n_streams):
        fetch(a).start()

    seq = q_ref.shape[0]
    hm = hm_ref[...]

    def head_norm(x, g):
        sq = x * x
        hi = sq.astype(BF16)
        lo = (sq - hi.astype(F32)).astype(BF16)
        ms = _dot(hi, hm) + _dot(lo, hm)
        return (x * lax.rsqrt(ms + EPS)) * g

    tq = ATTN_TQ
    nblk = seq // tq
    qg, kg = qg_ref[...] * (HEAD_DIM ** -0.5 * LOG2E), kg_ref[...]
    lane = lax.broadcasted_iota(jnp.int32, (1, LANES), 1)
    first = lane < HEAD_DIM

    def lane_in(lo, hi):
        return (lane >= lo) & (lane < hi)

    sel_hi, sel_mid, sel_lo = sel_ref[0, 0], sel_ref[0, 1], sel_ref[0, 2]

    for j in range(nblk):
        rows = slice(j * tq, (j + 1) * tq)
        c = c_ref[rows, :] * LOG2E
        hi = c.astype(BF16)
        mid = (c - hi.astype(F32)).astype(BF16)
        lo = ((c - hi.astype(F32)) - mid.astype(F32)).astype(BF16)
        pieces = _dot(hi, sel_hi) + _dot(mid, sel_mid) + _dot(lo, sel_lo)
        k_extra = jnp.where(lane_in(0, 6), -pieces, jnp.where(lane_in(6, ATTN_BIAS_LANES), 1.0, 0.0))
        qa_extra = jnp.where(lane_in(0, 3), 1.0, jnp.where(lane_in(6, 9), pieces, 0.0))
        qb_extra = jnp.where(lane_in(3, 6), 1.0, jnp.where(lane_in(9, ATTN_BIAS_LANES), pieces, 0.0))
        qn = head_norm(q_ref[rows, :], qg)
        q_scr[j, :tq, :LANES] = jnp.where(first, qn, 0.0).astype(BF16)
        q_scr[j, :tq, LANES:] = qa_extra.astype(BF16)
        q_scr[j, tq:, :LANES] = jnp.where(first, 0.0, qn).astype(BF16)
        q_scr[j, tq:, LANES:] = qb_extra.astype(BF16)
        k_scr[rows, :LANES] = head_norm(k_ref[rows, :], kg).astype(BF16)
        k_scr[rows, LANES:] = k_extra.astype(BF16)
        vt_scr[:, rows] = v_ref[rows, :].T.astype(BF16)

    rb = ATTN_ROWS
    nq = 2 * tq
    key_pos = lax.broadcasted_iota(jnp.int32, (tq, nq), 0)
    lane_q = lax.broadcasted_iota(jnp.int32, (tq, nq), 1)
    causal = key_pos <= jnp.where(lane_q < tq, lane_q, lane_q - tq)

    def fold8(x, op):
        out = x[:SUBLANES]
        for g in range(1, x.shape[0] // SUBLANES):
            out = op(out, x[g * SUBLANES:(g + 1) * SUBLANES])
        return out

    for i in range(nblk):
        q0, kend = i * tq, (i + 1) * tq
        s_blk = s_scr.at[i % 2]
        p_blk = p_scr.at[i % 2]

        m8 = None
        for kb in range(i + 1):
            rows = slice(kb * tq, (kb + 1) * tq)
            s = _dot_nt(k_scr[rows, :], q_scr[i])
            if kb == i:
                s = jnp.where(causal, s, -jnp.inf)
            s_blk[rows, :] = s
            part = fold8(s, jnp.maximum)
            m8 = part if m8 is None else jnp.maximum(m8, part)
        m = jnp.max(m8, axis=0, keepdims=True)

        l8 = jnp.zeros((SUBLANES, nq), F32)
        for c in range(kend // rb):
            p = jnp.exp2(s_blk[c * rb:(c + 1) * rb, :] - m)
            p_blk[c * rb:(c + 1) * rb, :] = p.astype(BF16)
            l8 = l8 + fold8(p, jnp.add)
        l = jnp.sum(l8, axis=0, keepdims=True)

        o_t = _dot(vt_scr[:, :kend], p_blk[:kend, :]) / l
        o = o_t.T
        o_ref[q0:kend, :] = jnp.where(first, o[:tq], o[tq:]).astype(o_ref.dtype)

    @pl.when(step > 0)
    def _():
        for a in range(n_streams):
            write_back(a, step - 1).wait()

    for a in range(n_streams):
        fetch(a).wait()
        w_out[a][...] = w_in[a][...].astype(BF16)
        write_back(a, step).start()

    @pl.when(step == last_step)
    def _():
        for a in range(n_streams):
            write_back(a, step).wait()


def _bias_lane_selectors():
    pair = jnp.arange(HEADS // 2)[:, None, None, None]
    p = jnp.arange(3)[None, :, None, None]
    row = jnp.arange(LANES)[None, None, :, None]
    col = jnp.arange(LANES)[None, None, None, :]
    head_a = (row == 2 * pair) & ((col == p) | (col == 6 + p))
    head_b = (row == 2 * pair + 1) & ((col == 3 + p) | (col == 9 + p))
    return (head_a | head_b).astype(BF16)


def _attention(proj, c, qg, kg, hm, weights, layer, *, batch, seq):
    npair = HEADS // 2
    col0 = COL_QKV // LANES
    steps = batch * npair

    def part(p):
        return pl.BlockSpec((seq, LANES), lambda b, h: (b, col0 + p * npair + h))

    def slab_shape(w, rows, split):
        return (rows // steps, w.shape[2]) if split == "rows" else (rows, w.shape[2] // steps)

    slabs = [slab_shape(w, rows, split) for w, _, rows, split in weights]
    any_spec = pl.BlockSpec(memory_space=pl.ANY)
    vec = pl.BlockSpec((None, 1, LANES), lambda b, h: (layer, 0, 0))
    n = len(weights)
    return pl.pallas_call(
        functools.partial(_attn_kernel, streams=tuple((lyr, split) for _, lyr, _, split in weights)),
        grid=(batch, npair),
        in_specs=[part(0), part(1), part(2),
                  pl.BlockSpec((seq, LANES), lambda b, h: (b, 0)),
                  pl.BlockSpec((1, 3, LANES, LANES), lambda b, h: (h, 0, 0, 0)),
                  vec, vec, pl.BlockSpec((LANES, LANES), lambda b, h: (0, 0))] + [any_spec] * n,
        out_specs=[pl.BlockSpec((seq, LANES), lambda b, h: (b, h))] + [any_spec] * n,
        out_shape=[jax.ShapeDtypeStruct((batch * seq, WIDTH), BF16)]
                  + [jax.ShapeDtypeStruct((rows, w.shape[2]), BF16) for w, _, rows, _ in weights],
        scratch_shapes=[pltpu.VMEM((seq // ATTN_TQ, 2 * ATTN_TQ, 2 * LANES), BF16),
                        pltpu.VMEM((seq, 2 * LANES), BF16),
                        pltpu.VMEM((LANES, seq), BF16),
                        pltpu.VMEM((2, seq, 2 * ATTN_TQ), F32),
                        pltpu.VMEM((2, seq, 2 * ATTN_TQ), BF16)]
                       + [pltpu.VMEM(sh, F32) for sh in slabs] + [pltpu.VMEM(sh, BF16) for sh in slabs]
                       + [pltpu.SemaphoreType.DMA((n,)), pltpu.SemaphoreType.DMA((n,))],
        compiler_params=_params(("arbitrary", "arbitrary")),
        name="fox_attention",
    )(proj, proj, proj, c, _bias_lane_selectors(), qg, kg, hm, *[w for w, _, _, _ in weights])


def _merge_kernel(xn_ref, ya_ref, yb_ref, yc_ref, yd_ref, wg0_ref, wg1_ref, wg2_ref, wg3_ref, wb_ref, o_ref):
    xn = xn_ref[...]
    acc = None
    for k, (y_ref, wg_ref) in enumerate(((ya_ref, wg0_ref), (yb_ref, wg1_ref),
                                         (yc_ref, wg2_ref), (yd_ref, wg3_ref))):
        gate = jax.nn.sigmoid(_dot_nt(xn, wg_ref[...].astype(BF16)))
        term = gate * _dot(y_ref[...], wb_ref[0, k].astype(BF16))
        acc = term if acc is None else acc + term
    o_ref[...] = acc.astype(o_ref.dtype)


def _merge(xn, ys, w_in_t, wb, layer, *, tm, tn):
    t, d = xn.shape
    nblk = d // tn

    def wg_spec(k):
        return pl.BlockSpec((pl.Squeezed(), pl.Element(tn), pl.Element(d)),
                            lambda i, j: (layer, pl.multiple_of(N_MIX + k * d + j * tn, SUBLANES), 0))

    y_spec = pl.BlockSpec((tm, WIDTH), lambda i, j: (i, 0))
    return pl.pallas_call(
        _merge_kernel,
        grid=(t // tm, nblk),
        in_specs=[pl.BlockSpec((tm, d), lambda i, j: (i, 0)), y_spec, y_spec, y_spec, y_spec,
                  wg_spec(0), wg_spec(1), wg_spec(2), wg_spec(3),
                  pl.BlockSpec((1, N_BRANCH, WIDTH, tn), lambda i, j: (layer, 0, 0, j))],
        out_specs=pl.BlockSpec((tm, tn), lambda i, j: (i, j)),
        out_shape=jax.ShapeDtypeStruct((t, d), BF16),
        compiler_params=_params(("parallel", "arbitrary")),
        name="merge",
    )(xn, *ys, w_in_t, w_in_t, w_in_t, w_in_t, wb)


def _outproj_kernel(m_ref, x_ref, w_ref, g_ref, x1_ref, xn_ref):
    x1 = x_ref[...] + _dot(m_ref[...], w_ref[...])
    x1_ref[...] = x1
    ms = jnp.mean(x1 * x1, axis=-1, keepdims=True)
    xn_ref[...] = ((x1 * lax.rsqrt(ms + EPS)) * g_ref[...]).astype(xn_ref.dtype)


def _outproj(merged, x, w_out, g, layer, *, tm):
    t, d = x.shape
    row = pl.BlockSpec((tm, d), lambda i: (i, 0))
    return pl.pallas_call(
        _outproj_kernel,
        grid=(t // tm,),
        in_specs=[row, row,
                  pl.BlockSpec((d, d), lambda i: (0, 0), pipeline_mode=pl.Buffered(1)),
                  pl.BlockSpec((None, 1, d), lambda i: (layer, 0, 0))],
        out_specs=[row, row],
        out_shape=[jax.ShapeDtypeStruct((t, d), F32), jax.ShapeDtypeStruct((t, d), BF16)],
        compiler_params=_params(("parallel",)),
        name="outproj",
    )(merged, x, w_out, g)


def _mlp_kernel(xn_ref, x_ref, wu_ref, wd_ref, o_ref):
    @pl.when(pl.program_id(1) == 0)
    def _():
        o_ref[...] = x_ref[...]

    h = jnp.maximum(_dot(xn_ref[...], wu_ref[...]), 0.0)
    o_ref[...] += _dot((h * h).astype(BF16), wd_ref[...])


def _mlp(xn, x, wu, wd, *, tm, tf):
    t, d = x.shape
    f = wu.shape[1]
    row = pl.BlockSpec((tm, d), lambda i, j: (i, 0))
    return pl.pallas_call(
        _mlp_kernel,
        grid=(t // tm, f // tf),
        in_specs=[row, row,
                  pl.BlockSpec((d, tf), lambda i, j: (0, j)),
                  pl.BlockSpec((tf, d), lambda i, j: (j, 0))],
        out_specs=row,
        out_shape=jax.ShapeDtypeStruct((t, d), F32),
        compiler_params=_params(("parallel", "arbitrary")),
        name="mlp",
    )(xn, x, wu, wd)


def _block_diag_pairs(w):
    depth, n2, bd, _ = w.shape
    per = LANES // bd
    w5 = w.reshape(depth, n2 // per, per, bd, bd)
    eye = jnp.eye(per, dtype=w.dtype)
    return jnp.einsum('ljaik,ab->ljaibk', w5, eye).reshape(depth, n2 // per, LANES, LANES)


def _rows(v):
    return v.reshape(v.shape[0], 1, -1).astype(F32)


def kernel(x, norm_mix_g, w_in, lru_conv_w, lru_conv_b, lru_wr, lru_br, lru_wi, lru_bi, lru_lambda, pool_w, pool_scale, sconv_w, q_norm_g, k_norm_g, forget_b, w_branch, w_out, norm_mlp_g, w_mlp_up, w_mlp_down):
    batch, seq, d = x.shape
    assert d == D_MODEL and seq % ATTN_TQ == 0
    depth = w_in.shape[0]
    t = batch * seq
    xf = x.reshape(t, d)

    lane = jnp.arange(LANES)
    same_head = ((lane[:, None] // HEAD_DIM == lane[None, :] // HEAD_DIM) / HEAD_DIM).astype(BF16)

    w_in_t = jnp.swapaxes(w_in, 1, 2)
    w_mix = _cast_mix_weights(w_in_t, 0, rows=N_MIX_PAD // 8)

    g_mix, g_mlp = _rows(norm_mix_g), _rows(norm_mlp_g)
    lru_cb, lru_rb, lru_ib, lru_lam = _rows(lru_conv_b), _rows(lru_br), _rows(lru_bi), _rows(lru_lambda)
    lru_wr_bd = _block_diag_pairs(lru_wr).astype(BF16)
    lru_wi_bd = _block_diag_pairs(lru_wi).astype(BF16)
    pool_wb, pool_sc = pool_w.astype(BF16), _rows(pool_scale)
    fb = jnp.pad(_rows(forget_b), ((0, 0), (0, 0), (0, LANES - HEADS)))
    qg = jnp.tile(_rows(q_norm_g), (1, 1, LANES // HEAD_DIM))
    kg = jnp.tile(_rows(k_norm_g), (1, 1, LANES // HEAD_DIM))

    for l in range(depth):
        proj, xn = _inproj(xf, g_mix, w_mix, l, tm=512)
        y_a, y_b, y_c, c = _mixers(proj, lru_conv_w, lru_cb, lru_wr_bd, lru_rb, lru_wi_bd, lru_ib, lru_lam,
                                   pool_wb, pool_sc, sconv_w, fb, l, batch=batch, seq=seq)
        to_convert = [(w_mlp_up, l, d, "rows"), (w_mlp_down, l, D_FF, "rows"), (w_out, l, d, "rows")]
        if l + 1 < depth:
            to_convert.append((w_in_t, l + 1, N_MIX_PAD, "cols"))
        y_d, wu_bf, wd_bf, wo_bf, *next_mix = _attention(proj, c, qg, kg, same_head, tuple(to_convert), l,
                                                         batch=batch, seq=seq)
        merged = _merge(xn, (y_a, y_b, y_c, y_d), w_in_t, w_branch, l, tm=1024, tn=256)
        x1, xn2 = _outproj(merged, xf, wo_bf, g_mlp, l, tm=512)
        xf = _mlp(xn2, x1, wu_bf, wd_bf, tm=512, tf=2048)
        if next_mix:
            w_mix = next_mix[0]

    return xf.reshape(batch, seq, d)
```

```python
import functools

import jax
import jax.numpy as jnp
from jax import lax
from jax.experimental import pallas as pl
from jax.experimental.pallas import tpu as pltpu

F32 = jnp.float32
BF16 = jnp.bfloat16

D_MODEL = 2048
WIDTH = 512
N_BRANCH = 4
LRU_CONV = 4
LRU_C = 8.0
POOL_WINDOWS = (2, 4, 8, 16)
SCONV_K = 3
HEADS = 8
HEAD_DIM = 64
D_FF = 4 * D_MODEL
EPS = 1e-6

LANES = 128
SUBLANES = 8
VMEM_LIMIT = 56 * 1024 * 1024

N_MIX = 2 * WIDTH + 3 * WIDTH + 3 * WIDTH + HEADS
N_MIX_PAD = -(-N_MIX // LANES) * LANES
COL_LRU = 0
COL_POOL = WIDTH
COL_SCONV = 2 * WIDTH
COL_QKV = 5 * WIDTH
COL_FORGET = 8 * WIDTH


def _params(semantics):
    return pltpu.CompilerParams(dimension_semantics=semantics, vmem_limit_bytes=VMEM_LIMIT)


def _dot(a, b):
    return jnp.dot(a, b, preferred_element_type=F32)


def _dot_nt(a, b):
    return lax.dot_general(a, b, (((1,), (1,)), ((), ())), preferred_element_type=F32)


def _shift_rows(x, k, fill):
    n, c = x.shape
    if k >= n:
        return jnp.full((n, c), fill, x.dtype)
    if k % SUBLANES == 0:
        return jnp.concatenate([jnp.full((k, c), fill, x.dtype), x[:n - k]], axis=0)
    r = pltpu.roll(x, k, 0)
    row = lax.broadcasted_iota(jnp.int32, (SUBLANES, c), 0)
    top = jnp.where(row < k, jnp.asarray(fill, x.dtype), r[:SUBLANES])
    return jnp.concatenate([top, r[SUBLANES:]], axis=0)


SCAN_PAD = 4


def _scan_scratch_rows(seq):
    return seq + SUBLANES * SCAN_PAD


def _chunk_rows(tau, stride):
    return pl.ds(tau, SUBLANES, stride=stride)


def _linear_scan_rows(a, b, a_scr, b_scr):
    n, c = a.shape
    length = n // SUBLANES
    stride = length + SCAN_PAD
    for ch in range(SUBLANES):
        a_scr[ch * stride:ch * stride + length, :] = a[ch * length:(ch + 1) * length]
        b_scr[ch * stride:ch * stride + length, :] = b[ch * length:(ch + 1) * length]

    h = b_scr[_chunk_rows(0, stride), :]
    prod = a_scr[_chunk_rows(0, stride), :]
    for tau in range(1, length):
        rows = _chunk_rows(tau, stride)
        at = a_scr[rows, :]
        h = at * h + b_scr[rows, :]
        prod = at * prod
        b_scr[rows, :] = h
        a_scr[rows, :] = prod

    out = []
    state = jnp.zeros((1, c), F32)
    for ch in range(SUBLANES):
        lo = ch * stride
        out.append(b_scr[lo:lo + length, :] + a_scr[lo:lo + length, :] * state)
        state = h[ch:ch + 1] + prod[ch:ch + 1] * state
    return jnp.concatenate(out, axis=0)


def _cumsum_rows(x, scr):
    n, c = x.shape
    length = n // SUBLANES
    stride = length + SCAN_PAD
    for ch in range(SUBLANES):
        scr[ch * stride:ch * stride + length, :] = x[ch * length:(ch + 1) * length]
    h = scr[_chunk_rows(0, stride), :]
    for tau in range(1, length):
        rows = _chunk_rows(tau, stride)
        h = h + scr[rows, :]
        scr[rows, :] = h
    out = []
    state = jnp.zeros((1, c), F32)
    for ch in range(SUBLANES):
        lo = ch * stride
        out.append(scr[lo:lo + length, :] + state)
        state = state + h[ch:ch + 1]
    return jnp.concatenate(out, axis=0)


def _softplus(x):
    return jnp.maximum(x, 0.0) + jnp.log1p(jnp.exp(-jnp.abs(x)))


MIX_STAGE_ROWS = N_MIX_PAD // 12


def _inproj_kernel(x_ref, g_ref, w_ref, proj_ref, xn_ref):
    x = x_ref[...]
    ms = jnp.mean(x * x, axis=-1, keepdims=True)
    xn = ((x * lax.rsqrt(ms + EPS)) * g_ref[...]).astype(xn_ref.dtype)
    xn_ref[...] = xn
    proj_ref[...] = _dot_nt(xn, w_ref[...])


def _inproj_convert_kernel(x_ref, g_ref, w_hbm, proj_ref, xn_ref, w_scr, stage, sem, *, layer):
    rows = stage.shape[1]
    n_chunks = w_scr.shape[0] // rows

    def fetch(c):
        return pltpu.make_async_copy(w_hbm.at[layer, pl.ds(c * rows, rows)], stage.at[c % 2], sem.at[c % 2])

    @pl.when(pl.program_id(0) == 0)
    def _():
        fetch(0).start()
        for c in range(n_chunks):
            if c + 1 < n_chunks:
                fetch(c + 1).start()
            fetch(c).wait()
            w_scr[c * rows:(c + 1) * rows, :] = stage[c % 2].astype(BF16)

    _inproj_kernel(x_ref, g_ref, w_scr, proj_ref, xn_ref)


def _inproj(x, g, w_mix, layer, *, tm):
    t, d = x.shape
    n = N_MIX_PAD
    row = pl.BlockSpec((tm, d), lambda i: (i, 0))
    gain = pl.BlockSpec((None, 1, d), lambda i: (layer, 0, 0))
    out_specs = [pl.BlockSpec((tm, n), lambda i: (i, 0)), row]
    out_shape = [jax.ShapeDtypeStruct((t, n), F32), jax.ShapeDtypeStruct((t, d), BF16)]
    if w_mix.dtype == BF16:
        return pl.pallas_call(
            _inproj_kernel,
            grid=(t // tm,),
            in_specs=[row, gain, pl.BlockSpec((n, d), lambda i: (0, 0), pipeline_mode=pl.Buffered(1))],
            out_specs=out_specs,
            out_shape=out_shape,
            compiler_params=_params(("parallel",)),
            name="inproj",
        )(x, g, w_mix)
    return pl.pallas_call(
        functools.partial(_inproj_convert_kernel, layer=layer),
        grid=(t // tm,),
        in_specs=[row, gain, pl.BlockSpec(memory_space=pl.ANY)],
        out_specs=out_specs,
        out_shape=out_shape,
        scratch_shapes=[pltpu.VMEM((n, d), BF16), pltpu.VMEM((2, MIX_STAGE_ROWS, d), F32),
                        pltpu.SemaphoreType.DMA((2,))],
        compiler_params=_params(("arbitrary",)),
        name="inproj_convert",
    )(x, g, w_mix)


def _lru_slab(x, cw, cb, wr, br, wi, bi, lam, a_scr, b_scr):
    u = cw[LRU_CONV - 1:LRU_CONV] * x + cb
    for k in range(1, LRU_CONV):
        u = u + cw[LRU_CONV - 1 - k:LRU_CONV - k] * _shift_rows(x, k, 0.0)
    ub = u.astype(BF16)
    r = jax.nn.sigmoid(_dot(ub, wr) + br)
    gi = jax.nn.sigmoid(_dot(ub, wi) + bi)
    log_a = (-LRU_C * r) * _softplus(-lam)
    a = jnp.exp(log_a)
    z = -jnp.tanh(log_a) * (a * a + 1.0)
    root = jnp.where(z > 0.0, z * lax.rsqrt(z), 0.0)
    inp = root * (gi * u)
    return _linear_scan_rows(a, inp, a_scr, b_scr)


def _pool_slab(x, group, w, scale):
    w2 = x + _shift_rows(x, 1, 0.0)
    w4 = w2 + _shift_rows(w2, 2, 0.0)
    w8 = w4 + _shift_rows(w4, 4, 0.0)
    w16 = w8 + _shift_rows(w8, 8, 0.0)
    wsum = jnp.where(group == 0, w2, jnp.where(group == 1, w4, jnp.where(group == 2, w8, w16)))
    win = jnp.left_shift(2, group)
    inv_win = jnp.where(group == 0, 0.5, jnp.where(group == 1, 0.25, jnp.where(group == 2, 0.125, 0.0625)))
    head = max(POOL_WINDOWS)
    t = lax.broadcasted_iota(jnp.int32, (head, x.shape[1]), 0)
    count = jnp.minimum(t + 1, win).astype(F32)
    pooled = jnp.concatenate([wsum[:head] / count - x[:head], wsum[head:] * inv_win - x[head:]], axis=0)
    return _dot(pooled.astype(BF16), w) * scale


def _sconv_slab(gate_b, gate_c, xc, w):
    z = gate_c * xc
    acc = w[SCONV_K - 1:SCONV_K] * z
    for k in range(1, SCONV_K):
        acc = acc + w[SCONV_K - 1 - k:SCONV_K - k] * _shift_rows(z, k, 0.0)
    return gate_b * acc


def _mixers_kernel(xa_ref, xp_ref, gb_ref, gc_ref, xc_ref, f_ref,
                   cw_ref, cb_ref, wr_ref, br_ref, wi_ref, bi_ref, lam_ref,
                   pw_ref, ps_ref, sw_ref, fb_ref,
                   ya_ref, yb_ref, yc_ref, c_ref, a_scr, b_scr):
    slab = pl.program_id(1)
    ya_ref[...] = _lru_slab(xa_ref[...], cw_ref[...], cb_ref[...], wr_ref[0], br_ref[...],
                            wi_ref[0], bi_ref[...], lam_ref[...], a_scr, b_scr).astype(ya_ref.dtype)
    yb_ref[...] = _pool_slab(xp_ref[...], slab, pw_ref[0], ps_ref[...]).astype(yb_ref.dtype)
    yc_ref[...] = _sconv_slab(gb_ref[...], gc_ref[...], xc_ref[...], sw_ref[...]).astype(yc_ref.dtype)

    @pl.when(slab == 0)
    def _():
        z = f_ref[...] + fb_ref[...]
        log_f = jnp.minimum(z, 0.0) - jnp.log1p(jnp.exp(-jnp.abs(z)))
        head_lane = lax.broadcasted_iota(jnp.int32, (1, LANES), 1) < HEADS
        c_ref[...] = jnp.where(head_lane, _cumsum_rows(log_f, b_scr), 0.0)


def _mixers(proj, cw, cb, wr_bd, br, wi_bd, bi, lam, pool_w, pool_scale, sconv_w, fb, layer, *, batch, seq):
    nc = WIDTH // LANES

    def col(c0, p=0):
        return pl.BlockSpec((seq, LANES), lambda b, c: (b, (c0 + p * WIDTH) // LANES + c))

    vec = pl.BlockSpec((None, 1, LANES), lambda b, c: (layer, 0, c))
    mat = pl.BlockSpec((None, 1, LANES, LANES), lambda b, c: (layer, c, 0, 0))
    y_spec = pl.BlockSpec((seq, LANES), lambda b, c: (b, c))
    y_shape = jax.ShapeDtypeStruct((batch * seq, WIDTH), BF16)
    return pl.pallas_call(
        _mixers_kernel,
        grid=(batch, nc),
        in_specs=[col(COL_LRU), col(COL_POOL), col(COL_SCONV, 0), col(COL_SCONV, 1), col(COL_SCONV, 2),
                  pl.BlockSpec((seq, LANES), lambda b, c: (b, COL_FORGET // LANES)),
                  pl.BlockSpec((None, LRU_CONV, LANES), lambda b, c: (layer, 0, c)),
                  vec, mat, vec, mat, vec, vec,
                  mat, vec,
                  pl.BlockSpec((None, SCONV_K, LANES), lambda b, c: (layer, 0, c)),
                  pl.BlockSpec((None, 1, LANES), lambda b, c: (layer, 0, 0))],
        out_specs=[y_spec, y_spec, y_spec, pl.BlockSpec((seq, LANES), lambda b, c: (b, 0))],
        out_shape=[y_shape, y_shape, y_shape, jax.ShapeDtypeStruct((batch * seq, LANES), F32)],
        scratch_shapes=[pltpu.VMEM((_scan_scratch_rows(seq), LANES), F32)] * 2,
        compiler_params=_params(("parallel", "arbitrary")),
        name="mixers",
    )(proj, proj, proj, proj, proj, proj, cw, cb, wr_bd, br, wi_bd, bi, lam, pool_w, pool_scale, sconv_w, fb)


ATTN_TQ = 512
LOG2E = 1.4426950408889634
ATTN_ROWS = 64
ATTN_BIAS_LANES = 12


def _attn_kernel(*refs, streams):
    n_streams = len(streams)
    q_ref, k_ref, v_ref, c_ref, sel_ref, qg_ref, kg_ref, hm_ref = refs[:8]
    w_src = refs[8:8 + n_streams]
    o_ref = refs[8 + n_streams]
    w_dst = refs[9 + n_streams:9 + 2 * n_streams]
    q_scr, k_scr, vt_scr, s_scr, p_scr = refs[9 + 2 * n_streams:14 + 2 * n_streams]
    w_in = refs[14 + 2 * n_streams:14 + 3 * n_streams]
    w_out = refs[14 + 3 * n_streams:14 + 4 * n_streams]
    sem_in, sem_out = refs[14 + 4 * n_streams:]

    step = pl.program_id(0) * pl.num_programs(1) + pl.program_id(1)
    last_step = pl.num_programs(0) * pl.num_programs(1) - 1

    def slab(a, at_step):
        rows, cols = w_in[a].shape
        if streams[a][1] == "cols":
            return (pl.ds(0, rows), pl.ds(pl.multiple_of(at_step * cols, cols), cols))
        return (pl.ds(pl.multiple_of(at_step * rows, rows), rows), pl.ds(0, cols))

    def fetch(a):
        return pltpu.make_async_copy(w_src[a].at[(streams[a][0],) + slab(a, step)], w_in[a], sem_in.at[a])

    def write_back(a, at_step):
        return pltpu.make_async_copy(w_out[a], w_dst[a].at[slab(a, at_step)], sem_out.at[a])

    for a in range(n_streams):
        fetch(a).start()

    seq = q_ref.shape[0]
    hm = hm_ref[...]

    def head_norm(x, g):
        sq = x * x
        hi = sq.astype(BF16)
        lo = (sq - hi.astype(F32)).astype(BF16)
        ms = _dot(hi, hm) + _dot(lo, hm)
        return (x * lax.rsqrt(ms + EPS)) * g

    tq = ATTN_TQ
    nblk = seq // tq
    qg, kg = qg_ref[...] * (HEAD_DIM ** -0.5 * LOG2E), kg_ref[...]
    lane = lax.broadcasted_iota(jnp.int32, (1, LANES), 1)
    first = lane < HEAD_DIM

    def lane_in(lo, hi):
        return (lane >= lo) & (lane < hi)

    sel_hi, sel_mid, sel_lo = sel_ref[0, 0], sel_ref[0, 1], sel_ref[0, 2]

    for j in range(nblk):
        rows = slice(j * tq, (j + 1) * tq)
        c = c_ref[rows, :] * LOG2E
        hi = c.astype(BF16)
        mid = (c - hi.astype(F32)).astype(BF16)
        lo = ((c - hi.astype(F32)) - mid.astype(F32)).astype(BF16)
        pieces = _dot(hi, sel_hi) + _dot(mid, sel_mid) + _dot(lo, sel_lo)
        k_extra = jnp.where(lane_in(0, 6), -pieces, jnp.where(lane_in(6, ATTN_BIAS_LANES), 1.0, 0.0))
        qa_extra = jnp.where(lane_in(0, 3), 1.0, jnp.where(lane_in(6, 9), pieces, 0.0))
        qb_extra = jnp.where(lane_in(3, 6), 1.0, jnp.where(lane_in(9, ATTN_BIAS_LANES), pieces, 0.0))
        qn = head_norm(q_ref[rows, :], qg)
        q_scr[j, :tq, :LANES] = jnp.where(first, qn, 0.0).astype(BF16)
        q_scr[j, :tq, LANES:] = qa_extra.astype(BF16)
        q_scr[j, tq:, :LANES] = jnp.where(first, 0.0, qn).astype(BF16)
        q_scr[j, tq:, LANES:] = qb_extra.astype(BF16)
        k_scr[rows, :LANES] = head_norm(k_ref[rows, :], kg).astype(BF16)
        k_scr[rows, LANES:] = k_extra.astype(BF16)
        vt_scr[:, rows] = v_ref[rows, :].T.astype(BF16)

    rb = ATTN_ROWS
    nq = 2 * tq
    key_pos = lax.broadcasted_iota(jnp.int32, (tq, nq), 0)
    lane_q = lax.broadcasted_iota(jnp.int32, (tq, nq), 1)
    causal = key_pos <= jnp.where(lane_q < tq, lane_q, lane_q - tq)

    def fold8(x, op):
        out = x[:SUBLANES]
        for g in range(1, x.shape[0] // SUBLANES):
            out = op(out, x[g * SUBLANES:(g + 1) * SUBLANES])
        return out

    for i in range(nblk):
        q0, kend = i * tq, (i + 1) * tq
        s_blk = s_scr.at[i % 2]
        p_blk = p_scr.at[i % 2]

        m8 = None
        for kb in range(i + 1):
            rows = slice(kb * tq, (kb + 1) * tq)
            s = _dot_nt(k_scr[rows, :], q_scr[i])
            if kb == i:
                s = jnp.where(causal, s, -jnp.inf)
            s_blk[rows, :] = s
            part = fold8(s, jnp.maximum)
            m8 = part if m8 is None else jnp.maximum(m8, part)
        m = jnp.max(m8, axis=0, keepdims=True)

        l8 = jnp.zeros((SUBLANES, nq), F32)
        for c in range(kend // rb):
            p = jnp.exp2(s_blk[c * rb:(c + 1) * rb, :] - m)
            p_blk[c * rb:(c + 1) * rb, :] = p.astype(BF16)
            l8 = l8 + fold8(p, jnp.add)
        l = jnp.sum(l8, axis=0, keepdims=True)

        o_t = _dot(vt_scr[:, :kend], p_blk[:kend, :]) / l
        o = o_t.T
        o_ref[q0:kend, :] = jnp.where(first, o[:tq], o[tq:]).astype(o_ref.dtype)

    @pl.when(step > 0)
    def _():
        for a in range(n_streams):
            write_back(a, step - 1).wait()

    for a in range(n_streams):
        fetch(a).wait()
        w_out[a][...] = w_in[a][...].astype(BF16)
        write_back(a, step).start()

    @pl.when(step == last_step)
    def _():
        for a in range(n_streams):
            write_back(a, step).wait()


def _bias_lane_selectors():
    pair = jnp.arange(HEADS // 2)[:, None, None, None]
    p = jnp.arange(3)[None, :, None, None]
    row = jnp.arange(LANES)[None, None, :, None]
    col = jnp.arange(LANES)[None, None, None, :]
    head_a = (row == 2 * pair) & ((col == p) | (col == 6 + p))
    head_b = (row == 2 * pair + 1) & ((col == 3 + p) | (col == 9 + p))
    return (head_a | head_b).astype(BF16)


def _attention(proj, c, qg, kg, hm, weights, layer, *, batch, seq):
    npair = HEADS // 2
    col0 = COL_QKV // LANES
    steps = batch * npair

    def part(p):
        return pl.BlockSpec((seq, LANES), lambda b, h: (b, col0 + p * npair + h))

    def slab_shape(w, rows, split):
        return (rows // steps, w.shape[2]) if split == "rows" else (rows, w.shape[2] // steps)

    slabs = [slab_shape(w, rows, split) for w, _, rows, split in weights]
    any_spec = pl.BlockSpec(memory_space=pl.ANY)
    vec = pl.BlockSpec((None, 1, LANES), lambda b, h: (layer, 0, 0))
    n = len(weights)
    return pl.pallas_call(
        functools.partial(_attn_kernel, streams=tuple((lyr, split) for _, lyr, _, split in weights)),
        grid=(batch, npair),
        in_specs=[part(0), part(1), part(2),
                  pl.BlockSpec((seq, LANES), lambda b, h: (b, 0)),
                  pl.BlockSpec((1, 3, LANES, LANES), lambda b, h: (h, 0, 0, 0)),
                  vec, vec, pl.BlockSpec((LANES, LANES), lambda b, h: (0, 0))] + [any_spec] * n,
        out_specs=[pl.BlockSpec((seq, LANES), lambda b, h: (b, h))] + [any_spec] * n,
        out_shape=[jax.ShapeDtypeStruct((batch * seq, WIDTH), BF16)]
                  + [jax.ShapeDtypeStruct((rows, w.shape[2]), BF16) for w, _, rows, _ in weights],
        scratch_shapes=[pltpu.VMEM((seq // ATTN_TQ, 2 * ATTN_TQ, 2 * LANES), BF16),
                        pltpu.VMEM((seq, 2 * LANES), BF16),
                        pltpu.VMEM((LANES, seq), BF16),
                        pltpu.VMEM((2, seq, 2 * ATTN_TQ), F32),
                        pltpu.VMEM((2, seq, 2 * ATTN_TQ), BF16)]
                       + [pltpu.VMEM(sh, F32) for sh in slabs] + [pltpu.VMEM(sh, BF16) for sh in slabs]
                       + [pltpu.SemaphoreType.DMA((n,)), pltpu.SemaphoreType.DMA((n,))],
        compiler_params=_params(("arbitrary", "arbitrary")),
        name="fox_attention",
    )(proj, proj, proj, c, _bias_lane_selectors(), qg, kg, hm, *[w for w, _, _, _ in weights])


def _merge_kernel(xn_ref, ya_ref, yb_ref, yc_ref, yd_ref, wg0_ref, wg1_ref, wg2_ref, wg3_ref, wb_ref, o_ref):
    xn = xn_ref[...]
    acc = None
    for k, (y_ref, wg_ref) in enumerate(((ya_ref, wg0_ref), (yb_ref, wg1_ref),
                                         (yc_ref, wg2_ref), (yd_ref, wg3_ref))):
        gate = jax.nn.sigmoid(_dot_nt(xn, wg_ref[...].astype(BF16)))
        term = gate * _dot(y_ref[...], wb_ref[k])
        acc = term if acc is None else acc + term
    o_ref[...] = acc.astype(o_ref.dtype)


def _merge(xn, ys, w_in_t, wb, layer, *, tm, tn):
    t, d = xn.shape
    nblk = d // tn

    def wg_spec(k):
        return pl.BlockSpec((pl.Squeezed(), pl.Element(tn), pl.Element(d)),
                            lambda i, j: (layer, pl.multiple_of(N_MIX + k * d + j * tn, SUBLANES), 0))

    y_spec = pl.BlockSpec((tm, WIDTH), lambda i, j: (i, 0))
    return pl.pallas_call(
        _merge_kernel,
        grid=(t // tm, nblk),
        in_specs=[pl.BlockSpec((tm, d), lambda i, j: (i, 0)), y_spec, y_spec, y_spec, y_spec,
                  wg_spec(0), wg_spec(1), wg_spec(2), wg_spec(3),
                  pl.BlockSpec((N_BRANCH, WIDTH, tn), lambda i, j: (0, 0, j))],
        out_specs=pl.BlockSpec((tm, tn), lambda i, j: (i, j)),
        out_shape=jax.ShapeDtypeStruct((t, d), BF16),
        compiler_params=_params(("parallel", "arbitrary")),
        name="merge",
    )(xn, *ys, w_in_t, w_in_t, w_in_t, w_in_t, wb)


def _outproj_kernel(m_ref, x_ref, w_ref, g_ref, x1_ref, xn_ref):
    x1 = x_ref[...] + _dot(m_ref[...], w_ref[...])
    x1_ref[...] = x1
    ms = jnp.mean(x1 * x1, axis=-1, keepdims=True)
    xn_ref[...] = ((x1 * lax.rsqrt(ms + EPS)) * g_ref[...]).astype(xn_ref.dtype)


def _outproj(merged, x, w_out, g, layer, *, tm):
    t, d = x.shape
    row = pl.BlockSpec((tm, d), lambda i: (i, 0))
    return pl.pallas_call(
        _outproj_kernel,
        grid=(t // tm,),
        in_specs=[row, row,
                  pl.BlockSpec((d, d), lambda i: (0, 0), pipeline_mode=pl.Buffered(1)),
                  pl.BlockSpec((None, 1, d), lambda i: (layer, 0, 0))],
        out_specs=[row, row],
        out_shape=[jax.ShapeDtypeStruct((t, d), F32), jax.ShapeDtypeStruct((t, d), BF16)],
        compiler_params=_params(("parallel",)),
        name="outproj",
    )(merged, x, w_out, g)


def _mlp_kernel(xn_ref, x_ref, wu_ref, wd_ref, o_ref):
    @pl.when(pl.program_id(1) == 0)
    def _():
        o_ref[...] = x_ref[...]

    h = jnp.maximum(_dot(xn_ref[...], wu_ref[...]), 0.0)
    o_ref[...] += _dot((h * h).astype(BF16), wd_ref[...])


def _mlp(xn, x, wu, wd, *, tm, tf):
    t, d = x.shape
    f = wu.shape[1]
    row = pl.BlockSpec((tm, d), lambda i, j: (i, 0))
    return pl.pallas_call(
        _mlp_kernel,
        grid=(t // tm, f // tf),
        in_specs=[row, row,
                  pl.BlockSpec((d, tf), lambda i, j: (0, j)),
                  pl.BlockSpec((tf, d), lambda i, j: (j, 0))],
        out_specs=row,
        out_shape=jax.ShapeDtypeStruct((t, d), F32),
        compiler_params=_params(("parallel", "arbitrary")),
        name="mlp",
    )(xn, x, wu, wd)


def _block_diag_pairs(w):
    depth, n2, bd, _ = w.shape
    per = LANES // bd
    w5 = w.reshape(depth, n2 // per, per, bd, bd)
    eye = jnp.eye(per, dtype=w.dtype)
    return jnp.einsum('ljaik,ab->ljaibk', w5, eye).reshape(depth, n2 // per, LANES, LANES)


def _rows(v):
    return v.reshape(v.shape[0], 1, -1).astype(F32)


def kernel(x, norm_mix_g, w_in, lru_conv_w, lru_conv_b, lru_wr, lru_br, lru_wi, lru_bi, lru_lambda, pool_w, pool_scale, sconv_w, q_norm_g, k_norm_g, forget_b, w_branch, w_out, norm_mlp_g, w_mlp_up, w_mlp_down):
    batch, seq, d = x.shape
    assert d == D_MODEL and seq % ATTN_TQ == 0
    depth = w_in.shape[0]
    t = batch * seq
    xf = x.reshape(t, d)

    lane = jnp.arange(LANES)
    same_head = ((lane[:, None] // HEAD_DIM == lane[None, :] // HEAD_DIM) / HEAD_DIM).astype(BF16)

    w_in_t = jnp.swapaxes(w_in, 1, 2)
    w_mix = w_in_t

    g_mix, g_mlp = _rows(norm_mix_g), _rows(norm_mlp_g)
    lru_cb, lru_rb, lru_ib, lru_lam = _rows(lru_conv_b), _rows(lru_br), _rows(lru_bi), _rows(lru_lambda)
    lru_wr_bd = _block_diag_pairs(lru_wr).astype(BF16)
    lru_wi_bd = _block_diag_pairs(lru_wi).astype(BF16)
    pool_wb, pool_sc = pool_w.astype(BF16), _rows(pool_scale)
    fb = jnp.pad(_rows(forget_b), ((0, 0), (0, 0), (0, LANES - HEADS)))
    qg = jnp.tile(_rows(q_norm_g), (1, 1, LANES // HEAD_DIM))
    kg = jnp.tile(_rows(k_norm_g), (1, 1, LANES // HEAD_DIM))

    w_branch_rows = w_branch.reshape(depth, N_BRANCH * WIDTH, d)

    for l in range(depth):
        proj, xn = _inproj(xf, g_mix, w_mix, l, tm=512)
        y_a, y_b, y_c, c = _mixers(proj, lru_conv_w, lru_cb, lru_wr_bd, lru_rb, lru_wi_bd, lru_ib, lru_lam,
                                   pool_wb, pool_sc, sconv_w, fb, l, batch=batch, seq=seq)
        to_convert = [(w_mlp_up, l, d, "rows"), (w_mlp_down, l, D_FF, "rows"), (w_out, l, d, "rows"),
                      (w_branch_rows, l, N_BRANCH * WIDTH, "rows")]
        if l + 1 < depth:
            to_convert.append((w_in_t, l + 1, N_MIX_PAD, "cols"))
        y_d, wu_bf, wd_bf, wo_bf, wb_bf, *next_mix = _attention(proj, c, qg, kg, same_head, tuple(to_convert), l,
                                                                batch=batch, seq=seq)
        merged = _merge(xn, (y_a, y_b, y_c, y_d), w_in_t, wb_bf.reshape(N_BRANCH, WIDTH, d), l, tm=1024, tn=256)
        x1, xn2 = _outproj(merged, xf, wo_bf, g_mlp, l, tm=512)
        xf = _mlp(xn2, x1, wu_bf, wd_bf, tm=512, tf=2048)
        if next_mix:
            w_mix = next_mix[0]

    return xf.reshape(batch, seq, d)
```
